```python
import jax, jax.numpy as jnp
from jax import lax
import numpy as np

D_MODEL = 2048
BATCH = 2
SEQ = 4096
DEPTH = 1

POOL_WIDTH = D_MODEL
POOL_WINDOWS = (2, 4, 8, 16)
N_POOL_GROUPS = len(POOL_WINDOWS)
POOL_GROUP_WIDTH = POOL_WIDTH // N_POOL_GROUPS
LRU_WIDTH = D_MODEL
LRU_BLOCK = 256
N_LRU_HEADS = LRU_WIDTH // LRU_BLOCK
CONV_WIDTH = 4
LRU_C = 8.0
N_DIRS = 2
N_BRANCHES = 2
D_FF = 4 * D_MODEL
IN_WIDTH = POOL_WIDTH + 2 * LRU_WIDTH + N_BRANCHES * D_MODEL
DN_ALPHA = (2.0 * DEPTH) ** 0.25
DN_BETA = (8.0 * DEPTH) ** -0.25
LN_EPS = 1e-5

kernel_name = "hybrid_pool_rglru_encoder_block"


def layer_norm(x, g, b):
    xf = x.astype(jnp.float32)
    mu = jnp.mean(xf, axis=-1, keepdims=True)
    xc = xf - mu
    var = jnp.mean(xc * xc, axis=-1, keepdims=True)
    y = xc * lax.rsqrt(var + LN_EPS) * g.astype(jnp.float32) + b.astype(jnp.float32)
    return y.astype(x.dtype)


def multiscale_pool(u, pool_w, pool_scale):
    B, S, P = u.shape
    uf = u.astype(jnp.float32)
    csum = jnp.pad(jnp.cumsum(uf, axis=1), ((0, 0), (1, 0), (0, 0)))
    t = jnp.arange(S)
    outs = []
    for g, w in enumerate(POOL_WINDOWS):
        lo = jnp.clip(t - w // 2, 0, S)
        hi = jnp.clip(t + w // 2, 0, S)
        sl = slice(g * POOL_GROUP_WIDTH, (g + 1) * POOL_GROUP_WIDTH)
        c = csum[:, :, sl]
        mean = (c[:, hi] - c[:, lo]) / (hi - lo).astype(jnp.float32)[None, :, None]
        outs.append(mean - uf[:, :, sl])
    d = jnp.stack(outs, axis=2)
    y = jnp.einsum('bsgi,gio->bsgo', d, pool_w.astype(jnp.float32)).reshape(B, S, P)
    return (y * pool_scale.astype(jnp.float32)).astype(u.dtype)


def centred_depthwise_conv(u, w, b):
    S = u.shape[1]
    left = CONV_WIDTH // 2
    right = CONV_WIDTH - 1 - left
    up = jnp.pad(u, ((0, 0), (left, right), (0, 0)))
    y = b
    for k in range(CONV_WIDTH):
        y = y + up[:, k:k + S, :] * w[k]
    return y


def _lin_combine(p, q):
    a1, b1 = p
    a2, b2 = q
    return a1 * a2, a2 * b1 + b2


def rg_lru(xc, wa, ba, wx, bx, lam, reverse):
    B, S, R = xc.shape
    xf = xc.astype(jnp.float32)
    xh = xf.reshape(B, S, N_LRU_HEADS, LRU_BLOCK)
    r = jax.nn.sigmoid(jnp.einsum('bshi,hio->bsho', xh, wa.astype(jnp.float32)).reshape(B, S, R) + ba.astype(jnp.float32))
    i = jax.nn.sigmoid(jnp.einsum('bshi,hio->bsho', xh, wx.astype(jnp.float32)).reshape(B, S, R) + bx.astype(jnp.float32))
    log_a = -LRU_C * jax.nn.softplus(-lam.astype(jnp.float32)) * r
    a = jnp.exp(log_a)
    inp = jnp.sqrt(-jnp.expm1(2.0 * log_a)) * (i * xf)
    _, h = lax.associative_scan(_lin_combine, (a, inp), axis=1, reverse=reverse)
    return h


def hybrid_mixer(x, w_in, pool_w, pool_scale, conv_w, conv_b, lru_wa, lru_ba, lru_wx, lru_bx,
                 lru_lambda, w_pool_up, w_lru_up, w_out, b_out):
    B, S, D = x.shape
    z = jnp.einsum('bsd,de->bse', x, w_in)
    o1 = POOL_WIDTH
    o2 = o1 + LRU_WIDTH
    o3 = o2 + LRU_WIDTH
    u_pool, u_lru, u_gate, g_logits = z[..., :o1], z[..., o1:o2], z[..., o2:o3], z[..., o3:]
    y_pool = multiscale_pool(u_pool, pool_w, pool_scale)
    xc = centred_depthwise_conv(u_lru, conv_w, conv_b)
    h = (rg_lru(xc, lru_wa[0], lru_ba[0], lru_wx[0], lru_bx[0], lru_lambda[0], False)
         + rg_lru(xc, lru_wa[1], lru_ba[1], lru_wx[1], lru_bx[1], lru_lambda[1], True))
    y_lru = h.astype(x.dtype) * jax.nn.gelu(u_gate)
    g = jax.nn.sigmoid(g_logits.astype(jnp.float32)).astype(x.dtype).reshape(B, S, N_BRANCHES, D)
    m = (g[:, :, 0] * jnp.einsum('bsp,pd->bsd', y_pool, w_pool_up)
         + g[:, :, 1] * jnp.einsum('bsr,rd->bsd', y_lru, w_lru_up))
    return jnp.einsum('bsd,de->bse', m, w_out) + b_out


def sq_relu_mlp(x, w1, b1, w2, b2):
    hdn = jnp.square(jax.nn.relu(jnp.einsum('bsd,df->bsf', x, w1) + b1))
    return jnp.einsum('bsf,fd->bsd', hdn, w2) + b2


def setup_inputs(seed: int = 0) -> dict:
    key = jax.random.key(seed)
    ks = jax.random.split(key, 24)
    f32 = jnp.float32
    L, D, P, R = DEPTH, D_MODEL, POOL_WIDTH, LRU_WIDTH
    nrm = lambda k, shape, s: jax.random.normal(k, shape, f32) * s
    a_base = jax.random.uniform(ks[10], (L, N_DIRS, R), f32, minval=0.9, maxval=0.999)
    s = a_base ** (1.0 / LRU_C)
    lam = jnp.log(s) - jnp.log1p(-s)
    return {
        "x": nrm(ks[0], (BATCH, SEQ, D), 1.0),
        "w_in": nrm(ks[1], (L, D, IN_WIDTH), D ** -0.5),
        "pool_w": nrm(ks[2], (L, N_POOL_GROUPS, POOL_GROUP_WIDTH, POOL_GROUP_WIDTH), POOL_GROUP_WIDTH ** -0.5),
        "pool_scale": 1.0 + nrm(ks[3], (L, P), 0.1),
        "conv_w": nrm(ks[4], (L, CONV_WIDTH, R), CONV_WIDTH ** -0.5),
        "conv_b": nrm(ks[5], (L, R), 0.01),
        "lru_wa": nrm(ks[6], (L, N_DIRS, N_LRU_HEADS, LRU_BLOCK, LRU_BLOCK), LRU_BLOCK ** -0.5),
        "lru_ba": nrm(ks[7], (L, N_DIRS, R), 0.01),
        "lru_wx": nrm(ks[8], (L, N_DIRS, N_LRU_HEADS, LRU_BLOCK, LRU_BLOCK), LRU_BLOCK ** -0.5),
        "lru_bx": nrm(ks[9], (L, N_DIRS, R), 0.01),
        "lru_lambda": lam,
        "w_pool_up": nrm(ks[11], (L, P, D), DN_BETA * P ** -0.5),
        "w_lru_up": nrm(ks[12], (L, R, D), DN_BETA * R ** -0.5),
        "w_out": nrm(ks[13], (L, D, D), DN_BETA * D ** -0.5),
        "b_out": nrm(ks[14], (L, D), 0.01),
        "ln1_g": 1.0 + nrm(ks[15], (L, D), 0.1),
        "ln1_b": nrm(ks[16], (L, D), 0.01),
        "w_ff1": nrm(ks[17], (L, D, D_FF), D ** -0.5),
        "b_ff1": nrm(ks[18], (L, D_FF), 0.01),
        "w_ff2": nrm(ks[19], (L, D_FF, D), DN_BETA * D_FF ** -0.5),
        "b_ff2": nrm(ks[20], (L, D), 0.01),
        "ln2_g": 1.0 + nrm(ks[21], (L, D), 0.1),
        "ln2_b": nrm(ks[22], (L, D), 0.01),
    }


def reference(x, w_in, pool_w, pool_scale, conv_w, conv_b, lru_wa, lru_ba, lru_wx, lru_bx,
              lru_lambda, w_pool_up, w_lru_up, w_out, b_out, ln1_g, ln1_b,
              w_ff1, b_ff1, w_ff2, b_ff2, ln2_g, ln2_b):
    for l in range(DEPTH):
        mix = hybrid_mixer(x, w_in[l], pool_w[l], pool_scale[l], conv_w[l], conv_b[l],
                           lru_wa[l], lru_ba[l], lru_wx[l], lru_bx[l], lru_lambda[l],
                           w_pool_up[l], w_lru_up[l], w_out[l], b_out[l])
        x = layer_norm(DN_ALPHA * x + mix, ln1_g[l], ln1_b[l])
        ff = sq_relu_mlp(x, w_ff1[l], b_ff1[l], w_ff2[l], b_ff2[l])
        x = layer_norm(DN_ALPHA * x + ff, ln2_g[l], ln2_b[l])
    return x
```

```python
import functools

import jax
import jax.numpy as jnp
from jax import lax
from jax.experimental import pallas as pl
from jax.experimental.pallas import tpu as pltpu

SUBLANES = 8
LANES = 128
NSEG = SUBLANES

POOL_WINDOWS = (2, 4, 8, 16)
LRU_BLOCK = 256
CONV_WIDTH = 4
CONV_LEFT = CONV_WIDTH // 2
LRU_C = 8.0
LN_EPS = 1e-5
HALO = 8

VMEM_LIMIT = 60 * 1024 * 1024

F32 = jnp.float32
BF16 = jnp.bfloat16


def _sigmoid(v):
    return 0.5 * jnp.tanh(0.5 * v) + 0.5


def _gelu_tanh(v):
    c = 0.7978845608028654
    return 0.5 * v * (1.0 + jnp.tanh(c * (v + 0.044715 * (v * v * v))))


def _layer_norm(y, g, b):
    mu = jnp.mean(y, axis=-1, keepdims=True)
    yc = y - mu
    var = jnp.mean(yc * yc, axis=-1, keepdims=True)
    return yc * lax.rsqrt(var + LN_EPS) * g + b


def _mm_act_kernel(x_ref, w_ref, o_ref, *, act):
    acc = jnp.dot(x_ref[...], w_ref[...], preferred_element_type=F32)
    if act == "gelu":
        acc = _gelu_tanh(acc)
    elif act == "sigmoid":
        acc = _sigmoid(acc)
    o_ref[...] = acc.astype(o_ref.dtype)


def _mm_act(x, w, act, tm=1024, tn=1024):
    m, k = x.shape
    n = w.shape[1]
    return pl.pallas_call(
        functools.partial(_mm_act_kernel, act=act),
        grid=(m // tm, n // tn),
        in_specs=[pl.BlockSpec((tm, k), lambda i, j: (i, 0)),
                  pl.BlockSpec((k, tn), lambda i, j: (0, j))],
        out_specs=pl.BlockSpec((tm, tn), lambda i, j: (i, j)),
        out_shape=jax.ShapeDtypeStruct((m, n), F32),
        compiler_params=pltpu.CompilerParams(
            dimension_semantics=("arbitrary", "arbitrary"),
            vmem_limit_bytes=VMEM_LIMIT),
        name=f"mm_in_{act}",
    )(x, w)


def _fill_ext(ext, u_ref, seq_rows):
    h_rows = HALO * SUBLANES
    c = ext.shape[1]
    ext[pl.ds(h_rows, seq_rows), :] = u_ref[0]
    sub = lax.broadcasted_iota(jnp.int32, (SUBLANES, c), 0)
    for m in range(HALO):
        head = u_ref[0, pl.ds(m * SUBLANES, SUBLANES), :]
        nxt = pltpu.roll(head, SUBLANES - 1, 0)
        ext[pl.ds(h_rows + seq_rows + m * SUBLANES, SUBLANES), :] = jnp.where(sub == SUBLANES - 1, 0.0, nxt)
        tail = u_ref[0, pl.ds(seq_rows - (m + 1) * SUBLANES, SUBLANES), :]
        prv = pltpu.roll(tail, 1, 0)
        ext[pl.ds(h_rows - (m + 1) * SUBLANES, SUBLANES), :] = jnp.where(sub == 0, 0.0, prv)


def _pool_kernel(u_ref, w_ref, s_ref, o_ref, ext, *, seq, chunk):
    seg_len = seq // NSEG
    h_rows = HALO * SUBLANES
    gw = u_ref.shape[2]
    _fill_ext(ext, u_ref, seq)
    grp = pl.program_id(1)
    n_chunks = seq // chunk

    def body(win):
        half = win // 2

        def do_chunk(ci, carry):
            r0 = pl.multiple_of(ci * chunk, chunk)
            base = r0 + h_rows
            tot = ext[pl.ds(base - half * SUBLANES, chunk), :]
            for m in range(-half + 1, half):
                tot = tot + ext[pl.ds(base + m * SUBLANES, chunk), :]
            row = r0 + lax.broadcasted_iota(jnp.int32, (chunk, LANES), 0)
            t = (row & (SUBLANES - 1)) * seg_len + (row >> 3)
            cnt = jnp.minimum(t + half, seq) - jnp.maximum(t - half, 0)
            inv = 1.0 / cnt.astype(F32)
            u = ext[pl.ds(base, chunk), :]
            d = jnp.concatenate(
                [tot[:, q * LANES:(q + 1) * LANES] * inv - u[:, q * LANES:(q + 1) * LANES]
                 for q in range(gw // LANES)], axis=1)
            y = jnp.dot(d.astype(BF16), w_ref[0], preferred_element_type=F32) * s_ref[0]
            o_ref[0, pl.ds(r0, chunk), :] = y.astype(o_ref.dtype)
            return carry

        lax.fori_loop(0, n_chunks, do_chunk, 0)

    for gi, win in enumerate(POOL_WINDOWS):
        pl.when(grp == gi)(functools.partial(body, win))


def _pool_branch(z_a, pool_w, pool_scale, batch, seq, chunk=512):
    n_groups, gw = pool_w.shape[0], pool_w.shape[1]
    z3 = z_a.reshape(batch, seq, z_a.shape[1])
    return pl.pallas_call(
        functools.partial(_pool_kernel, seq=seq, chunk=chunk),
        grid=(batch, n_groups),
        in_specs=[pl.BlockSpec((1, seq, gw), lambda b, g: (b, 0, g)),
                  pl.BlockSpec((1, gw, gw), lambda b, g: (g, 0, 0)),
                  pl.BlockSpec((1, 1, gw), lambda b, g: (g, 0, 0))],
        out_specs=pl.BlockSpec((1, seq, gw), lambda b, g: (b, 0, g)),
        out_shape=jax.ShapeDtypeStruct((batch, seq, n_groups * gw), BF16),
        scratch_shapes=[pltpu.VMEM((seq + 2 * HALO * SUBLANES, gw), F32)],
        compiler_params=pltpu.CompilerParams(
            dimension_semantics=("arbitrary", "arbitrary"),
            vmem_limit_bytes=VMEM_LIMIT),
        name="pool_branch",
    )(z3, pool_w, pool_scale.reshape(n_groups, 1, gw))


_P_CONV_W = 0
_P_CONV_B = CONV_WIDTH
_P_DIR = CONV_WIDTH + 1
_P_ROWS = 16


def _lru_kernel(u_ref, gate_ref, w_ref, p_ref, o_ref, ext, a_f, x_f, a_b, x_b, *, seq, chunk):
    seg_len = seq // NSEG
    h_rows = HALO * SUBLANES
    c = LRU_BLOCK
    _fill_ext(ext, u_ref, seq)

    conv_w = [p_ref[pl.ds(_P_CONV_W + k, 1), :] for k in range(CONV_WIDTH)]
    conv_b = p_ref[pl.ds(_P_CONV_B, 1), :]
    b_a, b_x, dec = [], [], []
    for d in range(2):
        b_a.append(p_ref[pl.ds(_P_DIR + 3 * d, 1), :])
        b_x.append(p_ref[pl.ds(_P_DIR + 3 * d + 1, 1), :])
        lam = p_ref[pl.ds(_P_DIR + 3 * d + 2, 1), :]
        dec.append(-LRU_C * jax.nn.softplus(-lam))
    a_out = (a_f, a_b)
    x_out = (x_f, x_b)

    def gates(ci, carry):
        r0 = pl.multiple_of(ci * chunk, chunk)
        base = r0 + h_rows
        xc = conv_b + ext[pl.ds(base - CONV_LEFT * SUBLANES, chunk), :] * conv_w[0]
        for k in range(1, CONV_WIDTH):
            xc = xc + ext[pl.ds(base + (k - CONV_LEFT) * SUBLANES, chunk), :] * conv_w[k]
        pre = jnp.dot(xc.astype(BF16), w_ref[0], preferred_element_type=F32)
        for d in range(2):
            r = _sigmoid(pre[:, (2 * d) * c:(2 * d + 1) * c] + b_a[d])
            i = _sigmoid(pre[:, (2 * d + 1) * c:(2 * d + 2) * c] + b_x[d])
            log_a = dec[d] * r
            a = jnp.exp(log_a)
            one_m_a2 = -jnp.tanh(log_a) * (1.0 + a * a)
            a_out[d][pl.ds(r0, chunk), :] = a
            x_out[d][pl.ds(r0, chunk), :] = jnp.sqrt(one_m_a2) * (i * xc)
        return carry

    lax.fori_loop(0, seq // chunk, gates, 0)

    def local_scan(j, carry):
        h_f, p_f, h_b, p_b = carry
        rf = pl.multiple_of(j * SUBLANES, SUBLANES)
        rb = pl.multiple_of((seg_len - 1 - j) * SUBLANES, SUBLANES)
        af = a_f[pl.ds(rf, SUBLANES), :]
        h_f = af * h_f + x_f[pl.ds(rf, SUBLANES), :]
        p_f = af * p_f
        ab = a_b[pl.ds(rb, SUBLANES), :]
        h_b = ab * h_b + x_b[pl.ds(rb, SUBLANES), :]
        p_b = ab * p_b
        return h_f, p_f, h_b, p_b

    zeros = jnp.zeros((SUBLANES, c), F32)
    ones = jnp.ones((SUBLANES, c), F32)
    h_f, p_f, h_b, p_b = lax.fori_loop(0, seg_len, local_scan, (zeros, ones, zeros, ones), unroll=8)

    sub = lax.broadcasted_iota(jnp.int32, (SUBLANES, c), 0)
    c_f = zeros
    c_b = zeros
    for _ in range(NSEG - 1):
        c_f = jnp.where(sub == 0, 0.0, pltpu.roll(h_f + p_f * c_f, 1, 0))
        c_b = jnp.where(sub == SUBLANES - 1, 0.0, pltpu.roll(h_b + p_b * c_b, SUBLANES - 1, 0))

    def fwd(j, h):
        rf = pl.multiple_of(j * SUBLANES, SUBLANES)
        h = a_f[pl.ds(rf, SUBLANES), :] * h + x_f[pl.ds(rf, SUBLANES), :]
        x_f[pl.ds(rf, SUBLANES), :] = h
        return h

    lax.fori_loop(0, seg_len, fwd, c_f, unroll=8)

    def bwd(j, h):
        rb = pl.multiple_of((seg_len - 1 - j) * SUBLANES, SUBLANES)
        h = a_b[pl.ds(rb, SUBLANES), :] * h + x_b[pl.ds(rb, SUBLANES), :]
        x_b[pl.ds(rb, SUBLANES), :] = (h + x_f[pl.ds(rb, SUBLANES), :]) * gate_ref[0, pl.ds(rb, SUBLANES), :]
        return h

    lax.fori_loop(0, seg_len, bwd, c_b, unroll=8)

    def emit(ci, carry):
        r0 = pl.multiple_of(ci * chunk, chunk)
        o_ref[0, pl.ds(r0, chunk), :] = x_b[pl.ds(r0, chunk), :].astype(o_ref.dtype)
        return carry

    lax.fori_loop(0, seq // chunk, emit, 0)


def _lru_branch(z_a, gate_act, w_gates, params, batch, seq, lru_col0, chunk=256):
    n_heads = w_gates.shape[0]
    c = LRU_BLOCK
    z3 = z_a.reshape(batch, seq, z_a.shape[1])
    g3 = gate_act.reshape(batch, seq, gate_act.shape[1])
    col0 = lru_col0 // c
    return pl.pallas_call(
        functools.partial(_lru_kernel, seq=seq, chunk=chunk),
        grid=(batch, n_heads),
        in_specs=[pl.BlockSpec((1, seq, c), lambda b, h: (b, 0, col0 + h)),
                  pl.BlockSpec((1, seq, c), lambda b, h: (b, 0, h)),
                  pl.BlockSpec((1, c, 4 * c), lambda b, h: (h, 0, 0)),
                  pl.BlockSpec((_P_ROWS, c), lambda b, h: (0, h))],
        out_specs=pl.BlockSpec((1, seq, c), lambda b, h: (b, 0, h)),
        out_shape=jax.ShapeDtypeStruct((batch, seq, n_heads * c), BF16),
        scratch_shapes=[pltpu.VMEM((seq + 2 * HALO * SUBLANES, c), F32)]
        + [pltpu.VMEM((seq, c), F32) for _ in range(4)],
        compiler_params=pltpu.CompilerParams(
            dimension_semantics=("arbitrary", "arbitrary"),
            vmem_limit_bytes=VMEM_LIMIT),
        name="lru_branch",
    )(z3, g3, w_gates, params)


def _merge_kernel(yp_ref, yl_ref, g0_ref, g1_ref, x_ref, wp_ref, wl_ref, wo_ref, bo_ref, g_ref, b_ref,
                  of_ref, ob_ref, *, alpha):
    up_p = jnp.dot(yp_ref[...], wp_ref[...], preferred_element_type=F32)
    up_l = jnp.dot(yl_ref[...], wl_ref[...], preferred_element_type=F32)
    m = g0_ref[...] * up_p + g1_ref[...] * up_l
    mix = jnp.dot(m.astype(BF16), wo_ref[...], preferred_element_type=F32) + bo_ref[...]
    y = _layer_norm(alpha * x_ref[...] + mix, g_ref[...], b_ref[...])
    of_ref[...] = y
    ob_ref[...] = y.astype(BF16)


def _merge(y_pool, y_lru, g_act, x_rows, w_pool_up, w_lru_up, w_out, b_out, ln_g, ln_b, alpha, tm=256):
    t, d = x_rows.shape
    row = lambda i: (i, 0)
    const = lambda i: (0, 0)
    wspec = pl.BlockSpec((d, d), const, pipeline_mode=pl.Buffered(1))
    vspec = pl.BlockSpec((1, d), const)
    return pl.pallas_call(
        functools.partial(_merge_kernel, alpha=alpha),
        grid=(t // tm,),
        in_specs=[pl.BlockSpec((tm, d), row), pl.BlockSpec((tm, d), row),
                  pl.BlockSpec((tm, d), lambda i: (i, 0)), pl.BlockSpec((tm, d), lambda i: (i, 1)),
                  pl.BlockSpec((tm, d), row), wspec, wspec, wspec, vspec, vspec, vspec],
        out_specs=[pl.BlockSpec((tm, d), row), pl.BlockSpec((tm, d), row)],
        out_shape=[jax.ShapeDtypeStruct((t, d), F32), jax.ShapeDtypeStruct((t, d), BF16)],
        compiler_params=pltpu.CompilerParams(
            dimension_semantics=("arbitrary",), vmem_limit_bytes=VMEM_LIMIT),
        name="merge_out_ln",
    )(y_pool, y_lru, g_act, g_act, x_rows, w_pool_up, w_lru_up, w_out,
      b_out.reshape(1, d), ln_g.reshape(1, d), ln_b.reshape(1, d))


def _mlp_kernel(xb_ref, xf_ref, w1_ref, b1_ref, w2_ref, b2_ref, g_ref, b_ref, o_ref, acc, *, alpha):
    f = pl.program_id(1)

    @pl.when(f == 0)
    def _():
        acc[...] = jnp.zeros_like(acc)

    h = jnp.dot(xb_ref[...], w1_ref[...], preferred_element_type=F32) + b1_ref[...]
    h = jnp.square(jnp.maximum(h, 0.0))
    acc[...] += jnp.dot(h.astype(BF16), w2_ref[...], preferred_element_type=F32)

    @pl.when(f == pl.num_programs(1) - 1)
    def _():
        y = alpha * xf_ref[...] + acc[...] + b2_ref[...]
        o_ref[...] = _layer_norm(y, g_ref[...], b_ref[...])


def _mlp(x_b, x_f, w1, b1, w2, b2, ln_g, ln_b, alpha, tm=512, tf=512):
    t, d = x_f.shape
    dff = w1.shape[1]
    return pl.pallas_call(
        functools.partial(_mlp_kernel, alpha=alpha),
        grid=(t // tm, dff // tf),
        in_specs=[pl.BlockSpec((tm, d), lambda i, f: (i, 0)),
                  pl.BlockSpec((tm, d), lambda i, f: (i, 0)),
                  pl.BlockSpec((d, tf), lambda i, f: (0, f)),
                  pl.BlockSpec((1, tf), lambda i, f: (0, f)),
                  pl.BlockSpec((tf, d), lambda i, f: (f, 0)),
                  pl.BlockSpec((1, d), lambda i, f: (0, 0)),
                  pl.BlockSpec((1, d), lambda i, f: (0, 0)),
                  pl.BlockSpec((1, d), lambda i, f: (0, 0))],
        out_specs=pl.BlockSpec((tm, d), lambda i, f: (i, 0)),
        out_shape=jax.ShapeDtypeStruct((t, d), F32),
        scratch_shapes=[pltpu.VMEM((tm, d), F32)],
        compiler_params=pltpu.CompilerParams(
            dimension_semantics=("arbitrary", "arbitrary"), vmem_limit_bytes=VMEM_LIMIT),
        name="mlp_ln",
    )(x_b, x_f, w1, b1.reshape(1, dff), w2, b2.reshape(1, d), ln_g.reshape(1, d), ln_b.reshape(1, d))


def _to_segment_rows(x):
    b, s, d = x.shape
    return x.reshape(b, NSEG, s // NSEG, d).transpose(0, 2, 1, 3).reshape(b * s, d)


def _from_segment_rows(rows, b, s):
    d = rows.shape[1]
    return rows.reshape(b, s // NSEG, NSEG, d).transpose(0, 2, 1, 3).reshape(b, s, d)


def _layer(x_rows, batch, seq, alpha, w_in, pool_w, pool_scale, conv_w, conv_b, lru_wa, lru_ba, lru_wx, lru_bx,
           lru_lambda, w_pool_up, w_lru_up, w_out, b_out, ln1_g, ln1_b, w_ff1, b_ff1, w_ff2, b_ff2, ln2_g, ln2_b):
    pool_width = pool_w.shape[0] * pool_w.shape[1]
    lru_width = conv_w.shape[1]
    o1, o2, o3 = pool_width, pool_width + lru_width, pool_width + 2 * lru_width

    x_b = x_rows.astype(BF16)
    w_in_b = w_in.astype(BF16)
    z_a = _mm_act(x_b, w_in_b[:, :o2], "none")
    gate_act = _mm_act(x_b, w_in_b[:, o2:o3], "gelu")
    g_act = _mm_act(x_b, w_in_b[:, o3:], "sigmoid")

    y_pool = _pool_branch(z_a, pool_w.astype(BF16), pool_scale, batch, seq)

    w_gates = jnp.concatenate([lru_wa[0], lru_wx[0], lru_wa[1], lru_wx[1]], axis=-1).astype(BF16)
    params = jnp.concatenate(
        [conv_w, conv_b[None], lru_ba[0][None], lru_bx[0][None], lru_lambda[0][None],
         lru_ba[1][None], lru_bx[1][None], lru_lambda[1][None]], axis=0).astype(F32)
    params = jnp.pad(params, ((0, _P_ROWS - params.shape[0]), (0, 0)))
    y_lru = _lru_branch(z_a, gate_act, w_gates, params, batch, seq, o1)

    t = batch * seq
    x1_f, x1_b = _merge(y_pool.reshape(t, -1), y_lru.reshape(t, -1), g_act, x_rows,
                        w_pool_up.astype(BF16), w_lru_up.astype(BF16), w_out.astype(BF16),
                        b_out, ln1_g, ln1_b, alpha)
    return _mlp(x1_b, x1_f, w_ff1.astype(BF16), b_ff1, w_ff2.astype(BF16), b_ff2, ln2_g, ln2_b, alpha)


def kernel(x, w_in, pool_w, pool_scale, conv_w, conv_b, lru_wa, lru_ba, lru_wx, lru_bx, lru_lambda, w_pool_up, w_lru_up, w_out, b_out, ln1_g, ln1_b, w_ff1, b_ff1, w_ff2, b_ff2, ln2_g, ln2_b):
    batch, seq, _ = x.shape
    depth = w_in.shape[0]
    alpha = (2.0 * depth) ** 0.25
    rows = _to_segment_rows(x)
    for l in range(depth):
        rows = _layer(rows, batch, seq, alpha, w_in[l], pool_w[l], pool_scale[l], conv_w[l], conv_b[l],
                      lru_wa[l], lru_ba[l], lru_wx[l], lru_bx[l], lru_lambda[l], w_pool_up[l], w_lru_up[l],
                      w_out[l], b_out[l], ln1_g[l], ln1_b[l], w_ff1[l], b_ff1[l], w_ff2[l], b_ff2[l],
                      ln2_g[l], ln2_b[l])
    return _from_segment_rows(rows, batch, seq)
```

```python
import functools

import jax
import jax.numpy as jnp
from jax import lax
from jax.experimental import pallas as pl
from jax.experimental.pallas import tpu as pltpu

SUBLANES = 8
LANES = 128
NSEG = SUBLANES

POOL_WINDOWS = (2, 4, 8, 16)
LRU_BLOCK = 256
CONV_WIDTH = 4
CONV_LEFT = CONV_WIDTH // 2
LRU_C = 8.0
LN_EPS = 1e-5
HALO = 8
SCAN_GROUPS = 8
LOG2_E = 1.4426950408889634
LN_2 = 0.6931471805599453
TINY = 1.1754944e-38

VMEM_LIMIT = 60 * 1024 * 1024

F32 = jnp.float32
BF16 = jnp.bfloat16


def _sigmoid(v):
    return 0.5 * jnp.tanh(0.5 * v) + 0.5


def _gelu_tanh(v):
    c = 0.7978845608028654
    return 0.5 * v * (1.0 + jnp.tanh(c * (v + 0.044715 * (v * v * v))))


def _layer_norm(y, g, b):
    mu = jnp.mean(y, axis=-1, keepdims=True)
    yc = y - mu
    var = jnp.mean(yc * yc, axis=-1, keepdims=True)
    return yc * lax.rsqrt(var + LN_EPS) * g + b


def _mm_act_kernel(x_ref, w_ref, o_ref, w_bf, *, act):
    @pl.when(pl.program_id(1) == 0)
    def _():
        w_bf[...] = w_ref[0].astype(BF16)

    acc = jnp.dot(x_ref[...], w_bf[...], preferred_element_type=F32)
    if act == "gelu":
        acc = _gelu_tanh(acc)
    elif act == "sigmoid":
        acc = _sigmoid(acc)
    o_ref[...] = acc.astype(o_ref.dtype)


def _mm_act(x, w, layer, col0, n, act, tm=1024, tn=1024):
    m, k = x.shape
    j0 = col0 // tn
    return pl.pallas_call(
        functools.partial(_mm_act_kernel, act=act),
        grid=(n // tn, m // tm),
        in_specs=[pl.BlockSpec((tm, k), lambda j, i: (i, 0)),
                  pl.BlockSpec((1, k, tn), lambda j, i: (layer, 0, j0 + j))],
        out_specs=pl.BlockSpec((tm, tn), lambda j, i: (i, j)),
        out_shape=jax.ShapeDtypeStruct((m, n), F32),
        scratch_shapes=[pltpu.VMEM((k, tn), BF16)],
        compiler_params=pltpu.CompilerParams(
            dimension_semantics=("arbitrary", "arbitrary"),
            vmem_limit_bytes=VMEM_LIMIT),
        name=f"mm_in_{act}",
    )(x, w)


def _fill_ext(ext, u_ref, seq_rows):
    h_rows = HALO * SUBLANES
    c = ext.shape[1]
    ext[pl.ds(h_rows, seq_rows), :] = u_ref[0]
    sub = lax.broadcasted_iota(jnp.int32, (SUBLANES, c), 0)
    for m in range(HALO):
        head = u_ref[0, pl.ds(m * SUBLANES, SUBLANES), :]
        nxt = pltpu.roll(head, SUBLANES - 1, 0)
        ext[pl.ds(h_rows + seq_rows + m * SUBLANES, SUBLANES), :] = jnp.where(sub == SUBLANES - 1, 0.0, nxt)
        tail = u_ref[0, pl.ds(seq_rows - (m + 1) * SUBLANES, SUBLANES), :]
        prv = pltpu.roll(tail, 1, 0)
        ext[pl.ds(h_rows - (m + 1) * SUBLANES, SUBLANES), :] = jnp.where(sub == 0, 0.0, prv)


def _pool_kernel(u_ref, w_ref, s_ref, o_ref, ext, *, seq, chunk):
    seg_len = seq // NSEG
    h_rows = HALO * SUBLANES
    gw = u_ref.shape[2]
    _fill_ext(ext, u_ref, seq)
    grp = pl.program_id(1)
    n_chunks = seq // chunk

    def body(win):
        half = win // 2

        def do_chunk(ci, carry):
            r0 = pl.multiple_of(ci * chunk, chunk)
            base = r0 + h_rows
            tot = ext[pl.ds(base - half * SUBLANES, chunk), :]
            for m in range(-half + 1, half):
                tot = tot + ext[pl.ds(base + m * SUBLANES, chunk), :]
            row = r0 + lax.broadcasted_iota(jnp.int32, (chunk, LANES), 0)
            t = (row & (SUBLANES - 1)) * seg_len + (row >> 3)
            cnt = jnp.minimum(t + half, seq) - jnp.maximum(t - half, 0)
            inv = 1.0 / cnt.astype(F32)
            u = ext[pl.ds(base, chunk), :]
            d = jnp.concatenate(
                [tot[:, q * LANES:(q + 1) * LANES] * inv - u[:, q * LANES:(q + 1) * LANES]
                 for q in range(gw // LANES)], axis=1)
            y = jnp.dot(d.astype(BF16), w_ref[0], preferred_element_type=F32) * s_ref[0]
            o_ref[0, pl.ds(r0, chunk), :] = y.astype(o_ref.dtype)
            return carry

        lax.fori_loop(0, n_chunks, do_chunk, 0)

    for gi, win in enumerate(POOL_WINDOWS):
        pl.when(grp == gi)(functools.partial(body, win))


def _pool_branch(z_a, pool_w, pool_scale, batch, seq, chunk=512):
    n_groups, gw = pool_w.shape[0], pool_w.shape[1]
    z3 = z_a.reshape(batch, seq, z_a.shape[1])
    return pl.pallas_call(
        functools.partial(_pool_kernel, seq=seq, chunk=chunk),
        grid=(batch, n_groups),
        in_specs=[pl.BlockSpec((1, seq, gw), lambda b, g: (b, 0, g)),
                  pl.BlockSpec((1, gw, gw), lambda b, g: (g, 0, 0)),
                  pl.BlockSpec((1, 1, gw), lambda b, g: (g, 0, 0))],
        out_specs=pl.BlockSpec((1, seq, gw), lambda b, g: (b, 0, g)),
        out_shape=jax.ShapeDtypeStruct((batch, seq, n_groups * gw), BF16),
        scratch_shapes=[pltpu.VMEM((seq + 2 * HALO * SUBLANES, gw), F32)],
        compiler_params=pltpu.CompilerParams(
            dimension_semantics=("arbitrary", "arbitrary"),
            vmem_limit_bytes=VMEM_LIMIT),
        name="pool_branch",
    )(z3, pool_w, pool_scale.reshape(n_groups, 1, gw))


_P_CONV_W = 0
_P_CONV_B = CONV_WIDTH
_P_DIR = CONV_WIDTH + 1
_P_ROWS = 16


def _lru_kernel(u_ref, gate_ref, w_ref, p_ref, o_ref, ext, a_f, x_f, a_b, x_b, hf_buf, *, seq, chunk):
    seg_len = seq // NSEG
    h_rows = HALO * SUBLANES
    c = LRU_BLOCK
    slab = SCAN_GROUPS * SUBLANES
    n_slabs = seq // slab
    _fill_ext(ext, u_ref, seq)

    conv_hw = [0.5 * p_ref[pl.ds(_P_CONV_W + k, 1), :] for k in range(CONV_WIDTH)]
    conv_hb = 0.5 * p_ref[pl.ds(_P_CONV_B, 1), :]
    hb_a, hb_x, c2 = [], [], []
    for d in range(2):
        hb_a.append(0.5 * p_ref[pl.ds(_P_DIR + 3 * d, 1), :])
        hb_x.append(0.5 * p_ref[pl.ds(_P_DIR + 3 * d + 1, 1), :])
        lam = p_ref[pl.ds(_P_DIR + 3 * d + 2, 1), :]
        c2.append((-0.5 * LRU_C * LOG2_E) * jax.nn.softplus(-lam))
    a_out = (a_f, a_b)
    x_out = (x_f, x_b)

    def gates(ci, carry):
        r0 = pl.multiple_of(ci * chunk, chunk)
        base = r0 + h_rows
        xh = conv_hb + ext[pl.ds(base - CONV_LEFT * SUBLANES, chunk), :] * conv_hw[0]
        for k in range(1, CONV_WIDTH):
            xh = xh + ext[pl.ds(base + (k - CONV_LEFT) * SUBLANES, chunk), :] * conv_hw[k]
        pre = jnp.dot(xh.astype(BF16), w_ref[0], preferred_element_type=F32)
        for d in range(2):
            t_r = jnp.tanh(pre[:, (2 * d) * c:(2 * d + 1) * c] + hb_a[d])
            t_i = jnp.tanh(pre[:, (2 * d + 1) * c:(2 * d + 2) * c] + hb_x[d])
            log2_a = c2[d] * t_r + c2[d]
            a = jnp.exp2(log2_a)
            one_m_a2 = (-1.0 - a * a) * jnp.tanh(LN_2 * log2_a)
            root = one_m_a2 * lax.rsqrt(jnp.maximum(one_m_a2, TINY))
            a_out[d][pl.ds(r0, chunk), :] = a
            x_out[d][pl.ds(r0, chunk), :] = root * (t_i * xh + xh)
        return carry

    lax.fori_loop(0, seq // chunk, gates, 0)

    def rows(v, k):
        return v[k * SUBLANES:(k + 1) * SUBLANES]

    def local_scan(it, carry):
        h_f, p_f, h_b, p_b = carry
        rf = pl.multiple_of(it * slab, slab)
        rb = pl.multiple_of((n_slabs - 1 - it) * slab, slab)
        af, xf = a_f[pl.ds(rf, slab), :], x_f[pl.ds(rf, slab), :]
        ab, xb = a_b[pl.ds(rb, slab), :], x_b[pl.ds(rb, slab), :]
        for k in range(SCAN_GROUPS):
            kb = SCAN_GROUPS - 1 - k
            h_f = rows(af, k) * h_f + rows(xf, k)
            p_f = rows(af, k) * p_f
            h_b = rows(ab, kb) * h_b + rows(xb, kb)
            p_b = rows(ab, kb) * p_b
        return h_f, p_f, h_b, p_b

    zeros = jnp.zeros((SUBLANES, c), F32)
    ones = jnp.ones((SUBLANES, c), F32)
    h_f, p_f, h_b, p_b = lax.fori_loop(0, n_slabs, local_scan, (zeros, ones, zeros, ones))

    sub = lax.broadcasted_iota(jnp.int32, (SUBLANES, c), 0)
    c_f = zeros
    c_b = zeros
    for _ in range(NSEG - 1):
        c_f = jnp.where(sub == 0, 0.0, pltpu.roll(h_f + p_f * c_f, 1, 0))
        c_b = jnp.where(sub == SUBLANES - 1, 0.0, pltpu.roll(h_b + p_b * c_b, SUBLANES - 1, 0))

    def fwd(it, h):
        rf = pl.multiple_of(it * slab, slab)
        af, xf = a_f[pl.ds(rf, slab), :], x_f[pl.ds(rf, slab), :]
        out = []
        for k in range(SCAN_GROUPS):
            h = rows(af, k) * h + rows(xf, k)
            out.append(h)
        hf_buf[pl.ds(rf, slab), :] = jnp.concatenate(out, axis=0)
        return h

    lax.fori_loop(0, n_slabs, fwd, c_f)

    def bwd(it, h):
        rb = pl.multiple_of((n_slabs - 1 - it) * slab, slab)
        ab, xb = a_b[pl.ds(rb, slab), :], x_b[pl.ds(rb, slab), :]
        hf, gate = hf_buf[pl.ds(rb, slab), :], gate_ref[0, pl.ds(rb, slab), :]
        out = [None] * SCAN_GROUPS
        for kb in range(SCAN_GROUPS - 1, -1, -1):
            h = rows(ab, kb) * h + rows(xb, kb)
            out[kb] = (h + rows(hf, kb)) * rows(gate, kb)
        o_ref[0, pl.ds(rb, slab), :] = jnp.concatenate(out, axis=0).astype(o_ref.dtype)
        return h

    lax.fori_loop(0, n_slabs, bwd, c_b)


def _lru_branch(z_a, gate_act, w_gates, params, batch, seq, lru_col0, chunk=256):
    n_heads = w_gates.shape[0]
    c = LRU_BLOCK
    z3 = z_a.reshape(batch, seq, z_a.shape[1])
    g3 = gate_act.reshape(batch, seq, gate_act.shape[1])
    col0 = lru_col0 // c
    return pl.pallas_call(
        functools.partial(_lru_kernel, seq=seq, chunk=chunk),
        grid=(batch, n_heads),
        in_specs=[pl.BlockSpec((1, seq, c), lambda b, h: (b, 0, col0 + h)),
                  pl.BlockSpec((1, seq, c), lambda b, h: (b, 0, h)),
                  pl.BlockSpec((1, c, 4 * c), lambda b, h: (h, 0, 0)),
                  pl.BlockSpec((_P_ROWS, c), lambda b, h: (0, h))],
        out_specs=pl.BlockSpec((1, seq, c), lambda b, h: (b, 0, h)),
        out_shape=jax.ShapeDtypeStruct((batch, seq, n_heads * c), BF16),
        scratch_shapes=[pltpu.VMEM((seq + 2 * HALO * SUBLANES, c), F32)]
        + [pltpu.VMEM((seq, c), F32) for _ in range(5)],
        compiler_params=pltpu.CompilerParams(
            dimension_semantics=("arbitrary", "arbitrary"),
            vmem_limit_bytes=VMEM_LIMIT),
        name="lru_branch",
    )(z3, g3, w_gates, params)


def _merge_kernel(yp_ref, yl_ref, g0_ref, g1_ref, x_ref, wp_ref, wl_ref, wo_ref, bo_ref, g_ref, b_ref,
                  of_ref, ob_ref, *, alpha):
    up_p = jnp.dot(yp_ref[...], wp_ref[...], preferred_element_type=F32)
    up_l = jnp.dot(yl_ref[...], wl_ref[...], preferred_element_type=F32)
    m = g0_ref[...] * up_p + g1_ref[...] * up_l
    mix = jnp.dot(m.astype(BF16), wo_ref[...], preferred_element_type=F32) + bo_ref[...]
    y = _layer_norm(alpha * x_ref[...] + mix, g_ref[...], b_ref[...])
    of_ref[...] = y
    ob_ref[...] = y.astype(BF16)


def _merge(y_pool, y_lru, g_act, x_rows, w_pool_up, w_lru_up, w_out, b_out, ln_g, ln_b, alpha, tm=256):
    t, d = x_rows.shape
    row = lambda i: (i, 0)
    const = lambda i: (0, 0)
    wspec = pl.BlockSpec((d, d), const, pipeline_mode=pl.Buffered(1))
    vspec = pl.BlockSpec((1, d), const)
    return pl.pallas_call(
        functools.partial(_merge_kernel, alpha=alpha),
        grid=(t // tm,),
        in_specs=[pl.BlockSpec((tm, d), row), pl.BlockSpec((tm, d), row),
                  pl.BlockSpec((tm, d), lambda i: (i, 0)), pl.BlockSpec((tm, d), lambda i: (i, 1)),
                  pl.BlockSpec((tm, d), row), wspec, wspec, wspec, vspec, vspec, vspec],
        out_specs=[pl.BlockSpec((tm, d), row), pl.BlockSpec((tm, d), row)],
        out_shape=[jax.ShapeDtypeStruct((t, d), F32), jax.ShapeDtypeStruct((t, d), BF16)],
        compiler_params=pltpu.CompilerParams(
            dimension_semantics=("arbitrary",), vmem_limit_bytes=VMEM_LIMIT),
        name="merge_out_ln",
    )(y_pool, y_lru, g_act, g_act, x_rows, w_pool_up, w_lru_up, w_out,
      b_out.reshape(1, d), ln_g.reshape(1, d), ln_b.reshape(1, d))


def _mlp_kernel(xb_ref, xf_ref, w1_ref, b1_ref, w2_ref, b2_ref, g_ref, b_ref, o_ref, *, alpha):
    f = pl.program_id(1)

    @pl.when(f == 0)
    def _():
        o_ref[...] = jnp.zeros_like(o_ref)

    h = jnp.dot(xb_ref[...], w1_ref[...], preferred_element_type=F32) + b1_ref[...]
    h = jnp.square(jnp.maximum(h, 0.0))
    o_ref[...] += jnp.dot(h.astype(BF16), w2_ref[...], preferred_element_type=F32)

    @pl.when(f == pl.num_programs(1) - 1)
    def _():
        y = alpha * xf_ref[...] + o_ref[...] + b2_ref[...]
        o_ref[...] = _layer_norm(y, g_ref[...], b_ref[...])


def _mlp(x_b, x_f, w1, b1, w2, b2, ln_g, ln_b, alpha, tm=1024, tf=512):
    t, d = x_f.shape
    dff = w1.shape[1]
    return pl.pallas_call(
        functools.partial(_mlp_kernel, alpha=alpha),
        grid=(t // tm, dff // tf),
        in_specs=[pl.BlockSpec((tm, d), lambda i, f: (i, 0)),
                  pl.BlockSpec((tm, d), lambda i, f: (i, 0), pipeline_mode=pl.Buffered(1)),
                  pl.BlockSpec((d, tf), lambda i, f: (0, f)),
                  pl.BlockSpec((1, tf), lambda i, f: (0, f)),
                  pl.BlockSpec((tf, d), lambda i, f: (f, 0)),
                  pl.BlockSpec((1, d), lambda i, f: (0, 0)),
                  pl.BlockSpec((1, d), lambda i, f: (0, 0)),
                  pl.BlockSpec((1, d), lambda i, f: (0, 0))],
        out_specs=pl.BlockSpec((tm, d), lambda i, f: (i, 0)),
        out_shape=jax.ShapeDtypeStruct((t, d), F32),
        compiler_params=pltpu.CompilerParams(
            dimension_semantics=("arbitrary", "arbitrary"), vmem_limit_bytes=VMEM_LIMIT),
        name="mlp_ln",
    )(x_b, x_f, w1, b1.reshape(1, dff), w2, b2.reshape(1, d), ln_g.reshape(1, d), ln_b.reshape(1, d))


def _to_segment_rows(x):
    b, s, d = x.shape
    return x.reshape(b, NSEG, s // NSEG, d).transpose(0, 2, 1, 3).reshape(b * s, d)


def _from_segment_rows(rows, b, s):
    d = rows.shape[1]
    return rows.reshape(b, s // NSEG, NSEG, d).transpose(0, 2, 1, 3).reshape(b, s, d)


def _layer(x_rows, batch, seq, alpha, layer, w_in, pool_w, pool_scale, conv_w, conv_b, lru_wa, lru_ba, lru_wx,
           lru_bx, lru_lambda, w_pool_up, w_lru_up, w_out, b_out, ln1_g, ln1_b, w_ff1, b_ff1, w_ff2, b_ff2,
           ln2_g, ln2_b):
    pool_width = pool_w.shape[0] * pool_w.shape[1]
    lru_width = conv_w.shape[1]
    o1, o2, o3 = pool_width, pool_width + lru_width, pool_width + 2 * lru_width

    x_b = x_rows.astype(BF16)
    z_a = _mm_act(x_b, w_in, layer, 0, o2, "none")
    gate_act = _mm_act(x_b, w_in, layer, o2, o3 - o2, "gelu")
    g_act = _mm_act(x_b, w_in, layer, o3, w_in.shape[2] - o3, "sigmoid")

    y_pool = _pool_branch(z_a, pool_w.astype(BF16), pool_scale, batch, seq)

    w_gates = jnp.concatenate([lru_wa[0], lru_wx[0], lru_wa[1], lru_wx[1]], axis=-1).astype(BF16)
    params = jnp.concatenate(
        [conv_w, conv_b[None], lru_ba[0][None], lru_bx[0][None], lru_lambda[0][None],
         lru_ba[1][None], lru_bx[1][None], lru_lambda[1][None]], axis=0).astype(F32)
    params = jnp.pad(params, ((0, _P_ROWS - params.shape[0]), (0, 0)))
    y_lru = _lru_branch(z_a, gate_act, w_gates, params, batch, seq, o1)

    t = batch * seq
    x1_f, x1_b = _merge(y_pool.reshape(t, -1), y_lru.reshape(t, -1), g_act, x_rows,
                        w_pool_up.astype(BF16), w_lru_up.astype(BF16), w_out.astype(BF16),
                        b_out, ln1_g, ln1_b, alpha)
    return _mlp(x1_b, x1_f, w_ff1.astype(BF16), b_ff1, w_ff2.astype(BF16), b_ff2, ln2_g, ln2_b, alpha)


def kernel(x, w_in, pool_w, pool_scale, conv_w, conv_b, lru_wa, lru_ba, lru_wx, lru_bx, lru_lambda, w_pool_up, w_lru_up, w_out, b_out, ln1_g, ln1_b, w_ff1, b_ff1, w_ff2, b_ff2, ln2_g, ln2_b):
    batch, seq, _ = x.shape
    depth = w_in.shape[0]
    alpha = (2.0 * depth) ** 0.25
    rows = _to_segment_rows(x)
    for l in range(depth):
        rows = _layer(rows, batch, seq, alpha, l, w_in, pool_w[l], pool_scale[l], conv_w[l], conv_b[l],
                      lru_wa[l], lru_ba[l], lru_wx[l], lru_bx[l], lru_lambda[l], w_pool_up[l], w_lru_up[l],
                      w_out[l], b_out[l], ln1_g[l], ln1_b[l], w_ff1[l], b_ff1[l], w_ff2[l], b_ff2[l],
                      ln2_g[l], ln2_b[l])
    return _from_segment_rows(rows, batch, seq)
```

```python
import functools

import jax
import jax.numpy as jnp
from jax import lax
from jax.experimental import pallas as pl
from jax.experimental.pallas import tpu as pltpu

SUBLANES = 8
LANES = 128
NSEG = SUBLANES

POOL_WINDOWS = (2, 4, 8, 16)
LRU_BLOCK = 256
CONV_WIDTH = 4
CONV_LEFT = CONV_WIDTH // 2
LRU_C = 8.0
LN_EPS = 1e-5
HALO = 8
SCAN_GROUPS = 8
PERM_ROWS = 128
LOG2_E = 1.4426950408889634
LN_2 = 0.6931471805599453
TINY = 1.1754944e-38

VMEM_LIMIT = 60 * 1024 * 1024

F32 = jnp.float32
BF16 = jnp.bfloat16


def _sigmoid(v):
    return 0.5 * jnp.tanh(0.5 * v) + 0.5


def _gelu_tanh(v):
    c = 0.7978845608028654
    return 0.5 * v * (1.0 + jnp.tanh(c * (v + 0.044715 * (v * v * v))))


def _layer_norm(y, g, b):
    mu = jnp.mean(y, axis=-1, keepdims=True)
    yc = y - mu
    var = jnp.mean(yc * yc, axis=-1, keepdims=True)
    return yc * lax.rsqrt(var + LN_EPS) * g + b


def _mm_act_kernel(x_ref, w_ref, o_ref, w_bf, *, act):
    @pl.when(pl.program_id(1) == 0)
    def _():
        w_bf[...] = w_ref[0].astype(BF16)

    acc = jnp.dot(x_ref[...], w_bf[...], preferred_element_type=F32)
    if act == "gelu":
        acc = _gelu_tanh(acc)
    elif act == "sigmoid":
        acc = _sigmoid(acc)
    o_ref[...] = acc.astype(o_ref.dtype)


def _mm_act(x, w, layer, col0, n, act, tm=1024, tn=1024):
    m, k = x.shape
    j0 = col0 // tn
    return pl.pallas_call(
        functools.partial(_mm_act_kernel, act=act),
        grid=(n // tn, m // tm),
        in_specs=[pl.BlockSpec((tm, k), lambda j, i: (i, 0)),
                  pl.BlockSpec((1, k, tn), lambda j, i: (layer, 0, j0 + j))],
        out_specs=pl.BlockSpec((tm, tn), lambda j, i: (i, j)),
        out_shape=jax.ShapeDtypeStruct((m, n), F32),
        scratch_shapes=[pltpu.VMEM((k, tn), BF16)],
        compiler_params=pltpu.CompilerParams(
            dimension_semantics=("arbitrary", "arbitrary"),
            vmem_limit_bytes=VMEM_LIMIT),
        name=f"mm_in_{act}",
    )(x, w)


def _load_slabs(buf, r, n):
    return jnp.concatenate([buf[q, pl.ds(r, n), :] for q in range(buf.shape[0])], axis=1)


def _store_slabs(buf, r, n, val):
    for q in range(buf.shape[0]):
        buf[q, pl.ds(r, n), :] = val[:, q * LANES:(q + 1) * LANES]


def _fill_ext(ext, u_ref, seq):
    seg_len = seq // NSEG
    h_rows = HALO * SUBLANES
    nq = ext.shape[0]

    def scatter(bi, carry):
        j0 = pl.multiple_of(bi * PERM_ROWS, PERM_ROWS)
        for s in range(NSEG):
            val = u_ref[0, pl.ds(s * seg_len + j0, PERM_ROWS), :]
            for q in range(nq):
                ext[q, pl.ds(h_rows + j0 * NSEG + s, PERM_ROWS, stride=NSEG), :] = val[:, q * LANES:(q + 1) * LANES]
        return carry

    lax.fori_loop(0, seg_len // PERM_ROWS, scatter, 0)

    zero = jnp.zeros((HALO, LANES), F32)
    for s in range(NSEG):
        nxt = u_ref[0, pl.ds((s + 1) * seg_len, HALO), :] if s + 1 < NSEG else None
        prv = u_ref[0, pl.ds(s * seg_len - HALO, HALO), :] if s > 0 else None
        for q in range(nq):
            lanes = slice(q * LANES, (q + 1) * LANES)
            ext[q, pl.ds(h_rows + seq + s, HALO, stride=NSEG), :] = zero if nxt is None else nxt[:, lanes]
            ext[q, pl.ds(s, HALO, stride=NSEG), :] = zero if prv is None else prv[:, lanes]


def _emit_time_order(src, o_ref, seq, post):
    seg_len = seq // NSEG

    def gather(bi, carry):
        j0 = pl.multiple_of(bi * PERM_ROWS, PERM_ROWS)
        for s in range(NSEG):
            val = jnp.concatenate(
                [src[q, pl.ds(j0 * NSEG + s, PERM_ROWS, stride=NSEG), :] for q in range(src.shape[0])], axis=1)
            r_time = s * seg_len + j0
            o_ref[0, pl.ds(r_time, PERM_ROWS), :] = post(val, r_time).astype(o_ref.dtype)
        return carry

    lax.fori_loop(0, seg_len // PERM_ROWS, gather, 0)


def _pool_kernel(u_ref, w_ref, s_ref, o_ref, ext, y_buf, *, seq, chunk):
    seg_len = seq // NSEG
    h_rows = HALO * SUBLANES
    nq = ext.shape[0]
    _fill_ext(ext, u_ref, seq)
    grp = pl.program_id(1)
    n_chunks = seq // chunk

    def body(win):
        half = win // 2

        def do_chunk(ci, carry):
            r0 = pl.multiple_of(ci * chunk, chunk)
            base = r0 + h_rows
            row = r0 + lax.broadcasted_iota(jnp.int32, (chunk, LANES), 0)
            t = (row & (SUBLANES - 1)) * seg_len + (row >> 3)
            cnt = jnp.minimum(t + half, seq) - jnp.maximum(t - half, 0)
            inv = 1.0 / cnt.astype(F32)
            d = []
            for q in range(nq):
                tot = ext[q, pl.ds(base - half * SUBLANES, chunk), :]
                for m in range(-half + 1, half):
                    tot = tot + ext[q, pl.ds(base + m * SUBLANES, chunk), :]
                d.append(tot * inv - ext[q, pl.ds(base, chunk), :])
            d = jnp.concatenate(d, axis=1)
            y = jnp.dot(d.astype(BF16), w_ref[0], preferred_element_type=F32) * s_ref[0]
            _store_slabs(y_buf, r0, chunk, y)
            return carry

        lax.fori_loop(0, n_chunks, do_chunk, 0)

    for gi, win in enumerate(POOL_WINDOWS):
        pl.when(grp == gi)(functools.partial(body, win))

    _emit_time_order(y_buf, o_ref, seq, lambda val, r_time: val)


def _pool_branch(z_a, pool_w, pool_scale, batch, seq, chunk=512):
    n_groups, gw = pool_w.shape[0], pool_w.shape[1]
    z3 = z_a.reshape(batch, seq, z_a.shape[1])
    nq = gw // LANES
    return pl.pallas_call(
        functools.partial(_pool_kernel, seq=seq, chunk=chunk),
        grid=(batch, n_groups),
        in_specs=[pl.BlockSpec((1, seq, gw), lambda b, g: (b, 0, g)),
                  pl.BlockSpec((1, gw, gw), lambda b, g: (g, 0, 0)),
                  pl.BlockSpec((1, 1, gw), lambda b, g: (g, 0, 0))],
        out_specs=pl.BlockSpec((1, seq, gw), lambda b, g: (b, 0, g)),
        out_shape=jax.ShapeDtypeStruct((batch, seq, n_groups * gw), BF16),
        scratch_shapes=[pltpu.VMEM((nq, seq + 2 * HALO * SUBLANES, LANES), F32),
                        pltpu.VMEM((nq, seq, LANES), F32)],
        compiler_params=pltpu.CompilerParams(
            dimension_semantics=("arbitrary", "arbitrary"),
            vmem_limit_bytes=VMEM_LIMIT),
        name="pool_branch",
    )(z3, pool_w, pool_scale.reshape(n_groups, 1, gw))


_P_CONV_W = 0
_P_CONV_B = CONV_WIDTH
_P_DIR = CONV_WIDTH + 1
_P_ROWS = 16


def _lru_kernel(u_ref, gate_ref, w_ref, p_ref, o_ref, ext, a_f, x_f, a_b, x_b, hf_buf, *, seq, chunk):
    h_rows = HALO * SUBLANES
    c = LRU_BLOCK
    slab = SCAN_GROUPS * SUBLANES
    n_slabs = seq // slab
    _fill_ext(ext, u_ref, seq)

    conv_hw = [0.5 * p_ref[pl.ds(_P_CONV_W + k, 1), :] for k in range(CONV_WIDTH)]
    conv_hb = 0.5 * p_ref[pl.ds(_P_CONV_B, 1), :]
    hb_a, hb_x, c2 = [], [], []
    for d in range(2):
        hb_a.append(0.5 * p_ref[pl.ds(_P_DIR + 3 * d, 1), :])
        hb_x.append(0.5 * p_ref[pl.ds(_P_DIR + 3 * d + 1, 1), :])
        lam = p_ref[pl.ds(_P_DIR + 3 * d + 2, 1), :]
        c2.append((-0.5 * LRU_C * LOG2_E) * jax.nn.softplus(-lam))
    a_out = (a_f, a_b)
    x_out = (x_f, x_b)

    def gates(ci, carry):
        r0 = pl.multiple_of(ci * chunk, chunk)
        base = r0 + h_rows
        xh = conv_hb + _load_slabs(ext, base - CONV_LEFT * SUBLANES, chunk) * conv_hw[0]
        for k in range(1, CONV_WIDTH):
            xh = xh + _load_slabs(ext, base + (k - CONV_LEFT) * SUBLANES, chunk) * conv_hw[k]
        pre = jnp.dot(xh.astype(BF16), w_ref[0], preferred_element_type=F32)
        for d in range(2):
            t_r = jnp.tanh(pre[:, (2 * d) * c:(2 * d + 1) * c] + hb_a[d])
            t_i = jnp.tanh(pre[:, (2 * d + 1) * c:(2 * d + 2) * c] + hb_x[d])
            log2_a = c2[d] * t_r + c2[d]
            a = jnp.exp2(log2_a)
            one_m_a2 = (-1.0 - a * a) * jnp.tanh(LN_2 * log2_a)
            root = one_m_a2 * lax.rsqrt(jnp.maximum(one_m_a2, TINY))
            _store_slabs(a_out[d], r0, chunk, a)
            _store_slabs(x_out[d], r0, chunk, root * (t_i * xh + xh))
        return carry

    lax.fori_loop(0, seq // chunk, gates, 0)

    def rows(v, k):
        return v[k * SUBLANES:(k + 1) * SUBLANES]

    def local_scan(it, carry):
        h_f, p_f, h_b, p_b = carry
        rf = pl.multiple_of(it * slab, slab)
        rb = pl.multiple_of((n_slabs - 1 - it) * slab, slab)
        af, xf = _load_slabs(a_f, rf, slab), _load_slabs(x_f, rf, slab)
        ab, xb = _load_slabs(a_b, rb, slab), _load_slabs(x_b, rb, slab)
        for k in range(SCAN_GROUPS):
            kb = SCAN_GROUPS - 1 - k
            h_f = rows(af, k) * h_f + rows(xf, k)
            p_f = rows(af, k) * p_f
            h_b = rows(ab, kb) * h_b + rows(xb, kb)
            p_b = rows(ab, kb) * p_b
        return h_f, p_f, h_b, p_b

    zeros = jnp.zeros((SUBLANES, c), F32)
    ones = jnp.ones((SUBLANES, c), F32)
    h_f, p_f, h_b, p_b = lax.fori_loop(0, n_slabs, local_scan, (zeros, ones, zeros, ones))

    sub = lax.broadcasted_iota(jnp.int32, (SUBLANES, c), 0)
    c_f = zeros
    c_b = zeros
    for _ in range(NSEG - 1):
        c_f = jnp.where(sub == 0, 0.0, pltpu.roll(h_f + p_f * c_f, 1, 0))
        c_b = jnp.where(sub == SUBLANES - 1, 0.0, pltpu.roll(h_b + p_b * c_b, SUBLANES - 1, 0))

    def fwd(it, h):
        rf = pl.multiple_of(it * slab, slab)
        af, xf = _load_slabs(a_f, rf, slab), _load_slabs(x_f, rf, slab)
        out = []
        for k in range(SCAN_GROUPS):
            h = rows(af, k) * h + rows(xf, k)
            out.append(h)
        _store_slabs(hf_buf, rf, slab, jnp.concatenate(out, axis=0))
        return h

    lax.fori_loop(0, n_slabs, fwd, c_f)

    def bwd(it, h):
        rb = pl.multiple_of((n_slabs - 1 - it) * slab, slab)
        ab, xb = _load_slabs(a_b, rb, slab), _load_slabs(x_b, rb, slab)
        hf = _load_slabs(hf_buf, rb, slab)
        out = [None] * SCAN_GROUPS
        for kb in range(SCAN_GROUPS - 1, -1, -1):
            h = rows(ab, kb) * h + rows(xb, kb)
            out[kb] = h + rows(hf, kb)
        _store_slabs(x_f, rb, slab, jnp.concatenate(out, axis=0))
        return h

    lax.fori_loop(0, n_slabs, bwd, c_b)

    _emit_time_order(x_f, o_ref, seq, lambda val, r_time: val * gate_ref[0, pl.ds(r_time, PERM_ROWS), :])


def _lru_branch(z_a, gate_act, w_gates, params, batch, seq, lru_col0, chunk=256):
    n_heads = w_gates.shape[0]
    c = LRU_BLOCK
    nq = c // LANES
    z3 = z_a.reshape(batch, seq, z_a.shape[1])
    g3 = gate_act.reshape(batch, seq, gate_act.shape[1])
    col0 = lru_col0 // c
    return pl.pallas_call(
        functools.partial(_lru_kernel, seq=seq, chunk=chunk),
        grid=(batch, n_heads),
        in_specs=[pl.BlockSpec((1, seq, c), lambda b, h: (b, 0, col0 + h)),
                  pl.BlockSpec((1, seq, c), lambda b, h: (b, 0, h)),
                  pl.BlockSpec((1, c, 4 * c), lambda b, h: (h, 0, 0)),
                  pl.BlockSpec((_P_ROWS, c), lambda b, h: (0, h))],
        out_specs=pl.BlockSpec((1, seq, c), lambda b, h: (b, 0, h)),
        out_shape=jax.ShapeDtypeStruct((batch, seq, n_heads * c), BF16),
        scratch_shapes=[pltpu.VMEM((nq, seq + 2 * HALO * SUBLANES, LANES), F32)]
        + [pltpu.VMEM((nq, seq, LANES), F32) for _ in range(5)],
        compiler_params=pltpu.CompilerParams(
            dimension_semantics=("arbitrary", "arbitrary"),
            vmem_limit_bytes=VMEM_LIMIT),
        name="lru_branch",
    )(z3, g3, w_gates, params)


def _merge_kernel(yp_ref, yl_ref, g0_ref, g1_ref, x_ref, wp_ref, wl_ref, wo_ref, bo_ref, g_ref, b_ref,
                  of_ref, ob_ref, *, alpha):
    up_p = jnp.dot(yp_ref[...], wp_ref[...], preferred_element_type=F32)
    up_l = jnp.dot(yl_ref[...], wl_ref[...], preferred_element_type=F32)
    m = g0_ref[...] * up_p + g1_ref[...] * up_l
    mix = jnp.dot(m.astype(BF16), wo_ref[...], preferred_element_type=F32) + bo_ref[...]
    y = _layer_norm(alpha * x_ref[...] + mix, g_ref[...], b_ref[...])
    of_ref[...] = y
    ob_ref[...] = y.astype(BF16)


def _merge(y_pool, y_lru, g_act, x_rows, w_pool_up, w_lru_up, w_out, b_out, ln_g, ln_b, alpha, tm=256):
    t, d = x_rows.shape
    row = lambda i: (i, 0)
    const = lambda i: (0, 0)
    wspec = pl.BlockSpec((d, d), const, pipeline_mode=pl.Buffered(1))
    vspec = pl.BlockSpec((1, d), const)
    return pl.pallas_call(
        functools.partial(_merge_kernel, alpha=alpha),
        grid=(t // tm,),
        in_specs=[pl.BlockSpec((tm, d), row), pl.BlockSpec((tm, d), row),
                  pl.BlockSpec((tm, d), lambda i: (i, 0)), pl.BlockSpec((tm, d), lambda i: (i, 1)),
                  pl.BlockSpec((tm, d), row), wspec, wspec, wspec, vspec, vspec, vspec],
        out_specs=[pl.BlockSpec((tm, d), row), pl.BlockSpec((tm, d), row)],
        out_shape=[jax.ShapeDtypeStruct((t, d), F32), jax.ShapeDtypeStruct((t, d), BF16)],
        compiler_params=pltpu.CompilerParams(
            dimension_semantics=("arbitrary",), vmem_limit_bytes=VMEM_LIMIT),
        name="merge_out_ln",
    )(y_pool, y_lru, g_act, g_act, x_rows, w_pool_up, w_lru_up, w_out,
      b_out.reshape(1, d), ln_g.reshape(1, d), ln_b.reshape(1, d))


def _mlp_kernel(xb_ref, xf_ref, w1_ref, b1_ref, w2_ref, b2_ref, g_ref, b_ref, o_ref, *, alpha):
    f = pl.program_id(1)

    @pl.when(f == 0)
    def _():
        o_ref[...] = jnp.zeros_like(o_ref)

    h = jnp.dot(xb_ref[...], w1_ref[...], preferred_element_type=F32) + b1_ref[...]
    h = jnp.square(jnp.maximum(h, 0.0))
    o_ref[...] += jnp.dot(h.astype(BF16), w2_ref[...], preferred_element_type=F32)

    @pl.when(f == pl.num_programs(1) - 1)
    def _():
        y = alpha * xf_ref[...] + o_ref[...] + b2_ref[...]
        o_ref[...] = _layer_norm(y, g_ref[...], b_ref[...])


def _mlp(x_b, x_f, w1, b1, w2, b2, ln_g, ln_b, alpha, tm=1024, tf=512):
    t, d = x_f.shape
    dff = w1.shape[1]
    return pl.pallas_call(
        functools.partial(_mlp_kernel, alpha=alpha),
        grid=(t // tm, dff // tf),
        in_specs=[pl.BlockSpec((tm, d), lambda i, f: (i, 0)),
                  pl.BlockSpec((tm, d), lambda i, f: (i, 0), pipeline_mode=pl.Buffered(1)),
                  pl.BlockSpec((d, tf), lambda i, f: (0, f)),
                  pl.BlockSpec((1, tf), lambda i, f: (0, f)),
                  pl.BlockSpec((tf, d), lambda i, f: (f, 0)),
                  pl.BlockSpec((1, d), lambda i, f: (0, 0)),
                  pl.BlockSpec((1, d), lambda i, f: (0, 0)),
                  pl.BlockSpec((1, d), lambda i, f: (0, 0))],
        out_specs=pl.BlockSpec((tm, d), lambda i, f: (i, 0)),
        out_shape=jax.ShapeDtypeStruct((t, d), F32),
        compiler_params=pltpu.CompilerParams(
            dimension_semantics=("arbitrary", "arbitrary"), vmem_limit_bytes=VMEM_LIMIT),
        name="mlp_ln",
    )(x_b, x_f, w1, b1.reshape(1, dff), w2, b2.reshape(1, d), ln_g.reshape(1, d), ln_b.reshape(1, d))


def _layer(x_rows, batch, seq, alpha, layer, w_in, pool_w, pool_scale, conv_w, conv_b, lru_wa, lru_ba, lru_wx,
           lru_bx, lru_lambda, w_pool_up, w_lru_up, w_out, b_out, ln1_g, ln1_b, w_ff1, b_ff1, w_ff2, b_ff2,
           ln2_g, ln2_b):
    pool_width = pool_w.shape[0] * pool_w.shape[1]
    lru_width = conv_w.shape[1]
    o1, o2, o3 = pool_width, pool_width + lru_width, pool_width + 2 * lru_width

    x_b = x_rows.astype(BF16)
    z_a = _mm_act(x_b, w_in, layer, 0, o2, "none")
    gate_act = _mm_act(x_b, w_in, layer, o2, o3 - o2, "gelu")
    g_act = _mm_act(x_b, w_in, layer, o3, w_in.shape[2] - o3, "sigmoid")

    y_pool = _pool_branch(z_a, pool_w.astype(BF16), pool_scale, batch, seq)

    w_gates = jnp.concatenate([lru_wa[0], lru_wx[0], lru_wa[1], lru_wx[1]], axis=-1).astype(BF16)
    params = jnp.concatenate(
        [conv_w, conv_b[None], lru_ba[0][None], lru_bx[0][None], lru_lambda[0][None],
         lru_ba[1][None], lru_bx[1][None], lru_lambda[1][None]], axis=0).astype(F32)
    params = jnp.pad(params, ((0, _P_ROWS - params.shape[0]), (0, 0)))
    y_lru = _lru_branch(z_a, gate_act, w_gates, params, batch, seq, o1)

    t = batch * seq
    x1_f, x1_b = _merge(y_pool.reshape(t, -1), y_lru.reshape(t, -1), g_act, x_rows,
                        w_pool_up.astype(BF16), w_lru_up.astype(BF16), w_out.astype(BF16),
                        b_out, ln1_g, ln1_b, alpha)
    return _mlp(x1_b, x1_f, w_ff1.astype(BF16), b_ff1, w_ff2.astype(BF16), b_ff2, ln2_g, ln2_b, alpha)


def kernel(x, w_in, pool_w, pool_scale, conv_w, conv_b, lru_wa, lru_ba, lru_wx, lru_bx, lru_lambda, w_pool_up, w_lru_up, w_out, b_out, ln1_g, ln1_b, w_ff1, b_ff1, w_ff2, b_ff2, ln2_g, ln2_b):
    batch, seq, d = x.shape
    depth = w_in.shape[0]
    alpha = (2.0 * depth) ** 0.25
    rows = x.reshape(batch * seq, d)
    for l in range(depth):
        rows = _layer(rows, batch, seq, alpha, l, w_in, pool_w[l], pool_scale[l], conv_w[l], conv_b[l],
                      lru_wa[l], lru_ba[l], lru_wx[l], lru_bx[l], lru_lambda[l], w_pool_up[l], w_lru_up[l],
                      w_out[l], b_out[l], ln1_g[l], ln1_b[l], w_ff1[l], b_ff1[l], w_ff2[l], b_ff2[l],
                      ln2_g[l], ln2_b[l])
    return rows.reshape(batch, seq, d)
```

```python
import functools

import jax
import jax.numpy as jnp
from jax import lax
from jax.experimental import pallas as pl
from jax.experimental.pallas import tpu as pltpu

SUBLANES = 8
LANES = 128
NSEG = SUBLANES

POOL_WINDOWS = (2, 4, 8, 16)
LRU_BLOCK = 256
CONV_WIDTH = 4
CONV_LEFT = CONV_WIDTH // 2
LRU_C = 8.0
LN_EPS = 1e-5
HALO = 8
SCAN_GROUPS = 8
PERM_ROWS = 128
LOG2_E = 1.4426950408889634
LN_2 = 0.6931471805599453
TINY = 1.1754944e-38

VMEM_LIMIT = 60 * 1024 * 1024

F32 = jnp.float32
BF16 = jnp.bfloat16


def _sigmoid(v):
    return 0.5 * jnp.tanh(0.5 * v) + 0.5


def _gelu_tanh(v):
    c = 0.7978845608028654
    return 0.5 * v * (1.0 + jnp.tanh(c * (v + 0.044715 * (v * v * v))))


def _layer_norm(y, g, b):
    mu = jnp.mean(y, axis=-1, keepdims=True)
    yc = y - mu
    var = jnp.mean(yc * yc, axis=-1, keepdims=True)
    return yc * lax.rsqrt(var + LN_EPS) * g + b


def _mm_act_kernel(x_ref, w_ref, o_ref, w_bf, *, act):
    @pl.when(pl.program_id(1) == 0)
    def _():
        w_bf[...] = w_ref[0].astype(BF16)

    acc = jnp.dot(x_ref[...].astype(BF16), w_bf[...], preferred_element_type=F32)
    if act == "gelu":
        acc = _gelu_tanh(acc)
    elif act == "sigmoid":
        acc = _sigmoid(acc)
    o_ref[...] = acc.astype(o_ref.dtype)


def _mm_act(x, w, layer, col0, n, act, tm=1024, tn=1024):
    m, k = x.shape
    j0 = col0 // tn
    return pl.pallas_call(
        functools.partial(_mm_act_kernel, act=act),
        grid=(n // tn, m // tm),
        in_specs=[pl.BlockSpec((tm, k), lambda j, i: (i, 0)),
                  pl.BlockSpec((1, k, tn), lambda j, i: (layer, 0, j0 + j))],
        out_specs=pl.BlockSpec((tm, tn), lambda j, i: (i, j)),
        out_shape=jax.ShapeDtypeStruct((m, n), F32),
        scratch_shapes=[pltpu.VMEM((k, tn), BF16)],
        compiler_params=pltpu.CompilerParams(
            dimension_semantics=("arbitrary", "arbitrary"),
            vmem_limit_bytes=VMEM_LIMIT),
        name=f"mm_in_{act}",
    )(x, w)


def _load_slabs(buf, r, n):
    return jnp.concatenate([buf[q, pl.ds(r, n), :] for q in range(buf.shape[0])], axis=1)


def _store_slabs(buf, r, n, val):
    for q in range(buf.shape[0]):
        buf[q, pl.ds(r, n), :] = val[:, q * LANES:(q + 1) * LANES]


def _fill_ext(ext, u_ref, seq):
    seg_len = seq // NSEG
    h_rows = HALO * SUBLANES
    nq = ext.shape[0]

    def scatter(bi, carry):
        j0 = pl.multiple_of(bi * PERM_ROWS, PERM_ROWS)
        for s in range(NSEG):
            val = u_ref[0, pl.ds(s * seg_len + j0, PERM_ROWS), :]
            for q in range(nq):
                ext[q, pl.ds(h_rows + j0 * NSEG + s, PERM_ROWS, stride=NSEG), :] = val[:, q * LANES:(q + 1) * LANES]
        return carry

    lax.fori_loop(0, seg_len // PERM_ROWS, scatter, 0)

    zero = jnp.zeros((HALO, LANES), F32)
    for s in range(NSEG):
        nxt = u_ref[0, pl.ds((s + 1) * seg_len, HALO), :] if s + 1 < NSEG else None
        prv = u_ref[0, pl.ds(s * seg_len - HALO, HALO), :] if s > 0 else None
        for q in range(nq):
            lanes = slice(q * LANES, (q + 1) * LANES)
            ext[q, pl.ds(h_rows + seq + s, HALO, stride=NSEG), :] = zero if nxt is None else nxt[:, lanes]
            ext[q, pl.ds(s, HALO, stride=NSEG), :] = zero if prv is None else prv[:, lanes]


def _emit_time_order(src, o_ref, seq, post):
    seg_len = seq // NSEG

    def gather(bi, carry):
        j0 = pl.multiple_of(bi * PERM_ROWS, PERM_ROWS)
        for s in range(NSEG):
            val = jnp.concatenate(
                [src[q, pl.ds(j0 * NSEG + s, PERM_ROWS, stride=NSEG), :] for q in range(src.shape[0])], axis=1)
            r_time = s * seg_len + j0
            o_ref[0, pl.ds(r_time, PERM_ROWS), :] = post(val, r_time).astype(o_ref.dtype)
        return carry

    lax.fori_loop(0, seg_len // PERM_ROWS, gather, 0)


def _pool_kernel(u_ref, w_ref, s_ref, o_ref, ext, y_buf, *, seq, chunk):
    seg_len = seq // NSEG
    h_rows = HALO * SUBLANES
    nq = ext.shape[0]
    _fill_ext(ext, u_ref, seq)
    grp = pl.program_id(1)
    n_chunks = seq // chunk

    def body(win):
        half = win // 2

        def do_chunk(ci, carry):
            r0 = pl.multiple_of(ci * chunk, chunk)
            base = r0 + h_rows
            row = r0 + lax.broadcasted_iota(jnp.int32, (chunk, LANES), 0)
            t = (row & (SUBLANES - 1)) * seg_len + (row >> 3)
            cnt = jnp.minimum(t + half, seq) - jnp.maximum(t - half, 0)
            inv = 1.0 / cnt.astype(F32)
            d = []
            for q in range(nq):
                tot = ext[q, pl.ds(base - half * SUBLANES, chunk), :]
                for m in range(-half + 1, half):
                    tot = tot + ext[q, pl.ds(base + m * SUBLANES, chunk), :]
                d.append(tot * inv - ext[q, pl.ds(base, chunk), :])
            d = jnp.concatenate(d, axis=1)
            y = jnp.dot(d.astype(BF16), w_ref[0], preferred_element_type=F32) * s_ref[0]
            _store_slabs(y_buf, r0, chunk, y)
            return carry

        lax.fori_loop(0, n_chunks, do_chunk, 0)

    for gi, win in enumerate(POOL_WINDOWS):
        pl.when(grp == gi)(functools.partial(body, win))

    _emit_time_order(y_buf, o_ref, seq, lambda val, r_time: val)


def _pool_branch(z_a, pool_w, pool_scale, batch, seq, chunk=512):
    n_groups, gw = pool_w.shape[0], pool_w.shape[1]
    z3 = z_a.reshape(batch, seq, z_a.shape[1])
    nq = gw // LANES
    return pl.pallas_call(
        functools.partial(_pool_kernel, seq=seq, chunk=chunk),
        grid=(batch, n_groups),
        in_specs=[pl.BlockSpec((1, seq, gw), lambda b, g: (b, 0, g)),
                  pl.BlockSpec((1, gw, gw), lambda b, g: (g, 0, 0)),
                  pl.BlockSpec((1, 1, gw), lambda b, g: (g, 0, 0))],
        out_specs=pl.BlockSpec((1, seq, gw), lambda b, g: (b, 0, g)),
        out_shape=jax.ShapeDtypeStruct((batch, seq, n_groups * gw), BF16),
        scratch_shapes=[pltpu.VMEM((nq, seq + 2 * HALO * SUBLANES, LANES), F32),
                        pltpu.VMEM((nq, seq, LANES), F32)],
        compiler_params=pltpu.CompilerParams(
            dimension_semantics=("arbitrary", "arbitrary"),
            vmem_limit_bytes=VMEM_LIMIT),
        name="pool_branch",
    )(z3, pool_w, pool_scale.reshape(n_groups, 1, gw))


_P_CONV_W = 0
_P_CONV_B = CONV_WIDTH
_P_DIR = CONV_WIDTH + 1
_P_ROWS = 16


def _lru_kernel(u_ref, gate_ref, w_ref, p_ref, o_ref, ext, a_f, x_f, a_b, x_b, hf_buf, *, seq, chunk):
    h_rows = HALO * SUBLANES
    c = LRU_BLOCK
    slab = SCAN_GROUPS * SUBLANES
    n_slabs = seq // slab
    _fill_ext(ext, u_ref, seq)

    conv_hw = [0.5 * p_ref[pl.ds(_P_CONV_W + k, 1), :] for k in range(CONV_WIDTH)]
    conv_hb = 0.5 * p_ref[pl.ds(_P_CONV_B, 1), :]
    hb_a, hb_x, c2 = [], [], []
    for d in range(2):
        hb_a.append(0.5 * p_ref[pl.ds(_P_DIR + 3 * d, 1), :])
        hb_x.append(0.5 * p_ref[pl.ds(_P_DIR + 3 * d + 1, 1), :])
        lam = p_ref[pl.ds(_P_DIR + 3 * d + 2, 1), :]
        c2.append((-0.5 * LRU_C * LOG2_E) * jax.nn.softplus(-lam))
    a_out = (a_f, a_b)
    x_out = (x_f, x_b)

    def gates(ci, carry):
        r0 = pl.multiple_of(ci * chunk, chunk)
        base = r0 + h_rows
        xh = conv_hb + _load_slabs(ext, base - CONV_LEFT * SUBLANES, chunk) * conv_hw[0]
        for k in range(1, CONV_WIDTH):
            xh = xh + _load_slabs(ext, base + (k - CONV_LEFT) * SUBLANES, chunk) * conv_hw[k]
        pre = jnp.dot(xh.astype(BF16), w_ref[0], preferred_element_type=F32)
        for d in range(2):
            t_r = jnp.tanh(pre[:, (2 * d) * c:(2 * d + 1) * c] + hb_a[d])
            t_i = jnp.tanh(pre[:, (2 * d + 1) * c:(2 * d + 2) * c] + hb_x[d])
            log2_a = c2[d] * t_r + c2[d]
            a = jnp.exp2(log2_a)
            one_m_a2 = (-1.0 - a * a) * jnp.tanh(LN_2 * log2_a)
            root = one_m_a2 * lax.rsqrt(jnp.maximum(one_m_a2, TINY))
            _store_slabs(a_out[d], r0, chunk, a)
            _store_slabs(x_out[d], r0, chunk, root * (t_i * xh + xh))
        return carry

    lax.fori_loop(0, seq // chunk, gates, 0)

    def rows(v, k):
        return v[k * SUBLANES:(k + 1) * SUBLANES]

    def local_scan(it, carry):
        h_f, p_f, h_b, p_b = carry
        rf = pl.multiple_of(it * slab, slab)
        rb = pl.multiple_of((n_slabs - 1 - it) * slab, slab)
        af, xf = _load_slabs(a_f, rf, slab), _load_slabs(x_f, rf, slab)
        ab, xb = _load_slabs(a_b, rb, slab), _load_slabs(x_b, rb, slab)
        for k in range(SCAN_GROUPS):
            kb = SCAN_GROUPS - 1 - k
            h_f = rows(af, k) * h_f + rows(xf, k)
            p_f = rows(af, k) * p_f
            h_b = rows(ab, kb) * h_b + rows(xb, kb)
            p_b = rows(ab, kb) * p_b
        return h_f, p_f, h_b, p_b

    zeros = jnp.zeros((SUBLANES, c), F32)
    ones = jnp.ones((SUBLANES, c), F32)
    h_f, p_f, h_b, p_b = lax.fori_loop(0, n_slabs, local_scan, (zeros, ones, zeros, ones))

    sub = lax.broadcasted_iota(jnp.int32, (SUBLANES, c), 0)
    c_f = zeros
    c_b = zeros
    for _ in range(NSEG - 1):
        c_f = jnp.where(sub == 0, 0.0, pltpu.roll(h_f + p_f * c_f, 1, 0))
        c_b = jnp.where(sub == SUBLANES - 1, 0.0, pltpu.roll(h_b + p_b * c_b, SUBLANES - 1, 0))

    def fwd(it, h):
        rf = pl.multiple_of(it * slab, slab)
        af, xf = _load_slabs(a_f, rf, slab), _load_slabs(x_f, rf, slab)
        out = []
        for k in range(SCAN_GROUPS):
            h = rows(af, k) * h + rows(xf, k)
            out.append(h)
        _store_slabs(hf_buf, rf, slab, jnp.concatenate(out, axis=0))
        return h

    lax.fori_loop(0, n_slabs, fwd, c_f)

    def bwd(it, h):
        rb = pl.multiple_of((n_slabs - 1 - it) * slab, slab)
        ab, xb = _load_slabs(a_b, rb, slab), _load_slabs(x_b, rb, slab)
        hf = _load_slabs(hf_buf, rb, slab)
        out = [None] * SCAN_GROUPS
        for kb in range(SCAN_GROUPS - 1, -1, -1):
            h = rows(ab, kb) * h + rows(xb, kb)
            out[kb] = h + rows(hf, kb)
        _store_slabs(x_f, rb, slab, jnp.concatenate(out, axis=0))
        return h

    lax.fori_loop(0, n_slabs, bwd, c_b)

    _emit_time_order(x_f, o_ref, seq, lambda val, r_time: val * gate_ref[0, pl.ds(r_time, PERM_ROWS), :])


def _lru_branch(z_a, gate_act, w_gates, params, batch, seq, lru_col0, chunk=256):
    n_heads = w_gates.shape[0]
    c = LRU_BLOCK
    nq = c // LANES
    z3 = z_a.reshape(batch, seq, z_a.shape[1])
    g3 = gate_act.reshape(batch, seq, gate_act.shape[1])
    col0 = lru_col0 // c
    return pl.pallas_call(
        functools.partial(_lru_kernel, seq=seq, chunk=chunk),
        grid=(batch, n_heads),
        in_specs=[pl.BlockSpec((1, seq, c), lambda b, h: (b, 0, col0 + h)),
                  pl.BlockSpec((1, seq, c), lambda b, h: (b, 0, h)),
                  pl.BlockSpec((1, c, 4 * c), lambda b, h: (h, 0, 0)),
                  pl.BlockSpec((_P_ROWS, c), lambda b, h: (0, h))],
        out_specs=pl.BlockSpec((1, seq, c), lambda b, h: (b, 0, h)),
        out_shape=jax.ShapeDtypeStruct((batch, seq, n_heads * c), BF16),
        scratch_shapes=[pltpu.VMEM((nq, seq + 2 * HALO * SUBLANES, LANES), F32)]
        + [pltpu.VMEM((nq, seq, LANES), F32) for _ in range(5)],
        compiler_params=pltpu.CompilerParams(
            dimension_semantics=("arbitrary", "arbitrary"),
            vmem_limit_bytes=VMEM_LIMIT),
        name="lru_branch",
    )(z3, g3, w_gates, params)


def _merge_kernel(yp_ref, yl_ref, g0_ref, g1_ref, x_ref, wp_ref, wl_ref, wo_ref, bo_ref, g_ref, b_ref,
                  of_ref, ob_ref, *, alpha):
    up_p = jnp.dot(yp_ref[...], wp_ref[...], preferred_element_type=F32)
    up_l = jnp.dot(yl_ref[...], wl_ref[...], preferred_element_type=F32)
    m = g0_ref[...] * up_p + g1_ref[...] * up_l
    mix = jnp.dot(m.astype(BF16), wo_ref[...], preferred_element_type=F32) + bo_ref[...]
    y = _layer_norm(alpha * x_ref[...] + mix, g_ref[...], b_ref[...])
    of_ref[...] = y
    ob_ref[...] = y.astype(BF16)


def _merge(y_pool, y_lru, g_act, x_rows, w_pool_up, w_lru_up, w_out, b_out, ln_g, ln_b, alpha, tm=256):
    t, d = x_rows.shape
    row = lambda i: (i, 0)
    const = lambda i: (0, 0)
    wspec = pl.BlockSpec((d, d), const, pipeline_mode=pl.Buffered(1))
    vspec = pl.BlockSpec((1, d), const)
    return pl.pallas_call(
        functools.partial(_merge_kernel, alpha=alpha),
        grid=(t // tm,),
        in_specs=[pl.BlockSpec((tm, d), row), pl.BlockSpec((tm, d), row),
                  pl.BlockSpec((tm, d), lambda i: (i, 0)), pl.BlockSpec((tm, d), lambda i: (i, 1)),
                  pl.BlockSpec((tm, d), row), wspec, wspec, wspec, vspec, vspec, vspec],
        out_specs=[pl.BlockSpec((tm, d), row), pl.BlockSpec((tm, d), row)],
        out_shape=[jax.ShapeDtypeStruct((t, d), F32), jax.ShapeDtypeStruct((t, d), BF16)],
        compiler_params=pltpu.CompilerParams(
            dimension_semantics=("arbitrary",), vmem_limit_bytes=VMEM_LIMIT),
        name="merge_out_ln",
    )(y_pool, y_lru, g_act, g_act, x_rows, w_pool_up, w_lru_up, w_out,
      b_out.reshape(1, d), ln_g.reshape(1, d), ln_b.reshape(1, d))


def _mlp_kernel(xb_ref, xf_ref, w1_ref, b1_ref, w2_ref, b2_ref, g_ref, b_ref, o_ref, *, alpha):
    f = pl.program_id(1)

    @pl.when(f == 0)
    def _():
        o_ref[...] = jnp.zeros_like(o_ref)

    h = jnp.dot(xb_ref[...], w1_ref[0].astype(BF16), preferred_element_type=F32) + b1_ref[...]
    h = jnp.square(jnp.maximum(h, 0.0))
    o_ref[...] += jnp.dot(h.astype(BF16), w2_ref[0].astype(BF16), preferred_element_type=F32)

    @pl.when(f == pl.num_programs(1) - 1)
    def _():
        y = alpha * xf_ref[...] + o_ref[...] + b2_ref[...]
        o_ref[...] = _layer_norm(y, g_ref[...], b_ref[...])


def _mlp(x_b, x_f, w1, layer, b1, w2, b2, ln_g, ln_b, alpha, tm=1024, tf=512):
    t, d = x_f.shape
    dff = w1.shape[2]
    return pl.pallas_call(
        functools.partial(_mlp_kernel, alpha=alpha),
        grid=(t // tm, dff // tf),
        in_specs=[pl.BlockSpec((tm, d), lambda i, f: (i, 0)),
                  pl.BlockSpec((tm, d), lambda i, f: (i, 0), pipeline_mode=pl.Buffered(1)),
                  pl.BlockSpec((1, d, tf), lambda i, f: (layer, 0, f)),
                  pl.BlockSpec((1, tf), lambda i, f: (0, f)),
                  pl.BlockSpec((1, tf, d), lambda i, f: (layer, f, 0)),
                  pl.BlockSpec((1, d), lambda i, f: (0, 0)),
                  pl.BlockSpec((1, d), lambda i, f: (0, 0)),
                  pl.BlockSpec((1, d), lambda i, f: (0, 0))],
        out_specs=pl.BlockSpec((tm, d), lambda i, f: (i, 0)),
        out_shape=jax.ShapeDtypeStruct((t, d), F32),
        compiler_params=pltpu.CompilerParams(
            dimension_semantics=("arbitrary", "arbitrary"), vmem_limit_bytes=VMEM_LIMIT),
        name="mlp_ln",
    )(x_b, x_f, w1, b1.reshape(1, dff), w2, b2.reshape(1, d), ln_g.reshape(1, d), ln_b.reshape(1, d))


def _layer(x_rows, batch, seq, alpha, layer, w_in, pool_w, pool_scale, conv_w, conv_b, lru_wa, lru_ba, lru_wx,
           lru_bx, lru_lambda, w_pool_up, w_lru_up, w_out, b_out, ln1_g, ln1_b, w_ff1, b_ff1, w_ff2, b_ff2,
           ln2_g, ln2_b):
    pool_width = pool_w.shape[0] * pool_w.shape[1]
    lru_width = conv_w.shape[1]
    o1, o2, o3 = pool_width, pool_width + lru_width, pool_width + 2 * lru_width

    z_a = _mm_act(x_rows, w_in, layer, 0, o2, "none")
    gate_act = _mm_act(x_rows, w_in, layer, o2, o3 - o2, "gelu")
    g_act = _mm_act(x_rows, w_in, layer, o3, w_in.shape[2] - o3, "sigmoid")

    y_pool = _pool_branch(z_a, pool_w.astype(BF16), pool_scale, batch, seq)

    w_gates = jnp.concatenate([lru_wa[0], lru_wx[0], lru_wa[1], lru_wx[1]], axis=-1).astype(BF16)
    params = jnp.concatenate(
        [conv_w, conv_b[None], lru_ba[0][None], lru_bx[0][None], lru_lambda[0][None],
         lru_ba[1][None], lru_bx[1][None], lru_lambda[1][None]], axis=0).astype(F32)
    params = jnp.pad(params, ((0, _P_ROWS - params.shape[0]), (0, 0)))
    y_lru = _lru_branch(z_a, gate_act, w_gates, params, batch, seq, o1)

    t = batch * seq
    x1_f, x1_b = _merge(y_pool.reshape(t, -1), y_lru.reshape(t, -1), g_act, x_rows,
                        w_pool_up.astype(BF16), w_lru_up.astype(BF16), w_out.astype(BF16),
                        b_out, ln1_g, ln1_b, alpha)
    return _mlp(x1_b, x1_f, w_ff1, layer, b_ff1, w_ff2, b_ff2, ln2_g, ln2_b, alpha)


def kernel(x, w_in, pool_w, pool_scale, conv_w, conv_b, lru_wa, lru_ba, lru_wx, lru_bx, lru_lambda, w_pool_up, w_lru_up, w_out, b_out, ln1_g, ln1_b, w_ff1, b_ff1, w_ff2, b_ff2, ln2_g, ln2_b):
    batch, seq, d = x.shape
    depth = w_in.shape[0]
    alpha = (2.0 * depth) ** 0.25
    rows = x.reshape(batch * seq, d)
    for l in range(depth):
        rows = _layer(rows, batch, seq, alpha, l, w_in, pool_w[l], pool_scale[l], conv_w[l], conv_b[l],
                      lru_wa[l], lru_ba[l], lru_wx[l], lru_bx[l], lru_lambda[l], w_pool_up[l], w_lru_up[l],
                      w_out[l], b_out[l], ln1_g[l], ln1_b[l], w_ff1, b_ff1[l], w_ff2, b_ff2[l],
                      ln2_g[l], ln2_b[l])
    return rows.reshape(batch, seq, d)
```

```python
import functools

import jax
import jax.numpy as jnp
from jax import lax
from jax.experimental import pallas as pl
from jax.experimental.pallas import tpu as pltpu

SUBLANES = 8
LANES = 128
NSEG = SUBLANES

POOL_WINDOWS = (2, 4, 8, 16)
LRU_BLOCK = 256
CONV_WIDTH = 4
CONV_LEFT = CONV_WIDTH // 2
LRU_C = 8.0
LN_EPS = 1e-5
HALO = 8
SCAN_GROUPS = 8
PERM_ROWS = 128
X_SLOTS = 4
LOG2_E = 1.4426950408889634
LN_2 = 0.6931471805599453
TINY = 1.1754944e-38

VMEM_LIMIT = 60 * 1024 * 1024

F32 = jnp.float32
BF16 = jnp.bfloat16


def _sigmoid(v):
    return 0.5 * jnp.tanh(0.5 * v) + 0.5


def _gelu_tanh(v):
    c = 0.7978845608028654
    return 0.5 * v * (1.0 + jnp.tanh(c * (v + 0.044715 * (v * v * v))))


def _layer_norm(y, g, b):
    mu = jnp.mean(y, axis=-1, keepdims=True)
    yc = y - mu
    var = jnp.mean(yc * yc, axis=-1, keepdims=True)
    return yc * lax.rsqrt(var + LN_EPS) * g + b


def _mm_act_kernel(x_ref, w_ref, o_ref, w_bf, *, act):
    @pl.when(pl.program_id(1) == 0)
    def _():
        w_bf[...] = w_ref[0].astype(BF16)

    acc = jnp.dot(x_ref[...].astype(BF16), w_bf[...], preferred_element_type=F32)
    if act == "gelu":
        acc = _gelu_tanh(acc)
    elif act == "sigmoid":
        acc = _sigmoid(acc)
    o_ref[...] = acc.astype(o_ref.dtype)


def _mm_act(x, w, layer, col0, n, act, tm=1024, tn=1024):
    m, k = x.shape
    j0 = col0 // tn
    return pl.pallas_call(
        functools.partial(_mm_act_kernel, act=act),
        grid=(n // tn, m // tm),
        in_specs=[pl.BlockSpec((tm, k), lambda j, i: (i, 0)),
                  pl.BlockSpec((1, k, tn), lambda j, i: (layer, 0, j0 + j))],
        out_specs=pl.BlockSpec((tm, tn), lambda j, i: (i, j)),
        out_shape=jax.ShapeDtypeStruct((m, n), F32),
        scratch_shapes=[pltpu.VMEM((k, tn), BF16)],
        compiler_params=pltpu.CompilerParams(
            dimension_semantics=("arbitrary", "arbitrary"),
            vmem_limit_bytes=VMEM_LIMIT),
        name=f"mm_in_{act}",
    )(x, w)


def _load_slabs(buf, r, n):
    return jnp.concatenate([buf[q, pl.ds(r, n), :] for q in range(buf.shape[0])], axis=1)


def _store_slabs(buf, r, n, val):
    for q in range(buf.shape[0]):
        buf[q, pl.ds(r, n), :] = val[:, q * LANES:(q + 1) * LANES]


def _fill_ext(ext, u_ref, seq):
    seg_len = seq // NSEG
    h_rows = HALO * SUBLANES
    nq = ext.shape[0]

    def scatter(bi, carry):
        j0 = pl.multiple_of(bi * PERM_ROWS, PERM_ROWS)
        for s in range(NSEG):
            val = u_ref[0, pl.ds(s * seg_len + j0, PERM_ROWS), :]
            for q in range(nq):
                ext[q, pl.ds(h_rows + j0 * NSEG + s, PERM_ROWS, stride=NSEG), :] = val[:, q * LANES:(q + 1) * LANES]
        return carry

    lax.fori_loop(0, seg_len // PERM_ROWS, scatter, 0)

    zero = jnp.zeros((HALO, LANES), F32)
    for s in range(NSEG):
        nxt = u_ref[0, pl.ds((s + 1) * seg_len, HALO), :] if s + 1 < NSEG else None
        prv = u_ref[0, pl.ds(s * seg_len - HALO, HALO), :] if s > 0 else None
        for q in range(nq):
            lanes = slice(q * LANES, (q + 1) * LANES)
            ext[q, pl.ds(h_rows + seq + s, HALO, stride=NSEG), :] = zero if nxt is None else nxt[:, lanes]
            ext[q, pl.ds(s, HALO, stride=NSEG), :] = zero if prv is None else prv[:, lanes]


def _emit_time_order(src, o_ref, seq, post):
    seg_len = seq // NSEG

    def gather(bi, carry):
        j0 = pl.multiple_of(bi * PERM_ROWS, PERM_ROWS)
        for s in range(NSEG):
            val = jnp.concatenate(
                [src[q, pl.ds(j0 * NSEG + s, PERM_ROWS, stride=NSEG), :] for q in range(src.shape[0])], axis=1)
            r_time = s * seg_len + j0
            o_ref[0, pl.ds(r_time, PERM_ROWS), :] = post(val, r_time).astype(o_ref.dtype)
        return carry

    lax.fori_loop(0, seg_len // PERM_ROWS, gather, 0)


def _pool_kernel(u_ref, w_ref, s_ref, o_ref, ext, y_buf, *, seq, chunk):
    seg_len = seq // NSEG
    h_rows = HALO * SUBLANES
    nq = ext.shape[0]
    _fill_ext(ext, u_ref, seq)
    grp = pl.program_id(1)
    n_chunks = seq // chunk

    def body(win):
        half = win // 2

        def do_chunk(ci, carry):
            r0 = pl.multiple_of(ci * chunk, chunk)
            base = r0 + h_rows
            row = r0 + lax.broadcasted_iota(jnp.int32, (chunk, LANES), 0)
            t = (row & (SUBLANES - 1)) * seg_len + (row >> 3)
            cnt = jnp.minimum(t + half, seq) - jnp.maximum(t - half, 0)
            inv = 1.0 / cnt.astype(F32)
            d = []
            for q in range(nq):
                tot = ext[q, pl.ds(base - half * SUBLANES, chunk), :]
                for m in range(-half + 1, half):
                    tot = tot + ext[q, pl.ds(base + m * SUBLANES, chunk), :]
                d.append(tot * inv - ext[q, pl.ds(base, chunk), :])
            d = jnp.concatenate(d, axis=1)
            y = jnp.dot(d.astype(BF16), w_ref[0], preferred_element_type=F32) * s_ref[0]
            _store_slabs(y_buf, r0, chunk, y)
            return carry

        lax.fori_loop(0, n_chunks, do_chunk, 0)

    for gi, win in enumerate(POOL_WINDOWS):
        pl.when(grp == gi)(functools.partial(body, win))

    _emit_time_order(y_buf, o_ref, seq, lambda val, r_time: val)


def _pool_branch(z_a, pool_w, pool_scale, batch, seq, chunk=512):
    n_groups, gw = pool_w.shape[0], pool_w.shape[1]
    z3 = z_a.reshape(batch, seq, z_a.shape[1])
    nq = gw // LANES
    return pl.pallas_call(
        functools.partial(_pool_kernel, seq=seq, chunk=chunk),
        grid=(batch, n_groups),
        in_specs=[pl.BlockSpec((1, seq, gw), lambda b, g: (b, 0, g)),
                  pl.BlockSpec((1, gw, gw), lambda b, g: (g, 0, 0)),
                  pl.BlockSpec((1, 1, gw), lambda b, g: (g, 0, 0))],
        out_specs=pl.BlockSpec((1, seq, gw), lambda b, g: (b, 0, g)),
        out_shape=jax.ShapeDtypeStruct((batch, seq, n_groups * gw), BF16),
        scratch_shapes=[pltpu.VMEM((nq, seq + 2 * HALO * SUBLANES, LANES), F32),
                        pltpu.VMEM((nq, seq, LANES), F32)],
        compiler_params=pltpu.CompilerParams(
            dimension_semantics=("arbitrary", "arbitrary"),
            vmem_limit_bytes=VMEM_LIMIT),
        name="pool_branch",
    )(z3, pool_w, pool_scale.reshape(n_groups, 1, gw))


_P_CONV_W = 0
_P_CONV_B = CONV_WIDTH
_P_DIR = CONV_WIDTH + 1
_P_ROWS = 16


def _lru_kernel(x_hbm, u_ref, wg_ref, w_ref, p_ref, o_ref,
                xbuf, sem, wg_bf, ext, gate_buf, a_f, x_f, a_b, x_b, hf_buf, *, seq, chunk):
    h_rows = HALO * SUBLANES
    c = LRU_BLOCK
    slab = SCAN_GROUPS * SUBLANES
    n_slabs = seq // slab
    n_chunks = seq // chunk
    bi = pl.program_id(0)

    def x_copy(p, slot):
        return pltpu.make_async_copy(x_hbm.at[bi, pl.ds(p * chunk, chunk), :], xbuf.at[slot], sem.at[slot])

    for p in range(X_SLOTS):
        x_copy(p, p).start()
    _fill_ext(ext, u_ref, seq)
    wg_bf[...] = wg_ref[0].astype(BF16)

    def gate_proj(p, slot):
        logits = jnp.dot(xbuf[slot], wg_bf[...], preferred_element_type=F32)
        gate_buf[pl.ds(pl.multiple_of(p * chunk, chunk), chunk), :] = _gelu_tanh(logits)

    conv_hw = [0.5 * p_ref[pl.ds(_P_CONV_W + k, 1), :] for k in range(CONV_WIDTH)]
    conv_hb = 0.5 * p_ref[pl.ds(_P_CONV_B, 1), :]
    hb_a, hb_x, c2 = [], [], []
    for d in range(2):
        hb_a.append(0.5 * p_ref[pl.ds(_P_DIR + 3 * d, 1), :])
        hb_x.append(0.5 * p_ref[pl.ds(_P_DIR + 3 * d + 1, 1), :])
        lam = p_ref[pl.ds(_P_DIR + 3 * d + 2, 1), :]
        c2.append((-0.5 * LRU_C * LOG2_E) * jax.nn.softplus(-lam))
    a_out = (a_f, a_b)
    x_out = (x_f, x_b)

    def gates(ci):
        r0 = pl.multiple_of(ci * chunk, chunk)
        base = r0 + h_rows
        xh = conv_hb + _load_slabs(ext, base - CONV_LEFT * SUBLANES, chunk) * conv_hw[0]
        for k in range(1, CONV_WIDTH):
            xh = xh + _load_slabs(ext, base + (k - CONV_LEFT) * SUBLANES, chunk) * conv_hw[k]
        pre = jnp.dot(xh.astype(BF16), w_ref[0], preferred_element_type=F32)
        for d in range(2):
            t_r = jnp.tanh(pre[:, (2 * d) * c:(2 * d + 1) * c] + hb_a[d])
            t_i = jnp.tanh(pre[:, (2 * d + 1) * c:(2 * d + 2) * c] + hb_x[d])
            log2_a = c2[d] * t_r + c2[d]
            a = jnp.exp2(log2_a)
            one_m_a2 = (-1.0 - a * a) * jnp.tanh(LN_2 * log2_a)
            root = one_m_a2 * lax.rsqrt(jnp.maximum(one_m_a2, TINY))
            _store_slabs(a_out[d], r0, chunk, a)
            _store_slabs(x_out[d], r0, chunk, root * (t_i * xh + xh))

    def step(ci, carry):
        slot = ci % X_SLOTS
        x_copy(ci, slot).wait()
        gates(ci)
        gate_proj(ci, slot)

        @pl.when(ci + X_SLOTS < n_chunks)
        def _():
            x_copy(ci + X_SLOTS, slot).start()

        return carry

    lax.fori_loop(0, n_chunks, step, 0)

    def rows(v, k):
        return v[k * SUBLANES:(k + 1) * SUBLANES]

    def local_scan(it, carry):
        h_f, p_f, h_b, p_b = carry
        rf = pl.multiple_of(it * slab, slab)
        rb = pl.multiple_of((n_slabs - 1 - it) * slab, slab)
        af, xf = _load_slabs(a_f, rf, slab), _load_slabs(x_f, rf, slab)
        ab, xb = _load_slabs(a_b, rb, slab), _load_slabs(x_b, rb, slab)
        for k in range(SCAN_GROUPS):
            kb = SCAN_GROUPS - 1 - k
            h_f = rows(af, k) * h_f + rows(xf, k)
            p_f = rows(af, k) * p_f
            h_b = rows(ab, kb) * h_b + rows(xb, kb)
            p_b = rows(ab, kb) * p_b
        return h_f, p_f, h_b, p_b

    zeros = jnp.zeros((SUBLANES, c), F32)
    ones = jnp.ones((SUBLANES, c), F32)
    h_f, p_f, h_b, p_b = lax.fori_loop(0, n_slabs, local_scan, (zeros, ones, zeros, ones))

    sub = lax.broadcasted_iota(jnp.int32, (SUBLANES, c), 0)
    c_f = zeros
    c_b = zeros
    for _ in range(NSEG - 1):
        c_f = jnp.where(sub == 0, 0.0, pltpu.roll(h_f + p_f * c_f, 1, 0))
        c_b = jnp.where(sub == SUBLANES - 1, 0.0, pltpu.roll(h_b + p_b * c_b, SUBLANES - 1, 0))

    def fwd(it, h):
        rf = pl.multiple_of(it * slab, slab)
        af, xf = _load_slabs(a_f, rf, slab), _load_slabs(x_f, rf, slab)
        out = []
        for k in range(SCAN_GROUPS):
            h = rows(af, k) * h + rows(xf, k)
            out.append(h)
        _store_slabs(hf_buf, rf, slab, jnp.concatenate(out, axis=0))
        return h

    lax.fori_loop(0, n_slabs, fwd, c_f)

    def bwd(it, h):
        rb = pl.multiple_of((n_slabs - 1 - it) * slab, slab)
        ab, xb = _load_slabs(a_b, rb, slab), _load_slabs(x_b, rb, slab)
        hf = _load_slabs(hf_buf, rb, slab)
        out = [None] * SCAN_GROUPS
        for kb in range(SCAN_GROUPS - 1, -1, -1):
            h = rows(ab, kb) * h + rows(xb, kb)
            out[kb] = h + rows(hf, kb)
        _store_slabs(x_f, rb, slab, jnp.concatenate(out, axis=0))
        return h

    lax.fori_loop(0, n_slabs, bwd, c_b)

    _emit_time_order(x_f, o_ref, seq, lambda val, r_time: val * gate_buf[pl.ds(r_time, PERM_ROWS), :])


def _lru_branch(x_b, z_a, w_in, layer, w_gates, params, lru_col0, gate_col0, chunk=256):
    batch, seq, d = x_b.shape
    n_heads = w_gates.shape[0]
    c = LRU_BLOCK
    nq = c // LANES
    ucol, gcol = lru_col0 // c, gate_col0 // c
    seg_buf = pltpu.VMEM((nq, seq, LANES), F32)
    return pl.pallas_call(
        functools.partial(_lru_kernel, seq=seq, chunk=chunk),
        grid=(batch, n_heads),
        in_specs=[pl.BlockSpec(memory_space=pl.ANY),
                  pl.BlockSpec((1, seq, c), lambda b, h: (b, 0, ucol + h)),
                  pl.BlockSpec((1, d, c), lambda b, h: (layer, 0, gcol + h)),
                  pl.BlockSpec((1, c, 4 * c), lambda b, h: (h, 0, 0)),
                  pl.BlockSpec((_P_ROWS, c), lambda b, h: (0, h))],
        out_specs=pl.BlockSpec((1, seq, c), lambda b, h: (b, 0, h)),
        out_shape=jax.ShapeDtypeStruct((batch, seq, n_heads * c), BF16),
        scratch_shapes=[pltpu.VMEM((X_SLOTS, chunk, d), BF16),
                        pltpu.SemaphoreType.DMA((X_SLOTS,)),
                        pltpu.VMEM((d, c), BF16),
                        pltpu.VMEM((nq, seq + 2 * HALO * SUBLANES, LANES), F32),
                        pltpu.VMEM((seq, c), F32)]
        + [seg_buf] * 5,
        compiler_params=pltpu.CompilerParams(
            dimension_semantics=("arbitrary", "arbitrary"),
            vmem_limit_bytes=VMEM_LIMIT),
        name="lru_branch",
    )(x_b, z_a.reshape(batch, seq, z_a.shape[1]), w_in, w_gates, params)


def _merge_kernel(yp_ref, yl_ref, g0_ref, g1_ref, x_ref, wp_ref, wl_ref, wo_ref, bo_ref, g_ref, b_ref,
                  of_ref, ob_ref, *, alpha):
    up_p = jnp.dot(yp_ref[...], wp_ref[...], preferred_element_type=F32)
    up_l = jnp.dot(yl_ref[...], wl_ref[...], preferred_element_type=F32)
    m = g0_ref[...] * up_p + g1_ref[...] * up_l
    mix = jnp.dot(m.astype(BF16), wo_ref[...], preferred_element_type=F32) + bo_ref[...]
    y = _layer_norm(alpha * x_ref[...] + mix, g_ref[...], b_ref[...])
    of_ref[...] = y
    ob_ref[...] = y.astype(BF16)


def _merge(y_pool, y_lru, g_act, x_rows, w_pool_up, w_lru_up, w_out, b_out, ln_g, ln_b, alpha, tm=256):
    t, d = x_rows.shape
    row = lambda i: (i, 0)
    const = lambda i: (0, 0)
    wspec = pl.BlockSpec((d, d), const, pipeline_mode=pl.Buffered(1))
    vspec = pl.BlockSpec((1, d), const)
    return pl.pallas_call(
        functools.partial(_merge_kernel, alpha=alpha),
        grid=(t // tm,),
        in_specs=[pl.BlockSpec((tm, d), row), pl.BlockSpec((tm, d), row),
                  pl.BlockSpec((tm, d), lambda i: (i, 0)), pl.BlockSpec((tm, d), lambda i: (i, 1)),
                  pl.BlockSpec((tm, d), row), wspec, wspec, wspec, vspec, vspec, vspec],
        out_specs=[pl.BlockSpec((tm, d), row), pl.BlockSpec((tm, d), row)],
        out_shape=[jax.ShapeDtypeStruct((t, d), F32), jax.ShapeDtypeStruct((t, d), BF16)],
        compiler_params=pltpu.CompilerParams(
            dimension_semantics=("arbitrary",), vmem_limit_bytes=VMEM_LIMIT),
        name="merge_out_ln",
    )(y_pool, y_lru, g_act, g_act, x_rows, w_pool_up, w_lru_up, w_out,
      b_out.reshape(1, d), ln_g.reshape(1, d), ln_b.reshape(1, d))


def _mlp_kernel(xb_ref, xf_ref, w1_ref, b1_ref, w2_ref, b2_ref, g_ref, b_ref, o_ref, *, alpha):
    f = pl.program_id(1)

    @pl.when(f == 0)
    def _():
        o_ref[...] = jnp.zeros_like(o_ref)

    h = jnp.dot(xb_ref[...], w1_ref[0].astype(BF16), preferred_element_type=F32) + b1_ref[...]
    h = jnp.square(jnp.maximum(h, 0.0))
    o_ref[...] += jnp.dot(h.astype(BF16), w2_ref[0].astype(BF16), preferred_element_type=F32)

    @pl.when(f == pl.num_programs(1) - 1)
    def _():
        y = alpha * xf_ref[...] + o_ref[...] + b2_ref[...]
        o_ref[...] = _layer_norm(y, g_ref[...], b_ref[...])


def _mlp(x_b, x_f, w1, layer, b1, w2, b2, ln_g, ln_b, alpha, tm=1024, tf=512):
    t, d = x_f.shape
    dff = w1.shape[2]
    return pl.pallas_call(
        functools.partial(_mlp_kernel, alpha=alpha),
        grid=(t // tm, dff // tf),
        in_specs=[pl.BlockSpec((tm, d), lambda i, f: (i, 0)),
                  pl.BlockSpec((tm, d), lambda i, f: (i, 0), pipeline_mode=pl.Buffered(1)),
                  pl.BlockSpec((1, d, tf), lambda i, f: (layer, 0, f)),
                  pl.BlockSpec((1, tf), lambda i, f: (0, f)),
                  pl.BlockSpec((1, tf, d), lambda i, f: (layer, f, 0)),
                  pl.BlockSpec((1, d), lambda i, f: (0, 0)),
                  pl.BlockSpec((1, d), lambda i, f: (0, 0)),
                  pl.BlockSpec((1, d), lambda i, f: (0, 0))],
        out_specs=pl.BlockSpec((tm, d), lambda i, f: (i, 0)),
        out_shape=jax.ShapeDtypeStruct((t, d), F32),
        compiler_params=pltpu.CompilerParams(
            dimension_semantics=("arbitrary", "arbitrary"), vmem_limit_bytes=VMEM_LIMIT),
        name="mlp_ln",
    )(x_b, x_f, w1, b1.reshape(1, dff), w2, b2.reshape(1, d), ln_g.reshape(1, d), ln_b.reshape(1, d))


def _layer(x_rows, batch, seq, alpha, layer, w_in, pool_w, pool_scale, conv_w, conv_b, lru_wa, lru_ba, lru_wx,
           lru_bx, lru_lambda, w_pool_up, w_lru_up, w_out, b_out, ln1_g, ln1_b, w_ff1, b_ff1, w_ff2, b_ff2,
           ln2_g, ln2_b):
    pool_width = pool_w.shape[0] * pool_w.shape[1]
    lru_width = conv_w.shape[1]
    o1, o2, o3 = pool_width, pool_width + lru_width, pool_width + 2 * lru_width

    x_b = x_rows.astype(BF16)
    z_a = _mm_act(x_b, w_in, layer, 0, o2, "none")
    g_act = _mm_act(x_b, w_in, layer, o3, w_in.shape[2] - o3, "sigmoid")

    y_pool = _pool_branch(z_a, pool_w.astype(BF16), pool_scale, batch, seq)

    w_gates = jnp.concatenate([lru_wa[0], lru_wx[0], lru_wa[1], lru_wx[1]], axis=-1).astype(BF16)
    params = jnp.concatenate(
        [conv_w, conv_b[None], lru_ba[0][None], lru_bx[0][None], lru_lambda[0][None],
         lru_ba[1][None], lru_bx[1][None], lru_lambda[1][None]], axis=0).astype(F32)
    params = jnp.pad(params, ((0, _P_ROWS - params.shape[0]), (0, 0)))
    y_lru = _lru_branch(x_b.reshape(batch, seq, -1), z_a, w_in, layer, w_gates, params, o1, o2)

    t = batch * seq
    x1_f, x1_b = _merge(y_pool.reshape(t, -1), y_lru.reshape(t, -1), g_act, x_rows,
                        w_pool_up.astype(BF16), w_lru_up.astype(BF16), w_out.astype(BF16),
                        b_out, ln1_g, ln1_b, alpha)
    return _mlp(x1_b, x1_f, w_ff1, layer, b_ff1, w_ff2, b_ff2, ln2_g, ln2_b, alpha)


def kernel(x, w_in, pool_w, pool_scale, conv_w, conv_b, lru_wa, lru_ba, lru_wx, lru_bx, lru_lambda, w_pool_up, w_lru_up, w_out, b_out, ln1_g, ln1_b, w_ff1, b_ff1, w_ff2, b_ff2, ln2_g, ln2_b):
    batch, seq, d = x.shape
    depth = w_in.shape[0]
    alpha = (2.0 * depth) ** 0.25
    rows = x.reshape(batch * seq, d)
    for l in range(depth):
        rows = _layer(rows, batch, seq, alpha, l, w_in, pool_w[l], pool_scale[l], conv_w[l], conv_b[l],
                      lru_wa[l], lru_ba[l], lru_wx[l], lru_bx[l], lru_lambda[l], w_pool_up[l], w_lru_up[l],
                      w_out[l], b_out[l], ln1_g[l], ln1_b[l], w_ff1, b_ff1[l], w_ff2, b_ff2[l],
                      ln2_g[l], ln2_b[l])
    return rows.reshape(batch, seq, d)
```

```python
import functools

import jax
import jax.numpy as jnp
from jax import lax
from jax.experimental import pallas as pl
from jax.experimental.pallas import tpu as pltpu

SUBLANES = 8
LANES = 128
NSEG = SUBLANES

POOL_WINDOWS = (2, 4, 8, 16)
LRU_BLOCK = 256
CONV_WIDTH = 4
CONV_LEFT = CONV_WIDTH // 2
LRU_C = 8.0
LN_EPS = 1e-5
HALO = 8
SCAN_GROUPS = 8
PERM_ROWS = 128
X_SLOTS = 4
LOG2_E = 1.4426950408889634
LN_2 = 0.6931471805599453
TINY = 1.1754944e-38

VMEM_LIMIT = 60 * 1024 * 1024

F32 = jnp.float32
BF16 = jnp.bfloat16


def _sigmoid(v):
    return 0.5 * jnp.tanh(0.5 * v) + 0.5


def _gelu_tanh(v):
    c = 0.7978845608028654
    return 0.5 * v * (1.0 + jnp.tanh(c * (v + 0.044715 * (v * v * v))))


def _layer_norm(y, g, b):
    mu = jnp.mean(y, axis=-1, keepdims=True)
    yc = y - mu
    var = jnp.mean(yc * yc, axis=-1, keepdims=True)
    return yc * lax.rsqrt(var + LN_EPS) * g + b


def _mm_act_kernel(x_ref, w_ref, o_ref, w_bf, *, act):
    @pl.when(pl.program_id(1) == 0)
    def _():
        w_bf[...] = w_ref[0].astype(BF16)

    acc = jnp.dot(x_ref[...].astype(BF16), w_bf[...], preferred_element_type=F32)
    if act == "gelu":
        acc = _gelu_tanh(acc)
    elif act == "sigmoid":
        acc = _sigmoid(acc)
    o_ref[...] = acc.astype(o_ref.dtype)


def _mm_act(x, w, layer, col0, n, act, tm=1024, tn=1024):
    m, k = x.shape
    j0 = col0 // tn
    return pl.pallas_call(
        functools.partial(_mm_act_kernel, act=act),
        grid=(n // tn, m // tm),
        in_specs=[pl.BlockSpec((tm, k), lambda j, i: (i, 0)),
                  pl.BlockSpec((1, k, tn), lambda j, i: (layer, 0, j0 + j))],
        out_specs=pl.BlockSpec((tm, tn), lambda j, i: (i, j)),
        out_shape=jax.ShapeDtypeStruct((m, n), F32),
        scratch_shapes=[pltpu.VMEM((k, tn), BF16)],
        compiler_params=pltpu.CompilerParams(
            dimension_semantics=("arbitrary", "arbitrary"),
            vmem_limit_bytes=VMEM_LIMIT),
        name=f"mm_in_{act}",
    )(x, w)


def _load_slabs(buf, r, n):
    return jnp.concatenate([buf[q, pl.ds(r, n), :] for q in range(buf.shape[0])], axis=1)


def _store_slabs(buf, r, n, val):
    for q in range(buf.shape[0]):
        buf[q, pl.ds(r, n), :] = val[:, q * LANES:(q + 1) * LANES]


def _fill_ext(ext, u_ref, seq):
    seg_len = seq // NSEG
    h_rows = HALO * SUBLANES
    nq = ext.shape[0]

    def scatter(bi, carry):
        j0 = pl.multiple_of(bi * PERM_ROWS, PERM_ROWS)
        for s in range(NSEG):
            val = u_ref[0, pl.ds(s * seg_len + j0, PERM_ROWS), :]
            for q in range(nq):
                ext[q, pl.ds(h_rows + j0 * NSEG + s, PERM_ROWS, stride=NSEG), :] = val[:, q * LANES:(q + 1) * LANES]
        return carry

    lax.fori_loop(0, seg_len // PERM_ROWS, scatter, 0)

    zero = jnp.zeros((HALO, LANES), F32)
    for s in range(NSEG):
        nxt = u_ref[0, pl.ds((s + 1) * seg_len, HALO), :] if s + 1 < NSEG else None
        prv = u_ref[0, pl.ds(s * seg_len - HALO, HALO), :] if s > 0 else None
        for q in range(nq):
            lanes = slice(q * LANES, (q + 1) * LANES)
            ext[q, pl.ds(h_rows + seq + s, HALO, stride=NSEG), :] = zero if nxt is None else nxt[:, lanes]
            ext[q, pl.ds(s, HALO, stride=NSEG), :] = zero if prv is None else prv[:, lanes]


def _emit_time_order(src, o_ref, seq, post):
    seg_len = seq // NSEG

    def gather(bi, carry):
        j0 = pl.multiple_of(bi * PERM_ROWS, PERM_ROWS)
        for s in range(NSEG):
            val = jnp.concatenate(
                [src[q, pl.ds(j0 * NSEG + s, PERM_ROWS, stride=NSEG), :] for q in range(src.shape[0])], axis=1)
            r_time = s * seg_len + j0
            o_ref[0, pl.ds(r_time, PERM_ROWS), :] = post(val, r_time).astype(o_ref.dtype)
        return carry

    lax.fori_loop(0, seg_len // PERM_ROWS, gather, 0)


def _pool_kernel(u_ref, w_ref, s_ref, o_ref, ext, y_buf, *, seq, chunk):
    seg_len = seq // NSEG
    h_rows = HALO * SUBLANES
    nq = ext.shape[0]
    _fill_ext(ext, u_ref, seq)
    grp = pl.program_id(1)
    n_chunks = seq // chunk

    def body(win):
        half = win // 2

        def do_chunk(ci, carry):
            r0 = pl.multiple_of(ci * chunk, chunk)
            base = r0 + h_rows
            row = r0 + lax.broadcasted_iota(jnp.int32, (chunk, LANES), 0)
            t = (row & (SUBLANES - 1)) * seg_len + (row >> 3)
            cnt = jnp.minimum(t + half, seq) - jnp.maximum(t - half, 0)
            inv = 1.0 / cnt.astype(F32)
            d = []
            for q in range(nq):
                tot = ext[q, pl.ds(base - half * SUBLANES, chunk), :]
                for m in range(-half + 1, half):
                    tot = tot + ext[q, pl.ds(base + m * SUBLANES, chunk), :]
                d.append(tot * inv - ext[q, pl.ds(base, chunk), :])
            d = jnp.concatenate(d, axis=1)
            y = jnp.dot(d.astype(BF16), w_ref[0], preferred_element_type=F32) * s_ref[0]
            _store_slabs(y_buf, r0, chunk, y)
            return carry

        lax.fori_loop(0, n_chunks, do_chunk, 0)

    for gi, win in enumerate(POOL_WINDOWS):
        pl.when(grp == gi)(functools.partial(body, win))

    _emit_time_order(y_buf, o_ref, seq, lambda val, r_time: val)


def _pool_branch(z_a, pool_w, pool_scale, batch, seq, chunk=512):
    n_groups, gw = pool_w.shape[0], pool_w.shape[1]
    z3 = z_a.reshape(batch, seq, z_a.shape[1])
    nq = gw // LANES
    return pl.pallas_call(
        functools.partial(_pool_kernel, seq=seq, chunk=chunk),
        grid=(batch, n_groups),
        in_specs=[pl.BlockSpec((1, seq, gw), lambda b, g: (b, 0, g)),
                  pl.BlockSpec((1, gw, gw), lambda b, g: (g, 0, 0)),
                  pl.BlockSpec((1, 1, gw), lambda b, g: (g, 0, 0))],
        out_specs=pl.BlockSpec((1, seq, gw), lambda b, g: (b, 0, g)),
        out_shape=jax.ShapeDtypeStruct((batch, seq, n_groups * gw), BF16),
        scratch_shapes=[pltpu.VMEM((nq, seq + 2 * HALO * SUBLANES, LANES), F32),
                        pltpu.VMEM((nq, seq, LANES), F32)],
        compiler_params=pltpu.CompilerParams(
            dimension_semantics=("arbitrary", "arbitrary"),
            vmem_limit_bytes=VMEM_LIMIT),
        name="pool_branch",
    )(z3, pool_w, pool_scale.reshape(n_groups, 1, gw))


_P_CONV_W = 0
_P_CONV_B = CONV_WIDTH
_P_DIR = CONV_WIDTH + 1
_P_ROWS = 16


def _lru_kernel(x_hbm, u_ref, wg_ref, w_ref, p_ref, o_ref,
                xbuf, sem, wg_bf, ext, gate_buf, a_f, x_f, a_b, x_b, hf_buf, *, seq, chunk):
    h_rows = HALO * SUBLANES
    c = LRU_BLOCK
    slab = SCAN_GROUPS * SUBLANES
    n_slabs = seq // slab
    n_chunks = seq // chunk
    bi = pl.program_id(0)

    def x_copy(p, slot):
        row0 = pl.multiple_of(bi * seq + p * chunk, chunk)
        return pltpu.make_async_copy(x_hbm.at[pl.ds(row0, chunk), :], xbuf.at[slot], sem.at[slot])

    for p in range(X_SLOTS):
        x_copy(p, p).start(priority=1)
    _fill_ext(ext, u_ref, seq)
    wg_bf[...] = wg_ref[0].astype(BF16)

    def gate_proj(p, slot):
        logits = jnp.dot(xbuf[slot], wg_bf[...], preferred_element_type=F32)
        gate_buf[pl.ds(pl.multiple_of(p * chunk, chunk), chunk), :] = _gelu_tanh(logits)

    conv_hw = [0.5 * p_ref[pl.ds(_P_CONV_W + k, 1), :] for k in range(CONV_WIDTH)]
    conv_hb = 0.5 * p_ref[pl.ds(_P_CONV_B, 1), :]
    hb_a, hb_x, c2 = [], [], []
    for d in range(2):
        hb_a.append(0.5 * p_ref[pl.ds(_P_DIR + 3 * d, 1), :])
        hb_x.append(0.5 * p_ref[pl.ds(_P_DIR + 3 * d + 1, 1), :])
        lam = p_ref[pl.ds(_P_DIR + 3 * d + 2, 1), :]
        c2.append((-0.5 * LRU_C * LOG2_E) * jax.nn.softplus(-lam))
    a_out = (a_f, a_b)
    x_out = (x_f, x_b)

    def gates(ci):
        r0 = pl.multiple_of(ci * chunk, chunk)
        base = r0 + h_rows
        xh = conv_hb + _load_slabs(ext, base - CONV_LEFT * SUBLANES, chunk) * conv_hw[0]
        for k in range(1, CONV_WIDTH):
            xh = xh + _load_slabs(ext, base + (k - CONV_LEFT) * SUBLANES, chunk) * conv_hw[k]
        pre = jnp.dot(xh.astype(BF16), w_ref[0], preferred_element_type=F32)
        for d in range(2):
            t_r = jnp.tanh(pre[:, (2 * d) * c:(2 * d + 1) * c] + hb_a[d])
            t_i = jnp.tanh(pre[:, (2 * d + 1) * c:(2 * d + 2) * c] + hb_x[d])
            log2_a = c2[d] * t_r + c2[d]
            a = jnp.exp2(log2_a)
            one_m_a2 = (-1.0 - a * a) * jnp.tanh(LN_2 * log2_a)
            root = one_m_a2 * lax.rsqrt(jnp.maximum(one_m_a2, TINY))
            _store_slabs(a_out[d], r0, chunk, a)
            _store_slabs(x_out[d], r0, chunk, root * (t_i * xh + xh))

    def step(ci, carry):
        slot = ci % X_SLOTS
        x_copy(ci, slot).wait()
        gates(ci)
        gate_proj(ci, slot)

        @pl.when(ci + X_SLOTS < n_chunks)
        def _():
            x_copy(ci + X_SLOTS, slot).start(priority=1)

        return carry

    lax.fori_loop(0, n_chunks, step, 0)

    def rows(v, k):
        return v[k * SUBLANES:(k + 1) * SUBLANES]

    def local_scan(it, carry):
        h_f, p_f, h_b, p_b = carry
        rf = pl.multiple_of(it * slab, slab)
        rb = pl.multiple_of((n_slabs - 1 - it) * slab, slab)
        af, xf = _load_slabs(a_f, rf, slab), _load_slabs(x_f, rf, slab)
        ab, xb = _load_slabs(a_b, rb, slab), _load_slabs(x_b, rb, slab)
        for k in range(SCAN_GROUPS):
            kb = SCAN_GROUPS - 1 - k
            h_f = rows(af, k) * h_f + rows(xf, k)
            p_f = rows(af, k) * p_f
            h_b = rows(ab, kb) * h_b + rows(xb, kb)
            p_b = rows(ab, kb) * p_b
        return h_f, p_f, h_b, p_b

    zeros = jnp.zeros((SUBLANES, c), F32)
    ones = jnp.ones((SUBLANES, c), F32)
    h_f, p_f, h_b, p_b = lax.fori_loop(0, n_slabs, local_scan, (zeros, ones, zeros, ones))

    sub = lax.broadcasted_iota(jnp.int32, (SUBLANES, c), 0)
    c_f = zeros
    c_b = zeros
    for _ in range(NSEG - 1):
        c_f = jnp.where(sub == 0, 0.0, pltpu.roll(h_f + p_f * c_f, 1, 0))
        c_b = jnp.where(sub == SUBLANES - 1, 0.0, pltpu.roll(h_b + p_b * c_b, SUBLANES - 1, 0))

    def fwd(it, h):
        rf = pl.multiple_of(it * slab, slab)
        af, xf = _load_slabs(a_f, rf, slab), _load_slabs(x_f, rf, slab)
        out = []
        for k in range(SCAN_GROUPS):
            h = rows(af, k) * h + rows(xf, k)
            out.append(h)
        _store_slabs(hf_buf, rf, slab, jnp.concatenate(out, axis=0))
        return h

    lax.fori_loop(0, n_slabs, fwd, c_f)

    def bwd(it, h):
        rb = pl.multiple_of((n_slabs - 1 - it) * slab, slab)
        ab, xb = _load_slabs(a_b, rb, slab), _load_slabs(x_b, rb, slab)
        hf = _load_slabs(hf_buf, rb, slab)
        out = [None] * SCAN_GROUPS
        for kb in range(SCAN_GROUPS - 1, -1, -1):
            h = rows(ab, kb) * h + rows(xb, kb)
            out[kb] = h + rows(hf, kb)
        _store_slabs(x_f, rb, slab, jnp.concatenate(out, axis=0))
        return h

    lax.fori_loop(0, n_slabs, bwd, c_b)

    _emit_time_order(x_f, o_ref, seq, lambda val, r_time: val * gate_buf[pl.ds(r_time, PERM_ROWS), :])


def _lru_branch(x_b, z_a, w_in, layer, w_gates, params, batch, seq, lru_col0, gate_col0, chunk=256):
    d = x_b.shape[1]
    n_heads = w_gates.shape[0]
    c = LRU_BLOCK
    nq = c // LANES
    ucol, gcol = lru_col0 // c, gate_col0 // c
    seg_buf = pltpu.VMEM((nq, seq, LANES), F32)
    return pl.pallas_call(
        functools.partial(_lru_kernel, seq=seq, chunk=chunk),
        grid=(batch, n_heads),
        in_specs=[pl.BlockSpec(memory_space=pl.ANY),
                  pl.BlockSpec((1, seq, c), lambda b, h: (b, 0, ucol + h)),
                  pl.BlockSpec((1, d, c), lambda b, h: (layer, 0, gcol + h)),
                  pl.BlockSpec((1, c, 4 * c), lambda b, h: (h, 0, 0)),
                  pl.BlockSpec((_P_ROWS, c), lambda b, h: (0, h))],
        out_specs=pl.BlockSpec((1, seq, c), lambda b, h: (b, 0, h)),
        out_shape=jax.ShapeDtypeStruct((batch, seq, n_heads * c), BF16),
        scratch_shapes=[pltpu.VMEM((X_SLOTS, chunk, d), BF16),
                        pltpu.SemaphoreType.DMA((X_SLOTS,)),
                        pltpu.VMEM((d, c), BF16),
                        pltpu.VMEM((nq, seq + 2 * HALO * SUBLANES, LANES), F32),
                        pltpu.VMEM((seq, c), F32)]
        + [seg_buf] * 5,
        compiler_params=pltpu.CompilerParams(
            dimension_semantics=("arbitrary", "arbitrary"),
            vmem_limit_bytes=VMEM_LIMIT),
        name="lru_branch",
    )(x_b, z_a.reshape(batch, seq, z_a.shape[1]), w_in, w_gates, params)


def _merge_kernel(yp_ref, yl_ref, g0_ref, g1_ref, x_ref, wp_ref, wl_ref, wo_ref, bo_ref, g_ref, b_ref,
                  of_ref, ob_ref, *, alpha):
    up_p = jnp.dot(yp_ref[...], wp_ref[...], preferred_element_type=F32)
    up_l = jnp.dot(yl_ref[...], wl_ref[...], preferred_element_type=F32)
    m = g0_ref[...] * up_p + g1_ref[...] * up_l
    mix = jnp.dot(m.astype(BF16), wo_ref[...], preferred_element_type=F32) + bo_ref[...]
    y = _layer_norm(alpha * x_ref[...] + mix, g_ref[...], b_ref[...])
    of_ref[...] = y
    ob_ref[...] = y.astype(BF16)


def _merge(y_pool, y_lru, g_act, x_rows, w_pool_up, w_lru_up, w_out, b_out, ln_g, ln_b, alpha, tm=256):
    t, d = x_rows.shape
    row = lambda i: (i, 0)
    const = lambda i: (0, 0)
    wspec = pl.BlockSpec((d, d), const, pipeline_mode=pl.Buffered(1))
    vspec = pl.BlockSpec((1, d), const)
    return pl.pallas_call(
        functools.partial(_merge_kernel, alpha=alpha),
        grid=(t // tm,),
        in_specs=[pl.BlockSpec((tm, d), row), pl.BlockSpec((tm, d), row),
                  pl.BlockSpec((tm, d), lambda i: (i, 0)), pl.BlockSpec((tm, d), lambda i: (i, 1)),
                  pl.BlockSpec((tm, d), row), wspec, wspec, wspec, vspec, vspec, vspec],
        out_specs=[pl.BlockSpec((tm, d), row), pl.BlockSpec((tm, d), row)],
        out_shape=[jax.ShapeDtypeStruct((t, d), F32), jax.ShapeDtypeStruct((t, d), BF16)],
        compiler_params=pltpu.CompilerParams(
            dimension_semantics=("arbitrary",), vmem_limit_bytes=VMEM_LIMIT),
        name="merge_out_ln",
    )(y_pool, y_lru, g_act, g_act, x_rows, w_pool_up, w_lru_up, w_out,
      b_out.reshape(1, d), ln_g.reshape(1, d), ln_b.reshape(1, d))


def _mlp_kernel(xb_ref, xf_ref, w1_ref, b1_ref, w2_ref, b2_ref, g_ref, b_ref, o_ref, *, alpha):
    f = pl.program_id(1)

    @pl.when(f == 0)
    def _():
        o_ref[...] = jnp.zeros_like(o_ref)

    h = jnp.dot(xb_ref[...], w1_ref[0].astype(BF16), preferred_element_type=F32) + b1_ref[...]
    h = jnp.square(jnp.maximum(h, 0.0))
    o_ref[...] += jnp.dot(h.astype(BF16), w2_ref[0].astype(BF16), preferred_element_type=F32)

    @pl.when(f == pl.num_programs(1) - 1)
    def _():
        y = alpha * xf_ref[...] + o_ref[...] + b2_ref[...]
        o_ref[...] = _layer_norm(y, g_ref[...], b_ref[...])


def _mlp(x_b, x_f, w1, layer, b1, w2, b2, ln_g, ln_b, alpha, tm=1024, tf=512):
    t, d = x_f.shape
    dff = w1.shape[2]
    return pl.pallas_call(
        functools.partial(_mlp_kernel, alpha=alpha),
        grid=(t // tm, dff // tf),
        in_specs=[pl.BlockSpec((tm, d), lambda i, f: (i, 0)),
                  pl.BlockSpec((tm, d), lambda i, f: (i, 0), pipeline_mode=pl.Buffered(1)),
                  pl.BlockSpec((1, d, tf), lambda i, f: (layer, 0, f)),
                  pl.BlockSpec((1, tf), lambda i, f: (0, f)),
                  pl.BlockSpec((1, tf, d), lambda i, f: (layer, f, 0)),
                  pl.BlockSpec((1, d), lambda i, f: (0, 0)),
                  pl.BlockSpec((1, d), lambda i, f: (0, 0)),
                  pl.BlockSpec((1, d), lambda i, f: (0, 0))],
        out_specs=pl.BlockSpec((tm, d), lambda i, f: (i, 0)),
        out_shape=jax.ShapeDtypeStruct((t, d), F32),
        compiler_params=pltpu.CompilerParams(
            dimension_semantics=("arbitrary", "arbitrary"), vmem_limit_bytes=VMEM_LIMIT),
        name="mlp_ln",
    )(x_b, x_f, w1, b1.reshape(1, dff), w2, b2.reshape(1, d), ln_g.reshape(1, d), ln_b.reshape(1, d))


def _layer(x_rows, batch, seq, alpha, layer, w_in, pool_w, pool_scale, conv_w, conv_b, lru_wa, lru_ba, lru_wx,
           lru_bx, lru_lambda, w_pool_up, w_lru_up, w_out, b_out, ln1_g, ln1_b, w_ff1, b_ff1, w_ff2, b_ff2,
           ln2_g, ln2_b):
    pool_width = pool_w.shape[0] * pool_w.shape[1]
    lru_width = conv_w.shape[1]
    o1, o2, o3 = pool_width, pool_width + lru_width, pool_width + 2 * lru_width

    x_b = x_rows.astype(BF16)
    z_a = _mm_act(x_b, w_in, layer, 0, o2, "none")
    g_act = _mm_act(x_b, w_in, layer, o3, w_in.shape[2] - o3, "sigmoid")

    y_pool = _pool_branch(z_a, pool_w.astype(BF16), pool_scale, batch, seq)

    w_gates = jnp.concatenate([lru_wa[0], lru_wx[0], lru_wa[1], lru_wx[1]], axis=-1).astype(BF16)
    params = jnp.concatenate(
        [conv_w, conv_b[None], lru_ba[0][None], lru_bx[0][None], lru_lambda[0][None],
         lru_ba[1][None], lru_bx[1][None], lru_lambda[1][None]], axis=0).astype(F32)
    params = jnp.pad(params, ((0, _P_ROWS - params.shape[0]), (0, 0)))
    y_lru = _lru_branch(x_b, z_a, w_in, layer, w_gates, params, batch, seq, o1, o2)

    t = batch * seq
    x1_f, x1_b = _merge(y_pool.reshape(t, -1), y_lru.reshape(t, -1), g_act, x_rows,
                        w_pool_up.astype(BF16), w_lru_up.astype(BF16), w_out.astype(BF16),
                        b_out, ln1_g, ln1_b, alpha)
    return _mlp(x1_b, x1_f, w_ff1, layer, b_ff1, w_ff2, b_ff2, ln2_g, ln2_b, alpha)


def kernel(x, w_in, pool_w, pool_scale, conv_w, conv_b, lru_wa, lru_ba, lru_wx, lru_bx, lru_lambda, w_pool_up, w_lru_up, w_out, b_out, ln1_g, ln1_b, w_ff1, b_ff1, w_ff2, b_ff2, ln2_g, ln2_b):
    batch, seq, d = x.shape
    depth = w_in.shape[0]
    alpha = (2.0 * depth) ** 0.25
    rows = x.reshape(batch * seq, d)
    for l in range(depth):
        rows = _layer(rows, batch, seq, alpha, l, w_in, pool_w[l], pool_scale[l], conv_w[l], conv_b[l],
                      lru_wa[l], lru_ba[l], lru_wx[l], lru_bx[l], lru_lambda[l], w_pool_up[l], w_lru_up[l],
                      w_out[l], b_out[l], ln1_g[l], ln1_b[l], w_ff1, b_ff1[l], w_ff2, b_ff2[l],
                      ln2_g[l], ln2_b[l])
    return rows.reshape(batch, seq, d)
```

```python
import functools

import jax
import jax.numpy as jnp
from jax import lax
from jax.experimental import pallas as pl
from jax.experimental.pallas import tpu as pltpu

SUBLANES = 8
LANES = 128
NSEG = SUBLANES

POOL_WINDOWS = (2, 4, 8, 16)
LRU_BLOCK = 256
CONV_WIDTH = 4
CONV_LEFT = CONV_WIDTH // 2
LRU_C = 8.0
LN_EPS = 1e-5
HALO = 8
REPLAY_CHUNKS = 2
PERM_ROWS = 128
LOG2_E = 1.4426950408889634
LN_2 = 0.6931471805599453
TINY = 1.1754944e-38

VMEM_LIMIT = 60 * 1024 * 1024

F32 = jnp.float32
BF16 = jnp.bfloat16


def _sigmoid(v):
    return 0.5 * jnp.tanh(0.5 * v) + 0.5


def _gelu_tanh(v):
    c = 0.7978845608028654
    return 0.5 * v * (1.0 + jnp.tanh(c * (v + 0.044715 * (v * v * v))))


def _layer_norm(y, g, b):
    mu = jnp.mean(y, axis=-1, keepdims=True)
    yc = y - mu
    var = jnp.mean(yc * yc, axis=-1, keepdims=True)
    return yc * lax.rsqrt(var + LN_EPS) * g + b


def _mm_act_kernel(x_ref, w_ref, o_ref, w_bf, *, act):
    @pl.when(pl.program_id(1) == 0)
    def _():
        w_bf[...] = w_ref[0].astype(BF16)

    acc = jnp.dot(x_ref[...].astype(BF16), w_bf[...], preferred_element_type=F32)
    if act == "gelu":
        acc = _gelu_tanh(acc)
    elif act == "sigmoid":
        acc = _sigmoid(acc)
    o_ref[...] = acc.astype(o_ref.dtype)


def _mm_act(x, w, layer, col0, n, act, tm=1024, tn=1024):
    m, k = x.shape
    j0 = col0 // tn
    return pl.pallas_call(
        functools.partial(_mm_act_kernel, act=act),
        grid=(n // tn, m // tm),
        in_specs=[pl.BlockSpec((tm, k), lambda j, i: (i, 0)),
                  pl.BlockSpec((1, k, tn), lambda j, i: (layer, 0, j0 + j))],
        out_specs=pl.BlockSpec((tm, tn), lambda j, i: (i, j)),
        out_shape=jax.ShapeDtypeStruct((m, n), F32),
        scratch_shapes=[pltpu.VMEM((k, tn), BF16)],
        compiler_params=pltpu.CompilerParams(
            dimension_semantics=("arbitrary", "arbitrary"),
            vmem_limit_bytes=VMEM_LIMIT),
        name=f"mm_in_{act}",
    )(x, w)


def _load_slabs(buf, r, n):
    return jnp.concatenate([buf[q, pl.ds(r, n), :] for q in range(buf.shape[0])], axis=1)


def _store_slabs(buf, r, n, val):
    for q in range(buf.shape[0]):
        buf[q, pl.ds(r, n), :] = val[:, q * LANES:(q + 1) * LANES]


def _fill_ext(ext, u_ref, seq):
    seg_len = seq // NSEG
    h_rows = HALO * SUBLANES
    nq = ext.shape[0]

    def scatter(bi, carry):
        j0 = pl.multiple_of(bi * PERM_ROWS, PERM_ROWS)
        for s in range(NSEG):
            val = u_ref[0, pl.ds(s * seg_len + j0, PERM_ROWS), :]
            for q in range(nq):
                ext[q, pl.ds(h_rows + j0 * NSEG + s, PERM_ROWS, stride=NSEG), :] = val[:, q * LANES:(q + 1) * LANES]
        return carry

    lax.fori_loop(0, seg_len // PERM_ROWS, scatter, 0)

    zero = jnp.zeros((HALO, LANES), F32)
    for s in range(NSEG):
        nxt = u_ref[0, pl.ds((s + 1) * seg_len, HALO), :] if s + 1 < NSEG else None
        prv = u_ref[0, pl.ds(s * seg_len - HALO, HALO), :] if s > 0 else None
        for q in range(nq):
            lanes = slice(q * LANES, (q + 1) * LANES)
            ext[q, pl.ds(h_rows + seq + s, HALO, stride=NSEG), :] = zero if nxt is None else nxt[:, lanes]
            ext[q, pl.ds(s, HALO, stride=NSEG), :] = zero if prv is None else prv[:, lanes]


def _emit_time_order(src, o_ref, seq, post):
    seg_len = seq // NSEG

    def gather(bi, carry):
        j0 = pl.multiple_of(bi * PERM_ROWS, PERM_ROWS)
        for s in range(NSEG):
            val = jnp.concatenate(
                [src[q, pl.ds(j0 * NSEG + s, PERM_ROWS, stride=NSEG), :] for q in range(src.shape[0])], axis=1)
            r_time = s * seg_len + j0
            o_ref[0, pl.ds(r_time, PERM_ROWS), :] = post(val, r_time).astype(o_ref.dtype)
        return carry

    lax.fori_loop(0, seg_len // PERM_ROWS, gather, 0)


def _pool_kernel(u_ref, w_ref, s_ref, o_ref, ext, y_buf, *, seq, chunk):
    seg_len = seq // NSEG
    h_rows = HALO * SUBLANES
    nq = ext.shape[0]
    _fill_ext(ext, u_ref, seq)
    grp = pl.program_id(1)
    n_chunks = seq // chunk

    def body(win):
        half = win // 2

        def do_chunk(ci, carry):
            r0 = pl.multiple_of(ci * chunk, chunk)
            base = r0 + h_rows
            row = r0 + lax.broadcasted_iota(jnp.int32, (chunk, LANES), 0)
            t = (row & (SUBLANES - 1)) * seg_len + (row >> 3)
            cnt = jnp.minimum(t + half, seq) - jnp.maximum(t - half, 0)
            inv = 1.0 / cnt.astype(F32)
            d = []
            for q in range(nq):
                tot = ext[q, pl.ds(base - half * SUBLANES, chunk), :]
                for m in range(-half + 1, half):
                    tot = tot + ext[q, pl.ds(base + m * SUBLANES, chunk), :]
                d.append(tot * inv - ext[q, pl.ds(base, chunk), :])
            d = jnp.concatenate(d, axis=1)
            y = jnp.dot(d.astype(BF16), w_ref[0], preferred_element_type=F32) * s_ref[0]
            _store_slabs(y_buf, r0, chunk, y)
            return carry

        lax.fori_loop(0, n_chunks, do_chunk, 0)

    for gi, win in enumerate(POOL_WINDOWS):
        pl.when(grp == gi)(functools.partial(body, win))

    _emit_time_order(y_buf, o_ref, seq, lambda val, r_time: val)


def _pool_branch(z_a, pool_w, pool_scale, batch, seq, chunk=512):
    n_groups, gw = pool_w.shape[0], pool_w.shape[1]
    z3 = z_a.reshape(batch, seq, z_a.shape[1])
    nq = gw // LANES
    return pl.pallas_call(
        functools.partial(_pool_kernel, seq=seq, chunk=chunk),
        grid=(batch, n_groups),
        in_specs=[pl.BlockSpec((1, seq, gw), lambda b, g: (b, 0, g)),
                  pl.BlockSpec((1, gw, gw), lambda b, g: (g, 0, 0)),
                  pl.BlockSpec((1, 1, gw), lambda b, g: (g, 0, 0))],
        out_specs=pl.BlockSpec((1, seq, gw), lambda b, g: (b, 0, g)),
        out_shape=jax.ShapeDtypeStruct((batch, seq, n_groups * gw), BF16),
        scratch_shapes=[pltpu.VMEM((nq, seq + 2 * HALO * SUBLANES, LANES), F32),
                        pltpu.VMEM((nq, seq, LANES), F32)],
        compiler_params=pltpu.CompilerParams(
            dimension_semantics=("arbitrary", "arbitrary"),
            vmem_limit_bytes=VMEM_LIMIT),
        name="pool_branch",
    )(z3, pool_w, pool_scale.reshape(n_groups, 1, gw))


_P_CONV_W = 0
_P_CONV_B = CONV_WIDTH
_P_DIR = CONV_WIDTH + 1
_P_ROWS = 16


def _lru_kernel(u_ref, gate_ref, w_ref, p_ref, o_ref, ext, a_f, x_f, a_b, x_b, y_buf, summ, ent, *, seq, chunk):
    h_rows = HALO * SUBLANES
    c = LRU_BLOCK
    n_chunks = seq // chunk
    groups = chunk // SUBLANES
    _fill_ext(ext, u_ref, seq)

    conv_hw = [0.5 * p_ref[pl.ds(_P_CONV_W + k, 1), :] for k in range(CONV_WIDTH)]
    conv_hb = 0.5 * p_ref[pl.ds(_P_CONV_B, 1), :]
    hb_a, hb_x, c2 = [], [], []
    for d in range(2):
        hb_a.append(0.5 * p_ref[pl.ds(_P_DIR + 3 * d, 1), :])
        hb_x.append(0.5 * p_ref[pl.ds(_P_DIR + 3 * d + 1, 1), :])
        lam = p_ref[pl.ds(_P_DIR + 3 * d + 2, 1), :]
        c2.append((-0.5 * LRU_C * LOG2_E) * jax.nn.softplus(-lam))
    a_out = (a_f, a_b)
    x_out = (x_f, x_b)

    def rows(v, k):
        return v[k * SUBLANES:(k + 1) * SUBLANES]

    def summary_rows(ci):
        return pl.ds(pl.multiple_of(ci * SUBLANES, SUBLANES), SUBLANES)

    zeros = jnp.zeros((SUBLANES, c), F32)
    ones = jnp.ones((SUBLANES, c), F32)

    def gates(ci, carry):
        r0 = pl.multiple_of(ci * chunk, chunk)
        base = r0 + h_rows
        xh = conv_hb + _load_slabs(ext, base - CONV_LEFT * SUBLANES, chunk) * conv_hw[0]
        for k in range(1, CONV_WIDTH):
            xh = xh + _load_slabs(ext, base + (k - CONV_LEFT) * SUBLANES, chunk) * conv_hw[k]
        pre = jnp.dot(xh.astype(BF16), w_ref[0], preferred_element_type=F32)
        for d in range(2):
            t_r = jnp.tanh(pre[:, (2 * d) * c:(2 * d + 1) * c] + hb_a[d])
            t_i = jnp.tanh(pre[:, (2 * d + 1) * c:(2 * d + 2) * c] + hb_x[d])
            log2_a = c2[d] * t_r + c2[d]
            a = jnp.exp2(log2_a)
            one_m_a2 = (-1.0 - a * a) * jnp.tanh(LN_2 * log2_a)
            root = one_m_a2 * lax.rsqrt(jnp.maximum(one_m_a2, TINY))
            inp = root * (t_i * xh + xh)
            _store_slabs(a_out[d], r0, chunk, a)
            _store_slabs(x_out[d], r0, chunk, inp)
            h, p = zeros, ones
            for k in (range(groups) if d == 0 else range(groups - 1, -1, -1)):
                h = rows(a, k) * h + rows(inp, k)
                p = rows(a, k) * p
            summ[2 * d, summary_rows(ci), :] = h
            summ[2 * d + 1, summary_rows(ci), :] = p
        return carry

    lax.fori_loop(0, n_chunks, gates, 0)

    def summary(idx, ci):
        return summ[idx, ci * SUBLANES:(ci + 1) * SUBLANES, :]

    h_f, p_f, h_b, p_b = zeros, ones, zeros, ones
    for ci in range(n_chunks):
        cb = n_chunks - 1 - ci
        h_f = summary(0, ci) + summary(1, ci) * h_f
        p_f = summary(1, ci) * p_f
        h_b = summary(2, cb) + summary(3, cb) * h_b
        p_b = summary(3, cb) * p_b
    sub = lax.broadcasted_iota(jnp.int32, (SUBLANES, c), 0)
    e_f = zeros
    e_b = zeros
    for _ in range(NSEG - 1):
        e_f = jnp.where(sub == 0, 0.0, pltpu.roll(h_f + p_f * e_f, 1, 0))
        e_b = jnp.where(sub == SUBLANES - 1, 0.0, pltpu.roll(h_b + p_b * e_b, SUBLANES - 1, 0))
    for ci in range(n_chunks):
        cb = n_chunks - 1 - ci
        ent[0, ci * SUBLANES:(ci + 1) * SUBLANES, :] = e_f
        ent[1, cb * SUBLANES:(cb + 1) * SUBLANES, :] = e_b
        e_f = summary(0, ci) + summary(1, ci) * e_f
        e_b = summary(2, cb) + summary(3, cb) * e_b

    def replay(it, carry):
        for sub_i in range(REPLAY_CHUNKS):
            ci = it * REPLAY_CHUNKS + sub_i
            r0 = pl.multiple_of(ci * chunk, chunk)
            af, xf = _load_slabs(a_f, r0, chunk), _load_slabs(x_f, r0, chunk)
            ab, xb = _load_slabs(a_b, r0, chunk), _load_slabs(x_b, r0, chunk)
            h = ent[0, summary_rows(ci), :]
            fwd = []
            for k in range(groups):
                h = rows(af, k) * h + rows(xf, k)
                fwd.append(h)
            h = ent[1, summary_rows(ci), :]
            out = [None] * groups
            for k in range(groups - 1, -1, -1):
                h = rows(ab, k) * h + rows(xb, k)
                out[k] = h + fwd[k]
            _store_slabs(y_buf, r0, chunk, jnp.concatenate(out, axis=0))
        return carry

    lax.fori_loop(0, n_chunks // REPLAY_CHUNKS, replay, 0)

    _emit_time_order(y_buf, o_ref, seq, lambda val, r_time: val * gate_ref[0, pl.ds(r_time, PERM_ROWS), :])


def _lru_branch(z_a, gate_act, w_gates, params, batch, seq, lru_col0, chunk=256):
    n_heads = w_gates.shape[0]
    c = LRU_BLOCK
    nq = c // LANES
    z3 = z_a.reshape(batch, seq, z_a.shape[1])
    g3 = gate_act.reshape(batch, seq, gate_act.shape[1])
    col0 = lru_col0 // c
    n_chunks = seq // chunk
    seg_buf = pltpu.VMEM((nq, seq, LANES), F32)
    return pl.pallas_call(
        functools.partial(_lru_kernel, seq=seq, chunk=chunk),
        grid=(batch, n_heads),
        in_specs=[pl.BlockSpec((1, seq, c), lambda b, h: (b, 0, col0 + h)),
                  pl.BlockSpec((1, seq, c), lambda b, h: (b, 0, h)),
                  pl.BlockSpec((1, c, 4 * c), lambda b, h: (h, 0, 0)),
                  pl.BlockSpec((_P_ROWS, c), lambda b, h: (0, h))],
        out_specs=pl.BlockSpec((1, seq, c), lambda b, h: (b, 0, h)),
        out_shape=jax.ShapeDtypeStruct((batch, seq, n_heads * c), BF16),
        scratch_shapes=[pltpu.VMEM((nq, seq + 2 * HALO * SUBLANES, LANES), F32)]
        + [seg_buf] * 5
        + [pltpu.VMEM((4, n_chunks * SUBLANES, c), F32), pltpu.VMEM((2, n_chunks * SUBLANES, c), F32)],
        compiler_params=pltpu.CompilerParams(
            dimension_semantics=("arbitrary", "arbitrary"),
            vmem_limit_bytes=VMEM_LIMIT),
        name="lru_branch",
    )(z3, g3, w_gates, params)


def _merge_kernel(yp_ref, yl_ref, g0_ref, g1_ref, x_ref, wp_ref, wl_ref, wo_ref, bo_ref, g_ref, b_ref,
                  of_ref, ob_ref, *, alpha):
    up_p = jnp.dot(yp_ref[...], wp_ref[...], preferred_element_type=F32)
    up_l = jnp.dot(yl_ref[...], wl_ref[...], preferred_element_type=F32)
    m = g0_ref[...] * up_p + g1_ref[...] * up_l
    mix = jnp.dot(m.astype(BF16), wo_ref[...], preferred_element_type=F32) + bo_ref[...]
    y = _layer_norm(alpha * x_ref[...] + mix, g_ref[...], b_ref[...])
    of_ref[...] = y
    ob_ref[...] = y.astype(BF16)


def _merge(y_pool, y_lru, g_act, x_rows, w_pool_up, w_lru_up, w_out, b_out, ln_g, ln_b, alpha, tm=256):
    t, d = x_rows.shape
    row = lambda i: (i, 0)
    const = lambda i: (0, 0)
    wspec = pl.BlockSpec((d, d), const, pipeline_mode=pl.Buffered(1))
    vspec = pl.BlockSpec((1, d), const)
    return pl.pallas_call(
        functools.partial(_merge_kernel, alpha=alpha),
        grid=(t // tm,),
        in_specs=[pl.BlockSpec((tm, d), row), pl.BlockSpec((tm, d), row),
                  pl.BlockSpec((tm, d), lambda i: (i, 0)), pl.BlockSpec((tm, d), lambda i: (i, 1)),
                  pl.BlockSpec((tm, d), row), wspec, wspec, wspec, vspec, vspec, vspec],
        out_specs=[pl.BlockSpec((tm, d), row), pl.BlockSpec((tm, d), row)],
        out_shape=[jax.ShapeDtypeStruct((t, d), F32), jax.ShapeDtypeStruct((t, d), BF16)],
        compiler_params=pltpu.CompilerParams(
            dimension_semantics=("arbitrary",), vmem_limit_bytes=VMEM_LIMIT),
        name="merge_out_ln",
    )(y_pool, y_lru, g_act, g_act, x_rows, w_pool_up, w_lru_up, w_out,
      b_out.reshape(1, d), ln_g.reshape(1, d), ln_b.reshape(1, d))


def _mlp_kernel(xb_ref, xf_ref, w1_ref, b1_ref, w2_ref, b2_ref, g_ref, b_ref, o_ref, *, alpha):
    f = pl.program_id(1)

    @pl.when(f == 0)
    def _():
        o_ref[...] = jnp.zeros_like(o_ref)

    h = jnp.dot(xb_ref[...], w1_ref[0].astype(BF16), preferred_element_type=F32) + b1_ref[...]
    h = jnp.square(jnp.maximum(h, 0.0))
    o_ref[...] += jnp.dot(h.astype(BF16), w2_ref[0].astype(BF16), preferred_element_type=F32)

    @pl.when(f == pl.num_programs(1) - 1)
    def _():
        y = alpha * xf_ref[...] + o_ref[...] + b2_ref[...]
        o_ref[...] = _layer_norm(y, g_ref[...], b_ref[...])


def _mlp(x_b, x_f, w1, layer, b1, w2, b2, ln_g, ln_b, alpha, tm=1024, tf=512):
    t, d = x_f.shape
    dff = w1.shape[2]
    return pl.pallas_call(
        functools.partial(_mlp_kernel, alpha=alpha),
        grid=(t // tm, dff // tf),
        in_specs=[pl.BlockSpec((tm, d), lambda i, f: (i, 0)),
                  pl.BlockSpec((tm, d), lambda i, f: (i, 0), pipeline_mode=pl.Buffered(1)),
                  pl.BlockSpec((1, d, tf), lambda i, f: (layer, 0, f)),
                  pl.BlockSpec((1, tf), lambda i, f: (0, f)),
                  pl.BlockSpec((1, tf, d), lambda i, f: (layer, f, 0)),
                  pl.BlockSpec((1, d), lambda i, f: (0, 0)),
                  pl.BlockSpec((1, d), lambda i, f: (0, 0)),
                  pl.BlockSpec((1, d), lambda i, f: (0, 0))],
        out_specs=pl.BlockSpec((tm, d), lambda i, f: (i, 0)),
        out_shape=jax.ShapeDtypeStruct((t, d), F32),
        compiler_params=pltpu.CompilerParams(
            dimension_semantics=("arbitrary", "arbitrary"), vmem_limit_bytes=VMEM_LIMIT),
        name="mlp_ln",
    )(x_b, x_f, w1, b1.reshape(1, dff), w2, b2.reshape(1, d), ln_g.reshape(1, d), ln_b.reshape(1, d))


def _layer(x_rows, batch, seq, alpha, layer, w_in, pool_w, pool_scale, conv_w, conv_b, lru_wa, lru_ba, lru_wx,
           lru_bx, lru_lambda, w_pool_up, w_lru_up, w_out, b_out, ln1_g, ln1_b, w_ff1, b_ff1, w_ff2, b_ff2,
           ln2_g, ln2_b):
    pool_width = pool_w.shape[0] * pool_w.shape[1]
    lru_width = conv_w.shape[1]
    o1, o2, o3 = pool_width, pool_width + lru_width, pool_width + 2 * lru_width

    z_a = _mm_act(x_rows, w_in, layer, 0, o2, "none")
    gate_act = _mm_act(x_rows, w_in, layer, o2, o3 - o2, "gelu")
    g_act = _mm_act(x_rows, w_in, layer, o3, w_in.shape[2] - o3, "sigmoid")

    y_pool = _pool_branch(z_a, pool_w.astype(BF16), pool_scale, batch, seq)

    w_gates = jnp.concatenate([lru_wa[0], lru_wx[0], lru_wa[1], lru_wx[1]], axis=-1).astype(BF16)
    params = jnp.concatenate(
        [conv_w, conv_b[None], lru_ba[0][None], lru_bx[0][None], lru_lambda[0][None],
         lru_ba[1][None], lru_bx[1][None], lru_lambda[1][None]], axis=0).astype(F32)
    params = jnp.pad(params, ((0, _P_ROWS - params.shape[0]), (0, 0)))
    y_lru = _lru_branch(z_a, gate_act, w_gates, params, batch, seq, o1)

    t = batch * seq
    x1_f, x1_b = _merge(y_pool.reshape(t, -1), y_lru.reshape(t, -1), g_act, x_rows,
                        w_pool_up.astype(BF16), w_lru_up.astype(BF16), w_out.astype(BF16),
                        b_out, ln1_g, ln1_b, alpha)
    return _mlp(x1_b, x1_f, w_ff1, layer, b_ff1, w_ff2, b_ff2, ln2_g, ln2_b, alpha)


def kernel(x, w_in, pool_w, pool_scale, conv_w, conv_b, lru_wa, lru_ba, lru_wx, lru_bx, lru_lambda, w_pool_up, w_lru_up, w_out, b_out, ln1_g, ln1_b, w_ff1, b_ff1, w_ff2, b_ff2, ln2_g, ln2_b):
    batch, seq, d = x.shape
    depth = w_in.shape[0]
    alpha = (2.0 * depth) ** 0.25
    rows = x.reshape(batch * seq, d)
    for l in range(depth):
        rows = _layer(rows, batch, seq, alpha, l, w_in, pool_w[l], pool_scale[l], conv_w[l], conv_b[l],
                      lru_wa[l], lru_ba[l], lru_wx[l], lru_bx[l], lru_lambda[l], w_pool_up[l], w_lru_up[l],
                      w_out[l], b_out[l], ln1_g[l], ln1_b[l], w_ff1, b_ff1[l], w_ff2, b_ff2[l],
                      ln2_g[l], ln2_b[l])
    return rows.reshape(batch, seq, d)
```

```python
import functools

import jax
import jax.numpy as jnp
from jax import lax
from jax.experimental import pallas as pl
from jax.experimental.pallas import tpu as pltpu

SUBLANES = 8
LANES = 128
NSEG = SUBLANES

POOL_WINDOWS = (2, 4, 8, 16)
LRU_BLOCK = 256
CONV_WIDTH = 4
CONV_LEFT = CONV_WIDTH // 2
LRU_C = 8.0
LN_EPS = 1e-5
HALO = 8
REPLAY_CHUNKS = 2
PERM_ROWS = 128
LOG2_E = 1.4426950408889634
LN_2 = 0.6931471805599453
TINY = 1.1754944e-38

VMEM_LIMIT = 60 * 1024 * 1024

F32 = jnp.float32
BF16 = jnp.bfloat16


def _sigmoid(v):
    return 0.5 * jnp.tanh(0.5 * v) + 0.5


def _gelu_tanh(v):
    c = 0.7978845608028654
    return 0.5 * v * (1.0 + jnp.tanh(c * (v + 0.044715 * (v * v * v))))


def _layer_norm(y, g, b):
    mu = jnp.mean(y, axis=-1, keepdims=True)
    yc = y - mu
    var = jnp.mean(yc * yc, axis=-1, keepdims=True)
    return yc * lax.rsqrt(var + LN_EPS) * g + b


def _mm_act_kernel(*refs, act, emit_x, n_casts):
    x_ref, w_ref = refs[:2]
    cast_in = refs[2:2 + n_casts]
    o_ref = refs[2 + n_casts]
    rest = refs[3 + n_casts:]
    xb_ref = rest[0] if emit_x else None
    cast_out = rest[int(emit_x):int(emit_x) + n_casts]
    w_bf = rest[-1]

    @pl.when(pl.program_id(1) == 0)
    def _():
        w_bf[...] = w_ref[0].astype(BF16)

    xb = x_ref[...].astype(BF16)
    if emit_x:
        @pl.when(pl.program_id(0) == 0)
        def _():
            xb_ref[...] = xb

    acc = jnp.dot(xb, w_bf[...], preferred_element_type=F32)
    if act == "gelu":
        acc = _gelu_tanh(acc)
    elif act == "sigmoid":
        acc = _sigmoid(acc)
    o_ref[...] = acc.astype(o_ref.dtype)
    for src, dst in zip(cast_in, cast_out):
        dst[...] = src[...].astype(BF16)


def _mm_act(x, w, layer, col0, n, act, emit_x=False, casts=(), tm=1024, tn=1024):
    m, k = x.shape
    j0 = col0 // tn
    n_i = m // tm
    steps = (n // tn) * n_i
    cast_specs = [pl.BlockSpec((a.shape[0] // steps, a.shape[1]), lambda j, i: (j * n_i + i, 0)) for a in casts]
    out_specs = [pl.BlockSpec((tm, tn), lambda j, i: (i, j))]
    out_shape = [jax.ShapeDtypeStruct((m, n), F32)]
    if emit_x:
        out_specs.append(pl.BlockSpec((tm, k), lambda j, i: (jnp.where(j == 0, i, n_i - 1), 0)))
        out_shape.append(jax.ShapeDtypeStruct((m, k), BF16))
    out_specs += cast_specs
    out_shape += [jax.ShapeDtypeStruct(a.shape, BF16) for a in casts]
    return pl.pallas_call(
        functools.partial(_mm_act_kernel, act=act, emit_x=emit_x, n_casts=len(casts)),
        grid=(n // tn, n_i),
        in_specs=[pl.BlockSpec((tm, k), lambda j, i: (i, 0)),
                  pl.BlockSpec((1, k, tn), lambda j, i: (layer, 0, j0 + j))] + cast_specs,
        out_specs=out_specs,
        out_shape=out_shape,
        scratch_shapes=[pltpu.VMEM((k, tn), BF16)],
        compiler_params=pltpu.CompilerParams(
            dimension_semantics=("arbitrary", "arbitrary"),
            vmem_limit_bytes=VMEM_LIMIT),
        name=f"mm_in_{act}",
    )(x, w, *casts)


def _load_slabs(buf, r, n):
    return jnp.concatenate([buf[q, pl.ds(r, n), :] for q in range(buf.shape[0])], axis=1)


def _store_slabs(buf, r, n, val):
    for q in range(buf.shape[0]):
        buf[q, pl.ds(r, n), :] = val[:, q * LANES:(q + 1) * LANES]


def _fill_ext(ext, u_ref, seq):
    seg_len = seq // NSEG
    h_rows = HALO * SUBLANES
    nq = ext.shape[0]

    def scatter(bi, carry):
        j0 = pl.multiple_of(bi * PERM_ROWS, PERM_ROWS)
        for s in range(NSEG):
            val = u_ref[0, pl.ds(s * seg_len + j0, PERM_ROWS), :]
            for q in range(nq):
                ext[q, pl.ds(h_rows + j0 * NSEG + s, PERM_ROWS, stride=NSEG), :] = val[:, q * LANES:(q + 1) * LANES]
        return carry

    lax.fori_loop(0, seg_len // PERM_ROWS, scatter, 0)

    zero = jnp.zeros((HALO, LANES), F32)
    for s in range(NSEG):
        nxt = u_ref[0, pl.ds((s + 1) * seg_len, HALO), :] if s + 1 < NSEG else None
        prv = u_ref[0, pl.ds(s * seg_len - HALO, HALO), :] if s > 0 else None
        for q in range(nq):
            lanes = slice(q * LANES, (q + 1) * LANES)
            ext[q, pl.ds(h_rows + seq + s, HALO, stride=NSEG), :] = zero if nxt is None else nxt[:, lanes]
            ext[q, pl.ds(s, HALO, stride=NSEG), :] = zero if prv is None else prv[:, lanes]


def _emit_time_order(src, o_ref, seq, post):
    seg_len = seq // NSEG

    def gather(bi, carry):
        j0 = pl.multiple_of(bi * PERM_ROWS, PERM_ROWS)
        for s in range(NSEG):
            val = jnp.concatenate(
                [src[q, pl.ds(j0 * NSEG + s, PERM_ROWS, stride=NSEG), :] for q in range(src.shape[0])], axis=1)
            r_time = s * seg_len + j0
            o_ref[0, pl.ds(r_time, PERM_ROWS), :] = post(val, r_time).astype(o_ref.dtype)
        return carry

    lax.fori_loop(0, seg_len // PERM_ROWS, gather, 0)


def _pool_kernel(u_ref, w_ref, s_ref, o_ref, ext, y_buf, *, seq, chunk):
    seg_len = seq // NSEG
    h_rows = HALO * SUBLANES
    nq = ext.shape[0]
    _fill_ext(ext, u_ref, seq)
    grp = pl.program_id(1)
    n_chunks = seq // chunk

    def body(win):
        half = win // 2

        def do_chunk(ci, carry):
            r0 = pl.multiple_of(ci * chunk, chunk)
            base = r0 + h_rows
            row = r0 + lax.broadcasted_iota(jnp.int32, (chunk, LANES), 0)
            t = (row & (SUBLANES - 1)) * seg_len + (row >> 3)
            cnt = jnp.minimum(t + half, seq) - jnp.maximum(t - half, 0)
            inv = 1.0 / cnt.astype(F32)
            d = []
            for q in range(nq):
                span_rows = (win - 1) * SUBLANES
                rows_in = ext[q, pl.ds(base - half * SUBLANES, chunk + span_rows), :]
                tot = rows_in
                step = SUBLANES
                while step <= half * SUBLANES:
                    keep = tot.shape[0] - step
                    tot = tot[:keep] + tot[step:step + keep]
                    step *= 2
                d.append(tot * inv - rows_in[half * SUBLANES:half * SUBLANES + chunk])
            d = jnp.concatenate(d, axis=1)
            y = jnp.dot(d.astype(BF16), w_ref[0], preferred_element_type=F32) * s_ref[0]
            _store_slabs(y_buf, r0, chunk, y)
            return carry

        lax.fori_loop(0, n_chunks, do_chunk, 0)

    for gi, win in enumerate(POOL_WINDOWS):
        pl.when(grp == gi)(functools.partial(body, win))

    _emit_time_order(y_buf, o_ref, seq, lambda val, r_time: val)


def _pool_branch(z_a, pool_w, pool_scale, batch, seq, chunk=512):
    n_groups, gw = pool_w.shape[0], pool_w.shape[1]
    z3 = z_a.reshape(batch, seq, z_a.shape[1])
    nq = gw // LANES
    return pl.pallas_call(
        functools.partial(_pool_kernel, seq=seq, chunk=chunk),
        grid=(batch, n_groups),
        in_specs=[pl.BlockSpec((1, seq, gw), lambda b, g: (b, 0, g)),
                  pl.BlockSpec((1, gw, gw), lambda b, g: (g, 0, 0)),
                  pl.BlockSpec((1, 1, gw), lambda b, g: (g, 0, 0))],
        out_specs=pl.BlockSpec((1, seq, gw), lambda b, g: (b, 0, g)),
        out_shape=jax.ShapeDtypeStruct((batch, seq, n_groups * gw), BF16),
        scratch_shapes=[pltpu.VMEM((nq, seq + 2 * HALO * SUBLANES, LANES), F32),
                        pltpu.VMEM((nq, seq, LANES), F32)],
        compiler_params=pltpu.CompilerParams(
            dimension_semantics=("arbitrary", "arbitrary"),
            vmem_limit_bytes=VMEM_LIMIT),
        name="pool_branch",
    )(z3, pool_w, pool_scale.reshape(n_groups, 1, gw))


_P_CONV_W = 0
_P_CONV_B = CONV_WIDTH
_P_DIR = CONV_WIDTH + 1
_P_ROWS = 16


def _lru_kernel(u_ref, gate_ref, w_ref, p_ref, o_ref, ext, a_f, x_f, a_b, x_b, y_buf, summ, ent, *, seq, chunk):
    h_rows = HALO * SUBLANES
    c = LRU_BLOCK
    n_chunks = seq // chunk
    groups = chunk // SUBLANES
    _fill_ext(ext, u_ref, seq)

    conv_hw = [0.5 * p_ref[pl.ds(_P_CONV_W + k, 1), :] for k in range(CONV_WIDTH)]
    conv_hb = 0.5 * p_ref[pl.ds(_P_CONV_B, 1), :]
    hb_a, hb_x, c2 = [], [], []
    for d in range(2):
        hb_a.append(0.5 * p_ref[pl.ds(_P_DIR + 3 * d, 1), :])
        hb_x.append(0.5 * p_ref[pl.ds(_P_DIR + 3 * d + 1, 1), :])
        lam = p_ref[pl.ds(_P_DIR + 3 * d + 2, 1), :]
        c2.append((-0.5 * LRU_C * LOG2_E) * jax.nn.softplus(-lam))
    a_out = (a_f, a_b)
    x_out = (x_f, x_b)

    def rows(v, k):
        return v[k * SUBLANES:(k + 1) * SUBLANES]

    def summary_rows(ci):
        return pl.ds(pl.multiple_of(ci * SUBLANES, SUBLANES), SUBLANES)

    zeros = jnp.zeros((SUBLANES, c), F32)
    ones = jnp.ones((SUBLANES, c), F32)

    def gates(ci, carry):
        r0 = pl.multiple_of(ci * chunk, chunk)
        base = r0 + h_rows
        xh = conv_hb + _load_slabs(ext, base - CONV_LEFT * SUBLANES, chunk) * conv_hw[0]
        for k in range(1, CONV_WIDTH):
            xh = xh + _load_slabs(ext, base + (k - CONV_LEFT) * SUBLANES, chunk) * conv_hw[k]
        pre = jnp.dot(xh.astype(BF16), w_ref[0], preferred_element_type=F32)
        for d in range(2):
            t_r = jnp.tanh(pre[:, (2 * d) * c:(2 * d + 1) * c] + hb_a[d])
            t_i = jnp.tanh(pre[:, (2 * d + 1) * c:(2 * d + 2) * c] + hb_x[d])
            log2_a = c2[d] * t_r + c2[d]
            a = jnp.exp2(log2_a)
            one_m_a2 = (-1.0 - a * a) * jnp.tanh(LN_2 * log2_a)
            root = one_m_a2 * lax.rsqrt(jnp.maximum(one_m_a2, TINY))
            inp = root * (t_i * xh + xh)
            _store_slabs(a_out[d], r0, chunk, a)
            _store_slabs(x_out[d], r0, chunk, inp)
            h, p = zeros, ones
            for k in (range(groups) if d == 0 else range(groups - 1, -1, -1)):
                h = rows(a, k) * h + rows(inp, k)
                p = rows(a, k) * p
            summ[2 * d, summary_rows(ci), :] = h
            summ[2 * d + 1, summary_rows(ci), :] = p
        return carry

    lax.fori_loop(0, n_chunks, gates, 0)

    def summary(idx, ci):
        return summ[idx, ci * SUBLANES:(ci + 1) * SUBLANES, :]

    h_f, p_f, h_b, p_b = zeros, ones, zeros, ones
    for ci in range(n_chunks):
        cb = n_chunks - 1 - ci
        h_f = summary(0, ci) + summary(1, ci) * h_f
        p_f = summary(1, ci) * p_f
        h_b = summary(2, cb) + summary(3, cb) * h_b
        p_b = summary(3, cb) * p_b
    sub = lax.broadcasted_iota(jnp.int32, (SUBLANES, c), 0)
    e_f = zeros
    e_b = zeros
    for _ in range(NSEG - 1):
        e_f = jnp.where(sub == 0, 0.0, pltpu.roll(h_f + p_f * e_f, 1, 0))
        e_b = jnp.where(sub == SUBLANES - 1, 0.0, pltpu.roll(h_b + p_b * e_b, SUBLANES - 1, 0))
    for ci in range(n_chunks):
        cb = n_chunks - 1 - ci
        ent[0, ci * SUBLANES:(ci + 1) * SUBLANES, :] = e_f
        ent[1, cb * SUBLANES:(cb + 1) * SUBLANES, :] = e_b
        e_f = summary(0, ci) + summary(1, ci) * e_f
        e_b = summary(2, cb) + summary(3, cb) * e_b

    def replay(it, carry):
        for sub_i in range(REPLAY_CHUNKS):
            ci = it * REPLAY_CHUNKS + sub_i
            r0 = pl.multiple_of(ci * chunk, chunk)
            af, xf = _load_slabs(a_f, r0, chunk), _load_slabs(x_f, r0, chunk)
            ab, xb = _load_slabs(a_b, r0, chunk), _load_slabs(x_b, r0, chunk)
            h = ent[0, summary_rows(ci), :]
            fwd = []
            for k in range(groups):
                h = rows(af, k) * h + rows(xf, k)
                fwd.append(h)
            h = ent[1, summary_rows(ci), :]
            out = [None] * groups
            for k in range(groups - 1, -1, -1):
                h = rows(ab, k) * h + rows(xb, k)
                out[k] = h + fwd[k]
            _store_slabs(y_buf, r0, chunk, jnp.concatenate(out, axis=0))
        return carry

    lax.fori_loop(0, n_chunks // REPLAY_CHUNKS, replay, 0)

    _emit_time_order(y_buf, o_ref, seq, lambda val, r_time: val * gate_ref[0, pl.ds(r_time, PERM_ROWS), :])


def _lru_branch(z_a, gate_act, w_gates, params, batch, seq, lru_col0, chunk=512):
    n_heads = w_gates.shape[0]
    c = LRU_BLOCK
    nq = c // LANES
    z3 = z_a.reshape(batch, seq, z_a.shape[1])
    g3 = gate_act.reshape(batch, seq, gate_act.shape[1])
    col0 = lru_col0 // c
    n_chunks = seq // chunk
    seg_buf = pltpu.VMEM((nq, seq, LANES), F32)
    return pl.pallas_call(
        functools.partial(_lru_kernel, seq=seq, chunk=chunk),
        grid=(batch, n_heads),
        in_specs=[pl.BlockSpec((1, seq, c), lambda b, h: (b, 0, col0 + h)),
                  pl.BlockSpec((1, seq, c), lambda b, h: (b, 0, h)),
                  pl.BlockSpec((1, c, 4 * c), lambda b, h: (h, 0, 0)),
                  pl.BlockSpec((_P_ROWS, c), lambda b, h: (0, h))],
        out_specs=pl.BlockSpec((1, seq, c), lambda b, h: (b, 0, h)),
        out_shape=jax.ShapeDtypeStruct((batch, seq, n_heads * c), BF16),
        scratch_shapes=[pltpu.VMEM((nq, seq + 2 * HALO * SUBLANES, LANES), F32)]
        + [seg_buf] * 5
        + [pltpu.VMEM((4, n_chunks * SUBLANES, c), F32), pltpu.VMEM((2, n_chunks * SUBLANES, c), F32)],
        compiler_params=pltpu.CompilerParams(
            dimension_semantics=("arbitrary", "arbitrary"),
            vmem_limit_bytes=VMEM_LIMIT),
        name="lru_branch",
    )(z3, g3, w_gates, params)


def _merge_kernel(yp_ref, yl_ref, g0_ref, g1_ref, x_ref, wp_ref, wl_ref, wo_ref, bo_ref, g_ref, b_ref,
                  of_ref, ob_ref, *, alpha):
    up_p = jnp.dot(yp_ref[...], wp_ref[...], preferred_element_type=F32)
    up_l = jnp.dot(yl_ref[...], wl_ref[...], preferred_element_type=F32)
    m = g0_ref[...] * up_p + g1_ref[...] * up_l
    mix = jnp.dot(m.astype(BF16), wo_ref[...], preferred_element_type=F32) + bo_ref[...]
    y = _layer_norm(alpha * x_ref[...] + mix, g_ref[...], b_ref[...])
    of_ref[...] = y
    ob_ref[...] = y.astype(BF16)


def _merge(y_pool, y_lru, g_act, x_rows, w_pool_up, w_lru_up, w_out, b_out, ln_g, ln_b, alpha, tm=256):
    t, d = x_rows.shape
    row = lambda i: (i, 0)
    const = lambda i: (0, 0)
    wspec = pl.BlockSpec((d, d), const, pipeline_mode=pl.Buffered(1))
    vspec = pl.BlockSpec((1, d), const)
    return pl.pallas_call(
        functools.partial(_merge_kernel, alpha=alpha),
        grid=(t // tm,),
        in_specs=[pl.BlockSpec((tm, d), row), pl.BlockSpec((tm, d), row),
                  pl.BlockSpec((tm, d), lambda i: (i, 0)), pl.BlockSpec((tm, d), lambda i: (i, 1)),
                  pl.BlockSpec((tm, d), row), wspec, wspec, wspec, vspec, vspec, vspec],
        out_specs=[pl.BlockSpec((tm, d), row), pl.BlockSpec((tm, d), row)],
        out_shape=[jax.ShapeDtypeStruct((t, d), F32), jax.ShapeDtypeStruct((t, d), BF16)],
        compiler_params=pltpu.CompilerParams(
            dimension_semantics=("arbitrary",), vmem_limit_bytes=VMEM_LIMIT),
        name="merge_out_ln",
    )(y_pool, y_lru, g_act, g_act, x_rows, w_pool_up, w_lru_up, w_out,
      b_out.reshape(1, d), ln_g.reshape(1, d), ln_b.reshape(1, d))


def _mlp_kernel(xb_ref, xf_ref, w1_ref, b1_ref, w2_ref, b2_ref, g_ref, b_ref, o_ref, *, alpha):
    f = pl.program_id(1)

    @pl.when(f == 0)
    def _():
        o_ref[...] = jnp.zeros_like(o_ref)

    h = jnp.dot(xb_ref[...], w1_ref[0].astype(BF16), preferred_element_type=F32) + b1_ref[...]
    h = jnp.square(jnp.maximum(h, 0.0))
    o_ref[...] += jnp.dot(h.astype(BF16), w2_ref[0].astype(BF16), preferred_element_type=F32)

    @pl.when(f == pl.num_programs(1) - 1)
    def _():
        y = alpha * xf_ref[...] + o_ref[...] + b2_ref[...]
        o_ref[...] = _layer_norm(y, g_ref[...], b_ref[...])


def _mlp(x_b, x_f, w1, layer, b1, w2, b2, ln_g, ln_b, alpha, tm=1024, tf=512):
    t, d = x_f.shape
    dff = w1.shape[2]
    return pl.pallas_call(
        functools.partial(_mlp_kernel, alpha=alpha),
        grid=(t // tm, dff // tf),
        in_specs=[pl.BlockSpec((tm, d), lambda i, f: (i, 0)),
                  pl.BlockSpec((tm, d), lambda i, f: (i, 0), pipeline_mode=pl.Buffered(1)),
                  pl.BlockSpec((1, d, tf), lambda i, f: (layer, 0, f)),
                  pl.BlockSpec((1, tf), lambda i, f: (0, f)),
                  pl.BlockSpec((1, tf, d), lambda i, f: (layer, f, 0)),
                  pl.BlockSpec((1, d), lambda i, f: (0, 0)),
                  pl.BlockSpec((1, d), lambda i, f: (0, 0)),
                  pl.BlockSpec((1, d), lambda i, f: (0, 0))],
        out_specs=pl.BlockSpec((tm, d), lambda i, f: (i, 0)),
        out_shape=jax.ShapeDtypeStruct((t, d), F32),
        compiler_params=pltpu.CompilerParams(
            dimension_semantics=("arbitrary", "arbitrary"), vmem_limit_bytes=VMEM_LIMIT),
        name="mlp_ln",
    )(x_b, x_f, w1, b1.reshape(1, dff), w2, b2.reshape(1, d), ln_g.reshape(1, d), ln_b.reshape(1, d))


def _layer(x_rows, batch, seq, alpha, layer, w_in, pool_w, pool_scale, conv_w, conv_b, lru_wa, lru_ba, lru_wx,
           lru_bx, lru_lambda, w_pool_up, w_lru_up, w_out, b_out, ln1_g, ln1_b, w_ff1, b_ff1, w_ff2, b_ff2,
           ln2_g, ln2_b):
    pool_width = pool_w.shape[0] * pool_w.shape[1]
    lru_width = conv_w.shape[1]
    o1, o2, o3 = pool_width, pool_width + lru_width, pool_width + 2 * lru_width

    g_act, x_b = _mm_act(x_rows, w_in, layer, o3, w_in.shape[2] - o3, "sigmoid", emit_x=True, tm=512)
    z_a, = _mm_act(x_b, w_in, layer, 0, o2, "none")
    gate_act, w_pool_up_b, w_lru_up_b, w_out_b = _mm_act(
        x_b, w_in, layer, o2, o3 - o2, "gelu", casts=(w_pool_up, w_lru_up, w_out))

    y_pool = _pool_branch(z_a, pool_w.astype(BF16), pool_scale, batch, seq)

    w_gates = jnp.concatenate([lru_wa[0], lru_wx[0], lru_wa[1], lru_wx[1]], axis=-1).astype(BF16)
    params = jnp.concatenate(
        [conv_w, conv_b[None], lru_ba[0][None], lru_bx[0][None], lru_lambda[0][None],
         lru_ba[1][None], lru_bx[1][None], lru_lambda[1][None]], axis=0).astype(F32)
    params = jnp.pad(params, ((0, _P_ROWS - params.shape[0]), (0, 0)))
    y_lru = _lru_branch(z_a, gate_act, w_gates, params, batch, seq, o1)

    t = batch * seq
    x1_f, x1_b = _merge(y_pool.reshape(t, -1), y_lru.reshape(t, -1), g_act, x_rows,
                        w_pool_up_b, w_lru_up_b, w_out_b,
                        b_out, ln1_g, ln1_b, alpha)
    return _mlp(x1_b, x1_f, w_ff1, layer, b_ff1, w_ff2, b_ff2, ln2_g, ln2_b, alpha)


def kernel(x, w_in, pool_w, pool_scale, conv_w, conv_b, lru_wa, lru_ba, lru_wx, lru_bx, lru_lambda, w_pool_up, w_lru_up, w_out, b_out, ln1_g, ln1_b, w_ff1, b_ff1, w_ff2, b_ff2, ln2_g, ln2_b):
    batch, seq, d = x.shape
    depth = w_in.shape[0]
    alpha = (2.0 * depth) ** 0.25
    rows = x.reshape(batch * seq, d)
    for l in range(depth):
        rows = _layer(rows, batch, seq, alpha, l, w_in, pool_w[l], pool_scale[l], conv_w[l], conv_b[l],
                      lru_wa[l], lru_ba[l], lru_wx[l], lru_bx[l], lru_lambda[l], w_pool_up[l], w_lru_up[l],
                      w_out[l], b_out[l], ln1_g[l], ln1_b[l], w_ff1, b_ff1[l], w_ff2, b_ff2[l],
                      ln2_g[l], ln2_b[l])
    return rows.reshape(batch, seq, d)
```

```python
import functools

import jax
import jax.numpy as jnp
from jax import lax
from jax.experimental import pallas as pl
from jax.experimental.pallas import tpu as pltpu

SUBLANES = 8
LANES = 128
NSEG = SUBLANES

POOL_WINDOWS = (2, 4, 8, 16)
LRU_BLOCK = 256
CONV_WIDTH = 4
CONV_LEFT = CONV_WIDTH // 2
LRU_C = 8.0
LN_EPS = 1e-5
HALO = 8
REPLAY_CHUNKS = 2
PERM_ROWS = 32
LOG2_E = 1.4426950408889634
LN_2 = 0.6931471805599453
TINY = 1.1754944e-38

VMEM_LIMIT = 60 * 1024 * 1024

F32 = jnp.float32
BF16 = jnp.bfloat16


def _sigmoid(v):
    return 0.5 * jnp.tanh(0.5 * v) + 0.5


def _gelu_tanh(v):
    c = 0.7978845608028654
    return 0.5 * v * (1.0 + jnp.tanh(c * (v + 0.044715 * (v * v * v))))


def _layer_norm(y, g, b):
    mu = jnp.mean(y, axis=-1, keepdims=True)
    yc = y - mu
    var = jnp.mean(yc * yc, axis=-1, keepdims=True)
    return yc * lax.rsqrt(var + LN_EPS) * g + b


def _mm_act_kernel(*refs, act, n_casts):
    x_ref, w_ref = refs[:2]
    cast_in = refs[2:2 + n_casts]
    o_ref = refs[2 + n_casts]
    cast_out = refs[3 + n_casts:3 + 2 * n_casts]
    w_bf = refs[-1]

    @pl.when(pl.program_id(1) == 0)
    def _():
        w_bf[...] = w_ref[0].astype(BF16)

    acc = jnp.dot(x_ref[...].astype(BF16), w_bf[...], preferred_element_type=F32)
    if act == "gelu":
        acc = _gelu_tanh(acc)
    elif act == "sigmoid":
        acc = _sigmoid(acc)
    o_ref[...] = acc.astype(o_ref.dtype)
    for src, dst in zip(cast_in, cast_out):
        dst[...] = src[...].astype(BF16)


def _mm_act(x, w, layer, col0, n, act, casts=(), tm=1024, tn=1024):
    m, k = x.shape
    j0 = col0 // tn
    n_i = m // tm
    steps = (n // tn) * n_i
    cast_specs = [pl.BlockSpec((a.shape[0] // steps, a.shape[1]), lambda j, i: (j * n_i + i, 0)) for a in casts]
    return pl.pallas_call(
        functools.partial(_mm_act_kernel, act=act, n_casts=len(casts)),
        grid=(n // tn, n_i),
        in_specs=[pl.BlockSpec((tm, k), lambda j, i: (i, 0)),
                  pl.BlockSpec((1, k, tn), lambda j, i: (layer, 0, j0 + j))] + cast_specs,
        out_specs=[pl.BlockSpec((tm, tn), lambda j, i: (i, j))] + cast_specs,
        out_shape=[jax.ShapeDtypeStruct((m, n), F32)] + [jax.ShapeDtypeStruct(a.shape, BF16) for a in casts],
        scratch_shapes=[pltpu.VMEM((k, tn), BF16)],
        compiler_params=pltpu.CompilerParams(
            dimension_semantics=("arbitrary", "arbitrary"),
            vmem_limit_bytes=VMEM_LIMIT),
        name=f"mm_in_{act}",
    )(x, w, *casts)


def _load_slabs(buf, r, n):
    return jnp.concatenate([buf[q, pl.ds(r, n), :] for q in range(buf.shape[0])], axis=1)


def _store_slabs(buf, r, n, val):
    for q in range(buf.shape[0]):
        buf[q, pl.ds(r, n), :] = val[:, q * LANES:(q + 1) * LANES]


def _fill_ext(ext, u_ref, seq):
    seg_len = seq // NSEG
    h_rows = HALO * SUBLANES
    nq = ext.shape[0]

    def scatter(bi, carry):
        j0 = pl.multiple_of(bi * PERM_ROWS, PERM_ROWS)
        for s in range(NSEG):
            val = u_ref[0, pl.ds(s * seg_len + j0, PERM_ROWS), :]
            for q in range(nq):
                ext[q, pl.ds(h_rows + j0 * NSEG + s, PERM_ROWS, stride=NSEG), :] = val[:, q * LANES:(q + 1) * LANES]
        return carry

    lax.fori_loop(0, seg_len // PERM_ROWS, scatter, 0)

    zero = jnp.zeros((HALO, LANES), F32)
    for s in range(NSEG):
        nxt = u_ref[0, pl.ds((s + 1) * seg_len, HALO), :] if s + 1 < NSEG else None
        prv = u_ref[0, pl.ds(s * seg_len - HALO, HALO), :] if s > 0 else None
        for q in range(nq):
            lanes = slice(q * LANES, (q + 1) * LANES)
            ext[q, pl.ds(h_rows + seq + s, HALO, stride=NSEG), :] = zero if nxt is None else nxt[:, lanes]
            ext[q, pl.ds(s, HALO, stride=NSEG), :] = zero if prv is None else prv[:, lanes]


def _emit_time_order(src, o_ref, seq, post):
    seg_len = seq // NSEG

    def gather(bi, carry):
        j0 = pl.multiple_of(bi * PERM_ROWS, PERM_ROWS)
        for s in range(NSEG):
            val = jnp.concatenate(
                [src[q, pl.ds(j0 * NSEG + s, PERM_ROWS, stride=NSEG), :] for q in range(src.shape[0])], axis=1)
            r_time = s * seg_len + j0
            o_ref[0, pl.ds(r_time, PERM_ROWS), :] = post(val, r_time).astype(o_ref.dtype)
        return carry

    lax.fori_loop(0, seg_len // PERM_ROWS, gather, 0)


def _pool_kernel(u_ref, w_ref, s_ref, o_ref, ext, y_buf, *, seq, chunk):
    seg_len = seq // NSEG
    h_rows = HALO * SUBLANES
    nq = ext.shape[0]
    _fill_ext(ext, u_ref, seq)
    grp = pl.program_id(1)
    n_chunks = seq // chunk

    def body(win):
        half = win // 2

        def do_chunk(ci, carry=0):
            clipped = isinstance(ci, int)
            r0 = ci * chunk if clipped else pl.multiple_of(ci * chunk, chunk)
            base = r0 + h_rows
            if clipped:
                row = r0 + lax.broadcasted_iota(jnp.int32, (chunk, LANES), 0)
                t = (row & (SUBLANES - 1)) * seg_len + (row >> 3)
                cnt = jnp.minimum(t + half, seq) - jnp.maximum(t - half, 0)
                inv = 1.0 / cnt.astype(F32)
            else:
                inv = 1.0 / win
            d = []
            for q in range(nq):
                span_rows = (win - 1) * SUBLANES
                rows_in = ext[q, pl.ds(base - half * SUBLANES, chunk + span_rows), :]
                tot = rows_in
                step = SUBLANES
                while step <= half * SUBLANES:
                    keep = tot.shape[0] - step
                    tot = tot[:keep] + tot[step:step + keep]
                    step *= 2
                d.append(tot * inv - rows_in[half * SUBLANES:half * SUBLANES + chunk])
            d = jnp.concatenate(d, axis=1)
            y = jnp.dot(d.astype(BF16), w_ref[0], preferred_element_type=F32) * s_ref[0]
            _store_slabs(y_buf, r0, chunk, y)
            return carry

        assert chunk // SUBLANES >= half and n_chunks >= 2
        do_chunk(0)
        lax.fori_loop(1, n_chunks - 1, do_chunk, 0)
        do_chunk(n_chunks - 1)

    for gi, win in enumerate(POOL_WINDOWS):
        pl.when(grp == gi)(functools.partial(body, win))

    _emit_time_order(y_buf, o_ref, seq, lambda val, r_time: val)


def _pool_branch(z_a, pool_w, pool_scale, batch, seq, chunk=512):
    n_groups, gw = pool_w.shape[0], pool_w.shape[1]
    z3 = z_a.reshape(batch, seq, z_a.shape[1])
    nq = gw // LANES
    return pl.pallas_call(
        functools.partial(_pool_kernel, seq=seq, chunk=chunk),
        grid=(batch, n_groups),
        in_specs=[pl.BlockSpec((1, seq, gw), lambda b, g: (b, 0, g)),
                  pl.BlockSpec((1, gw, gw), lambda b, g: (g, 0, 0)),
                  pl.BlockSpec((1, 1, gw), lambda b, g: (g, 0, 0))],
        out_specs=pl.BlockSpec((1, seq, gw), lambda b, g: (b, 0, g)),
        out_shape=jax.ShapeDtypeStruct((batch, seq, n_groups * gw), BF16),
        scratch_shapes=[pltpu.VMEM((nq, seq + 2 * HALO * SUBLANES, LANES), F32),
                        pltpu.VMEM((nq, seq, LANES), F32)],
        compiler_params=pltpu.CompilerParams(
            dimension_semantics=("arbitrary", "arbitrary"),
            vmem_limit_bytes=VMEM_LIMIT),
        name="pool_branch",
    )(z3, pool_w, pool_scale.reshape(n_groups, 1, gw))


_P_CONV_W = 0
_P_CONV_B = CONV_WIDTH
_P_DIR = CONV_WIDTH + 1
_P_ROWS = 16


def _lru_kernel(u_ref, gate_ref, w_ref, p_ref, o_ref, ext, a_f, x_f, a_b, x_b, y_buf, summ, ent, *, seq, chunk):
    h_rows = HALO * SUBLANES
    c = LRU_BLOCK
    n_chunks = seq // chunk
    groups = chunk // SUBLANES
    _fill_ext(ext, u_ref, seq)

    conv_hw = [0.5 * p_ref[pl.ds(_P_CONV_W + k, 1), :] for k in range(CONV_WIDTH)]
    conv_hb = 0.5 * p_ref[pl.ds(_P_CONV_B, 1), :]
    hb_a, hb_x, c2 = [], [], []
    for d in range(2):
        hb_a.append(0.5 * p_ref[pl.ds(_P_DIR + 3 * d, 1), :])
        hb_x.append(0.5 * p_ref[pl.ds(_P_DIR + 3 * d + 1, 1), :])
        lam = p_ref[pl.ds(_P_DIR + 3 * d + 2, 1), :]
        c2.append((-0.5 * LRU_C * LOG2_E) * jax.nn.softplus(-lam))
    a_out = (a_f, a_b)
    x_out = (x_f, x_b)

    def rows(v, k):
        return v[k * SUBLANES:(k + 1) * SUBLANES]

    def summary_rows(ci):
        return pl.ds(pl.multiple_of(ci * SUBLANES, SUBLANES), SUBLANES)

    zeros = jnp.zeros((SUBLANES, c), F32)
    ones = jnp.ones((SUBLANES, c), F32)

    def gates(ci, carry):
        r0 = pl.multiple_of(ci * chunk, chunk)
        base = r0 + h_rows
        xh = conv_hb + _load_slabs(ext, base - CONV_LEFT * SUBLANES, chunk) * conv_hw[0]
        for k in range(1, CONV_WIDTH):
            xh = xh + _load_slabs(ext, base + (k - CONV_LEFT) * SUBLANES, chunk) * conv_hw[k]
        pre = jnp.dot(xh.astype(BF16), w_ref[0], preferred_element_type=F32)
        for d in range(2):
            t_r = jnp.tanh(pre[:, (2 * d) * c:(2 * d + 1) * c] + hb_a[d])
            t_i = jnp.tanh(pre[:, (2 * d + 1) * c:(2 * d + 2) * c] + hb_x[d])
            log2_a = c2[d] * t_r + c2[d]
            a = jnp.exp2(log2_a)
            one_m_a2 = (-1.0 - a * a) * jnp.tanh(LN_2 * log2_a)
            root = one_m_a2 * lax.rsqrt(jnp.maximum(one_m_a2, TINY))
            inp = root * (t_i * xh + xh)
            _store_slabs(a_out[d], r0, chunk, a)
            _store_slabs(x_out[d], r0, chunk, inp)
            h, p = zeros, ones
            for k in (range(groups) if d == 0 else range(groups - 1, -1, -1)):
                h = rows(a, k) * h + rows(inp, k)
                p = rows(a, k) * p
            summ[2 * d, summary_rows(ci), :] = h
            summ[2 * d + 1, summary_rows(ci), :] = p
        return carry

    lax.fori_loop(0, n_chunks, gates, 0)

    def summary(idx, ci):
        return summ[idx, ci * SUBLANES:(ci + 1) * SUBLANES, :]

    h_f, p_f, h_b, p_b = zeros, ones, zeros, ones
    for ci in range(n_chunks):
        cb = n_chunks - 1 - ci
        h_f = summary(0, ci) + summary(1, ci) * h_f
        p_f = summary(1, ci) * p_f
        h_b = summary(2, cb) + summary(3, cb) * h_b
        p_b = summary(3, cb) * p_b
    sub = lax.broadcasted_iota(jnp.int32, (SUBLANES, c), 0)
    e_f = zeros
    e_b = zeros
    for _ in range(NSEG - 1):
        e_f = jnp.where(sub == 0, 0.0, pltpu.roll(h_f + p_f * e_f, 1, 0))
        e_b = jnp.where(sub == SUBLANES - 1, 0.0, pltpu.roll(h_b + p_b * e_b, SUBLANES - 1, 0))
    for ci in range(n_chunks):
        cb = n_chunks - 1 - ci
        ent[0, ci * SUBLANES:(ci + 1) * SUBLANES, :] = e_f
        ent[1, cb * SUBLANES:(cb + 1) * SUBLANES, :] = e_b
        e_f = summary(0, ci) + summary(1, ci) * e_f
        e_b = summary(2, cb) + summary(3, cb) * e_b

    def replay(it, carry):
        for sub_i in range(REPLAY_CHUNKS):
            ci = it * REPLAY_CHUNKS + sub_i
            r0 = pl.multiple_of(ci * chunk, chunk)
            af, xf = _load_slabs(a_f, r0, chunk), _load_slabs(x_f, r0, chunk)
            ab, xb = _load_slabs(a_b, r0, chunk), _load_slabs(x_b, r0, chunk)
            h = ent[0, summary_rows(ci), :]
            fwd = []
            for k in range(groups):
                h = rows(af, k) * h + rows(xf, k)
                fwd.append(h)
            h = ent[1, summary_rows(ci), :]
            out = [None] * groups
            for k in range(groups - 1, -1, -1):
                h = rows(ab, k) * h + rows(xb, k)
                out[k] = h + fwd[k]
            _store_slabs(y_buf, r0, chunk, jnp.concatenate(out, axis=0))
        return carry

    lax.fori_loop(0, n_chunks // REPLAY_CHUNKS, replay, 0)

    _emit_time_order(y_buf, o_ref, seq, lambda val, r_time: val * gate_ref[0, pl.ds(r_time, PERM_ROWS), :])


def _lru_branch(z_a, gate_act, w_gates, params, batch, seq, lru_col0, chunk=512):
    n_heads = w_gates.shape[0]
    c = LRU_BLOCK
    nq = c // LANES
    z3 = z_a.reshape(batch, seq, z_a.shape[1])
    g3 = gate_act.reshape(batch, seq, gate_act.shape[1])
    col0 = lru_col0 // c
    n_chunks = seq // chunk
    seg_buf = pltpu.VMEM((nq, seq, LANES), F32)
    return pl.pallas_call(
        functools.partial(_lru_kernel, seq=seq, chunk=chunk),
        grid=(batch, n_heads),
        in_specs=[pl.BlockSpec((1, seq, c), lambda b, h: (b, 0, col0 + h)),
                  pl.BlockSpec((1, seq, c), lambda b, h: (b, 0, h)),
                  pl.BlockSpec((1, c, 4 * c), lambda b, h: (h, 0, 0)),
                  pl.BlockSpec((_P_ROWS, c), lambda b, h: (0, h))],
        out_specs=pl.BlockSpec((1, seq, c), lambda b, h: (b, 0, h)),
        out_shape=jax.ShapeDtypeStruct((batch, seq, n_heads * c), BF16),
        scratch_shapes=[pltpu.VMEM((nq, seq + 2 * HALO * SUBLANES, LANES), F32)]
        + [seg_buf] * 5
        + [pltpu.VMEM((4, n_chunks * SUBLANES, c), F32), pltpu.VMEM((2, n_chunks * SUBLANES, c), F32)],
        compiler_params=pltpu.CompilerParams(
            dimension_semantics=("arbitrary", "arbitrary"),
            vmem_limit_bytes=VMEM_LIMIT),
        name="lru_branch",
    )(z3, g3, w_gates, params)


def _merge_kernel(yp_ref, yl_ref, g0_ref, g1_ref, x_ref, wp_ref, wl_ref, wo_ref, bo_ref, g_ref, b_ref,
                  of_ref, ob_ref, *, alpha):
    up_p = jnp.dot(yp_ref[...], wp_ref[...], preferred_element_type=F32)
    up_l = jnp.dot(yl_ref[...], wl_ref[...], preferred_element_type=F32)
    m = g0_ref[...] * up_p + g1_ref[...] * up_l
    mix = jnp.dot(m.astype(BF16), wo_ref[...], preferred_element_type=F32) + bo_ref[...]
    y = _layer_norm(alpha * x_ref[...] + mix, g_ref[...], b_ref[...])
    of_ref[...] = y
    ob_ref[...] = y.astype(BF16)


def _merge(y_pool, y_lru, g_act, x_rows, w_pool_up, w_lru_up, w_out, b_out, ln_g, ln_b, alpha, tm=256):
    t, d = x_rows.shape
    row = lambda i: (i, 0)
    const = lambda i: (0, 0)
    wspec = pl.BlockSpec((d, d), const, pipeline_mode=pl.Buffered(1))
    vspec = pl.BlockSpec((1, d), const)
    return pl.pallas_call(
        functools.partial(_merge_kernel, alpha=alpha),
        grid=(t // tm,),
        in_specs=[pl.BlockSpec((tm, d), row), pl.BlockSpec((tm, d), row),
                  pl.BlockSpec((tm, d), lambda i: (i, 0)), pl.BlockSpec((tm, d), lambda i: (i, 1)),
                  pl.BlockSpec((tm, d), row), wspec, wspec, wspec, vspec, vspec, vspec],
        out_specs=[pl.BlockSpec((tm, d), row), pl.BlockSpec((tm, d), row)],
        out_shape=[jax.ShapeDtypeStruct((t, d), F32), jax.ShapeDtypeStruct((t, d), BF16)],
        compiler_params=pltpu.CompilerParams(
            dimension_semantics=("arbitrary",), vmem_limit_bytes=VMEM_LIMIT),
        name="merge_out_ln",
    )(y_pool, y_lru, g_act, g_act, x_rows, w_pool_up, w_lru_up, w_out,
      b_out.reshape(1, d), ln_g.reshape(1, d), ln_b.reshape(1, d))


def _mlp_kernel(xb_ref, xf_ref, w1_ref, b1_ref, w2_ref, b2_ref, g_ref, b_ref, o_ref, *, alpha):
    f = pl.program_id(1)

    @pl.when(f == 0)
    def _():
        o_ref[...] = alpha * xf_ref[...] + b2_ref[...]

    h = jnp.dot(xb_ref[...], w1_ref[0].astype(BF16), preferred_element_type=F32) + b1_ref[...]
    h = jnp.square(jnp.maximum(h, 0.0))
    o_ref[...] += jnp.dot(h.astype(BF16), w2_ref[0].astype(BF16), preferred_element_type=F32)

    @pl.when(f == pl.num_programs(1) - 1)
    def _():
        o_ref[...] = _layer_norm(o_ref[...], g_ref[...], b_ref[...])


def _mlp(x_b, x_f, w1, layer, b1, w2, b2, ln_g, ln_b, alpha, tm=1024, tf=512):
    t, d = x_f.shape
    dff = w1.shape[2]
    return pl.pallas_call(
        functools.partial(_mlp_kernel, alpha=alpha),
        grid=(t // tm, dff // tf),
        in_specs=[pl.BlockSpec((tm, d), lambda i, f: (i, 0)),
                  pl.BlockSpec((tm, d), lambda i, f: (i, 0), pipeline_mode=pl.Buffered(1)),
                  pl.BlockSpec((1, d, tf), lambda i, f: (layer, 0, f)),
                  pl.BlockSpec((1, tf), lambda i, f: (0, f)),
                  pl.BlockSpec((1, tf, d), lambda i, f: (layer, f, 0)),
                  pl.BlockSpec((1, d), lambda i, f: (0, 0)),
                  pl.BlockSpec((1, d), lambda i, f: (0, 0)),
                  pl.BlockSpec((1, d), lambda i, f: (0, 0))],
        out_specs=pl.BlockSpec((tm, d), lambda i, f: (i, 0)),
        out_shape=jax.ShapeDtypeStruct((t, d), F32),
        compiler_params=pltpu.CompilerParams(
            dimension_semantics=("arbitrary", "arbitrary"), vmem_limit_bytes=VMEM_LIMIT),
        name="mlp_ln",
    )(x_b, x_f, w1, b1.reshape(1, dff), w2, b2.reshape(1, d), ln_g.reshape(1, d), ln_b.reshape(1, d))


def _layer(x_rows, batch, seq, alpha, layer, w_in, pool_w, pool_scale, conv_w, conv_b, lru_wa, lru_ba, lru_wx,
           lru_bx, lru_lambda, w_pool_up, w_lru_up, w_out, b_out, ln1_g, ln1_b, w_ff1, b_ff1, w_ff2, b_ff2,
           ln2_g, ln2_b):
    pool_width = pool_w.shape[0] * pool_w.shape[1]
    lru_width = conv_w.shape[1]
    o1, o2, o3 = pool_width, pool_width + lru_width, pool_width + 2 * lru_width

    z_a, = _mm_act(x_rows, w_in, layer, 0, o2, "none")
    g_act, = _mm_act(x_rows, w_in, layer, o3, w_in.shape[2] - o3, "sigmoid")
    gate_act, w_pool_up_b, w_lru_up_b, w_out_b = _mm_act(
        x_rows, w_in, layer, o2, o3 - o2, "gelu", casts=(w_pool_up, w_lru_up, w_out))

    y_pool = _pool_branch(z_a, pool_w.astype(BF16), pool_scale, batch, seq)

    w_gates = jnp.concatenate([lru_wa[0], lru_wx[0], lru_wa[1], lru_wx[1]], axis=-1).astype(BF16)
    params = jnp.concatenate(
        [conv_w, conv_b[None], lru_ba[0][None], lru_bx[0][None], lru_lambda[0][None],
         lru_ba[1][None], lru_bx[1][None], lru_lambda[1][None]], axis=0).astype(F32)
    params = jnp.pad(params, ((0, _P_ROWS - params.shape[0]), (0, 0)))
    y_lru = _lru_branch(z_a, gate_act, w_gates, params, batch, seq, o1)

    t = batch * seq
    x1_f, x1_b = _merge(y_pool.reshape(t, -1), y_lru.reshape(t, -1), g_act, x_rows,
                        w_pool_up_b, w_lru_up_b, w_out_b,
                        b_out, ln1_g, ln1_b, alpha)
    return _mlp(x1_b, x1_f, w_ff1, layer, b_ff1, w_ff2, b_ff2, ln2_g, ln2_b, alpha)


def kernel(x, w_in, pool_w, pool_scale, conv_w, conv_b, lru_wa, lru_ba, lru_wx, lru_bx, lru_lambda, w_pool_up, w_lru_up, w_out, b_out, ln1_g, ln1_b, w_ff1, b_ff1, w_ff2, b_ff2, ln2_g, ln2_b):
    batch, seq, d = x.shape
    depth = w_in.shape[0]
    alpha = (2.0 * depth) ** 0.25
    rows = x.reshape(batch * seq, d)
    for l in range(depth):
        rows = _layer(rows, batch, seq, alpha, l, w_in, pool_w[l], pool_scale[l], conv_w[l], conv_b[l],
                      lru_wa[l], lru_ba[l], lru_wx[l], lru_bx[l], lru_lambda[l], w_pool_up[l], w_lru_up[l],
                      w_out[l], b_out[l], ln1_g[l], ln1_b[l], w_ff1, b_ff1[l], w_ff2, b_ff2[l],
                      ln2_g[l], ln2_b[l])
    return rows.reshape(batch, seq, d)
```

```python
import functools

import jax
import jax.numpy as jnp
from jax import lax
from jax.experimental import pallas as pl
from jax.experimental.pallas import tpu as pltpu

SUBLANES = 8
LANES = 128
NSEG = SUBLANES

POOL_WINDOWS = (2, 4, 8, 16)
LRU_BLOCK = 256
CONV_WIDTH = 4
CONV_LEFT = CONV_WIDTH // 2
LRU_C = 8.0
LN_EPS = 1e-5
HALO = 8
REPLAY_CHUNKS = 2
PERM_ROWS = 32
LOG2_E = 1.4426950408889634
LN_2 = 0.6931471805599453
TINY = 1.1754944e-38

VMEM_LIMIT = 60 * 1024 * 1024

F32 = jnp.float32
BF16 = jnp.bfloat16


def _sigmoid(v):
    return 0.5 * jnp.tanh(0.5 * v) + 0.5


def _gelu_tanh(v):
    c = 0.7978845608028654
    return 0.5 * v * (1.0 + jnp.tanh(c * (v + 0.044715 * (v * v * v))))


def _layer_norm(y, g, b):
    mu = jnp.mean(y, axis=-1, keepdims=True)
    yc = y - mu
    var = jnp.mean(yc * yc, axis=-1, keepdims=True)
    return yc * lax.rsqrt(var + LN_EPS) * g + b


def _mm_act_kernel(*refs, act, n_casts):
    x_ref, w_ref = refs[:2]
    cast_in = refs[2:2 + n_casts]
    o_ref = refs[2 + n_casts]
    cast_out = refs[3 + n_casts:3 + 2 * n_casts]
    w_bf = refs[-1]

    @pl.when(pl.program_id(1) == 0)
    def _():
        w_bf[...] = w_ref[0].astype(BF16)

    acc = jnp.dot(x_ref[...].astype(BF16), w_bf[...], preferred_element_type=F32)
    if act == "gelu":
        acc = _gelu_tanh(acc)
    elif act == "sigmoid":
        acc = _sigmoid(acc)
    o_ref[...] = acc.astype(o_ref.dtype)
    for src, dst in zip(cast_in, cast_out):
        dst[...] = src[...].astype(BF16)


def _mm_act(x, w, layer, col0, n, act, casts=(), tm=512, tn=2048):
    m, k = x.shape
    j0 = col0 // tn
    n_i = m // tm
    steps = (n // tn) * n_i
    cast_specs = [pl.BlockSpec((a.shape[0] // steps, a.shape[1]), lambda j, i: (j * n_i + i, 0)) for a in casts]
    return pl.pallas_call(
        functools.partial(_mm_act_kernel, act=act, n_casts=len(casts)),
        grid=(n // tn, n_i),
        in_specs=[pl.BlockSpec((tm, k), lambda j, i: (i, 0)),
                  pl.BlockSpec((1, k, tn), lambda j, i: (layer, 0, j0 + j))] + cast_specs,
        out_specs=[pl.BlockSpec((tm, tn), lambda j, i: (i, j))] + cast_specs,
        out_shape=[jax.ShapeDtypeStruct((m, n), F32)] + [jax.ShapeDtypeStruct(a.shape, BF16) for a in casts],
        scratch_shapes=[pltpu.VMEM((k, tn), BF16)],
        compiler_params=pltpu.CompilerParams(
            dimension_semantics=("arbitrary", "arbitrary"),
            vmem_limit_bytes=VMEM_LIMIT),
        name=f"mm_in_{act}",
    )(x, w, *casts)


def _load_slabs(buf, r, n):
    return jnp.concatenate([buf[q, pl.ds(r, n), :] for q in range(buf.shape[0])], axis=1)


def _store_slabs(buf, r, n, val):
    for q in range(buf.shape[0]):
        buf[q, pl.ds(r, n), :] = val[:, q * LANES:(q + 1) * LANES]


def _fill_ext(ext, u_ref, seq):
    seg_len = seq // NSEG
    h_rows = HALO * SUBLANES
    nq = ext.shape[0]

    def scatter(bi, carry):
        j0 = pl.multiple_of(bi * PERM_ROWS, PERM_ROWS)
        for s in range(NSEG):
            val = u_ref[0, pl.ds(s * seg_len + j0, PERM_ROWS), :]
            for q in range(nq):
                ext[q, pl.ds(h_rows + j0 * NSEG + s, PERM_ROWS, stride=NSEG), :] = val[:, q * LANES:(q + 1) * LANES]
        return carry

    lax.fori_loop(0, seg_len // PERM_ROWS, scatter, 0)

    zero = jnp.zeros((HALO, LANES), F32)
    for s in range(NSEG):
        nxt = u_ref[0, pl.ds((s + 1) * seg_len, HALO), :] if s + 1 < NSEG else None
        prv = u_ref[0, pl.ds(s * seg_len - HALO, HALO), :] if s > 0 else None
        for q in range(nq):
            lanes = slice(q * LANES, (q + 1) * LANES)
            ext[q, pl.ds(h_rows + seq + s, HALO, stride=NSEG), :] = zero if nxt is None else nxt[:, lanes]
            ext[q, pl.ds(s, HALO, stride=NSEG), :] = zero if prv is None else prv[:, lanes]


def _emit_time_order(src, o_ref, seq, post):
    seg_len = seq // NSEG

    def gather(bi, carry):
        j0 = pl.multiple_of(bi * PERM_ROWS, PERM_ROWS)
        for s in range(NSEG):
            val = jnp.concatenate(
                [src[q, pl.ds(j0 * NSEG + s, PERM_ROWS, stride=NSEG), :] for q in range(src.shape[0])], axis=1)
            r_time = s * seg_len + j0
            o_ref[0, pl.ds(r_time, PERM_ROWS), :] = post(val, r_time).astype(o_ref.dtype)
        return carry

    lax.fori_loop(0, seg_len // PERM_ROWS, gather, 0)


def _pool_kernel(u_ref, w_ref, s_ref, o_ref, ext, y_buf, *, seq, chunk):
    seg_len = seq // NSEG
    h_rows = HALO * SUBLANES
    nq = ext.shape[0]
    _fill_ext(ext, u_ref, seq)
    grp = pl.program_id(1)
    n_chunks = seq // chunk

    def body(win):
        half = win // 2

        def do_chunk(ci, carry=0):
            clipped = isinstance(ci, int)
            r0 = ci * chunk if clipped else pl.multiple_of(ci * chunk, chunk)
            base = r0 + h_rows
            if clipped:
                row = r0 + lax.broadcasted_iota(jnp.int32, (chunk, LANES), 0)
                t = (row & (SUBLANES - 1)) * seg_len + (row >> 3)
                cnt = jnp.minimum(t + half, seq) - jnp.maximum(t - half, 0)
                inv = 1.0 / cnt.astype(F32)
            else:
                inv = 1.0 / win
            d = []
            for q in range(nq):
                span_rows = (win - 1) * SUBLANES
                rows_in = ext[q, pl.ds(base - half * SUBLANES, chunk + span_rows), :]
                tot = rows_in
                step = SUBLANES
                while step <= half * SUBLANES:
                    keep = tot.shape[0] - step
                    tot = tot[:keep] + tot[step:step + keep]
                    step *= 2
                d.append(tot * inv - rows_in[half * SUBLANES:half * SUBLANES + chunk])
            d = jnp.concatenate(d, axis=1)
            y = jnp.dot(d.astype(BF16), w_ref[0], preferred_element_type=F32) * s_ref[0]
            _store_slabs(y_buf, r0, chunk, y)
            return carry

        assert chunk // SUBLANES >= half and n_chunks >= 2
        do_chunk(0)
        lax.fori_loop(1, n_chunks - 1, do_chunk, 0)
        do_chunk(n_chunks - 1)

    for gi, win in enumerate(POOL_WINDOWS):
        pl.when(grp == gi)(functools.partial(body, win))

    _emit_time_order(y_buf, o_ref, seq, lambda val, r_time: val)


def _pool_branch(z_a, pool_w, pool_scale, batch, seq, chunk=512):
    n_groups, gw = pool_w.shape[0], pool_w.shape[1]
    z3 = z_a.reshape(batch, seq, z_a.shape[1])
    nq = gw // LANES
    return pl.pallas_call(
        functools.partial(_pool_kernel, seq=seq, chunk=chunk),
        grid=(batch, n_groups),
        in_specs=[pl.BlockSpec((1, seq, gw), lambda b, g: (b, 0, g)),
                  pl.BlockSpec((1, gw, gw), lambda b, g: (g, 0, 0)),
                  pl.BlockSpec((1, 1, gw), lambda b, g: (g, 0, 0))],
        out_specs=pl.BlockSpec((1, seq, gw), lambda b, g: (b, 0, g)),
        out_shape=jax.ShapeDtypeStruct((batch, seq, n_groups * gw), BF16),
        scratch_shapes=[pltpu.VMEM((nq, seq + 2 * HALO * SUBLANES, LANES), F32),
                        pltpu.VMEM((nq, seq, LANES), F32)],
        compiler_params=pltpu.CompilerParams(
            dimension_semantics=("arbitrary", "arbitrary"),
            vmem_limit_bytes=VMEM_LIMIT),
        name="pool_branch",
    )(z3, pool_w, pool_scale.reshape(n_groups, 1, gw))


_P_CONV_W = 0
_P_CONV_B = CONV_WIDTH
_P_DIR = CONV_WIDTH + 1
_P_ROWS = 16


def _lru_kernel(u_ref, gate_ref, w_ref, p_ref, o_ref, ext, a_f, x_f, a_b, x_b, y_buf, summ, ent, *, seq, chunk):
    h_rows = HALO * SUBLANES
    c = LRU_BLOCK
    n_chunks = seq // chunk
    groups = chunk // SUBLANES
    _fill_ext(ext, u_ref, seq)

    conv_hw = [0.5 * p_ref[pl.ds(_P_CONV_W + k, 1), :] for k in range(CONV_WIDTH)]
    conv_hb = 0.5 * p_ref[pl.ds(_P_CONV_B, 1), :]
    hb_a, hb_x, c2 = [], [], []
    for d in range(2):
        hb_a.append(0.5 * p_ref[pl.ds(_P_DIR + 3 * d, 1), :])
        hb_x.append(0.5 * p_ref[pl.ds(_P_DIR + 3 * d + 1, 1), :])
        lam = p_ref[pl.ds(_P_DIR + 3 * d + 2, 1), :]
        c2.append((-0.5 * LRU_C * LOG2_E) * jax.nn.softplus(-lam))
    a_out = (a_f, a_b)
    x_out = (x_f, x_b)

    def rows(v, k):
        return v[k * SUBLANES:(k + 1) * SUBLANES]

    def summary_rows(ci):
        return pl.ds(pl.multiple_of(ci * SUBLANES, SUBLANES), SUBLANES)

    zeros = jnp.zeros((SUBLANES, c), F32)
    ones = jnp.ones((SUBLANES, c), F32)

    def gates(ci, carry):
        r0 = pl.multiple_of(ci * chunk, chunk)
        base = r0 + h_rows
        xh = conv_hb + _load_slabs(ext, base - CONV_LEFT * SUBLANES, chunk) * conv_hw[0]
        for k in range(1, CONV_WIDTH):
            xh = xh + _load_slabs(ext, base + (k - CONV_LEFT) * SUBLANES, chunk) * conv_hw[k]
        pre = jnp.dot(xh.astype(BF16), w_ref[0], preferred_element_type=F32)
        for d in range(2):
            t_r = jnp.tanh(pre[:, (2 * d) * c:(2 * d + 1) * c] + hb_a[d])
            t_i = jnp.tanh(pre[:, (2 * d + 1) * c:(2 * d + 2) * c] + hb_x[d])
            log2_a = c2[d] * t_r + c2[d]
            a = jnp.exp2(log2_a)
            one_m_a2 = (-1.0 - a * a) * jnp.tanh(LN_2 * log2_a)
            root = one_m_a2 * lax.rsqrt(jnp.maximum(one_m_a2, TINY))
            inp = root * (t_i * xh + xh)
            _store_slabs(a_out[d], r0, chunk, a)
            _store_slabs(x_out[d], r0, chunk, inp)
            h, p = zeros, ones
            for k in (range(groups) if d == 0 else range(groups - 1, -1, -1)):
                h = rows(a, k) * h + rows(inp, k)
                p = rows(a, k) * p
            summ[2 * d, summary_rows(ci), :] = h
            summ[2 * d + 1, summary_rows(ci), :] = p
        return carry

    lax.fori_loop(0, n_chunks, gates, 0)

    def summary(idx, ci):
        return summ[idx, ci * SUBLANES:(ci + 1) * SUBLANES, :]

    h_f, p_f, h_b, p_b = zeros, ones, zeros, ones
    for ci in range(n_chunks):
        cb = n_chunks - 1 - ci
        h_f = summary(0, ci) + summary(1, ci) * h_f
        p_f = summary(1, ci) * p_f
        h_b = summary(2, cb) + summary(3, cb) * h_b
        p_b = summary(3, cb) * p_b
    sub = lax.broadcasted_iota(jnp.int32, (SUBLANES, c), 0)
    e_f = zeros
    e_b = zeros
    for _ in range(NSEG - 1):
        e_f = jnp.where(sub == 0, 0.0, pltpu.roll(h_f + p_f * e_f, 1, 0))
        e_b = jnp.where(sub == SUBLANES - 1, 0.0, pltpu.roll(h_b + p_b * e_b, SUBLANES - 1, 0))
    for ci in range(n_chunks):
        cb = n_chunks - 1 - ci
        ent[0, ci * SUBLANES:(ci + 1) * SUBLANES, :] = e_f
        ent[1, cb * SUBLANES:(cb + 1) * SUBLANES, :] = e_b
        e_f = summary(0, ci) + summary(1, ci) * e_f
        e_b = summary(2, cb) + summary(3, cb) * e_b

    def replay(it, carry):
        for sub_i in range(REPLAY_CHUNKS):
            ci = it * REPLAY_CHUNKS + sub_i
            r0 = pl.multiple_of(ci * chunk, chunk)
            af, xf = _load_slabs(a_f, r0, chunk), _load_slabs(x_f, r0, chunk)
            ab, xb = _load_slabs(a_b, r0, chunk), _load_slabs(x_b, r0, chunk)
            h = ent[0, summary_rows(ci), :]
            fwd = []
            for k in range(groups):
                h = rows(af, k) * h + rows(xf, k)
                fwd.append(h)
            h = ent[1, summary_rows(ci), :]
            out = [None] * groups
            for k in range(groups - 1, -1, -1):
                h = rows(ab, k) * h + rows(xb, k)
                out[k] = h + fwd[k]
            _store_slabs(y_buf, r0, chunk, jnp.concatenate(out, axis=0))
        return carry

    lax.fori_loop(0, n_chunks // REPLAY_CHUNKS, replay, 0)

    _emit_time_order(y_buf, o_ref, seq, lambda val, r_time: val * gate_ref[0, pl.ds(r_time, PERM_ROWS), :])


def _lru_branch(z_a, gate_act, w_gates, params, batch, seq, lru_col0, chunk=512):
    n_heads = w_gates.shape[0]
    c = LRU_BLOCK
    nq = c // LANES
    z3 = z_a.reshape(batch, seq, z_a.shape[1])
    g3 = gate_act.reshape(batch, seq, gate_act.shape[1])
    col0 = lru_col0 // c
    n_chunks = seq // chunk
    seg_buf = pltpu.VMEM((nq, seq, LANES), F32)
    return pl.pallas_call(
        functools.partial(_lru_kernel, seq=seq, chunk=chunk),
        grid=(batch, n_heads),
        in_specs=[pl.BlockSpec((1, seq, c), lambda b, h: (b, 0, col0 + h)),
                  pl.BlockSpec((1, seq, c), lambda b, h: (b, 0, h)),
                  pl.BlockSpec((1, c, 4 * c), lambda b, h: (h, 0, 0)),
                  pl.BlockSpec((_P_ROWS, c), lambda b, h: (0, h))],
        out_specs=pl.BlockSpec((1, seq, c), lambda b, h: (b, 0, h)),
        out_shape=jax.ShapeDtypeStruct((batch, seq, n_heads * c), BF16),
        scratch_shapes=[pltpu.VMEM((nq, seq + 2 * HALO * SUBLANES, LANES), F32)]
        + [seg_buf] * 5
        + [pltpu.VMEM((4, n_chunks * SUBLANES, c), F32), pltpu.VMEM((2, n_chunks * SUBLANES, c), F32)],
        compiler_params=pltpu.CompilerParams(
            dimension_semantics=("arbitrary", "arbitrary"),
            vmem_limit_bytes=VMEM_LIMIT),
        name="lru_branch",
    )(z3, g3, w_gates, params)


def _merge_kernel(yp_ref, yl_ref, g0_ref, g1_ref, x_ref, wp_ref, wl_ref, wo_ref, bo_ref, g_ref, b_ref,
                  of_ref, ob_ref, *, alpha):
    up_p = jnp.dot(yp_ref[...], wp_ref[...], preferred_element_type=F32)
    up_l = jnp.dot(yl_ref[...], wl_ref[...], preferred_element_type=F32)
    m = g0_ref[...] * up_p + g1_ref[...] * up_l
    mix = jnp.dot(m.astype(BF16), wo_ref[...], preferred_element_type=F32) + bo_ref[...]
    y = _layer_norm(alpha * x_ref[...] + mix, g_ref[...], b_ref[...])
    of_ref[...] = y
    ob_ref[...] = y.astype(BF16)


def _merge(y_pool, y_lru, g_act, x_rows, w_pool_up, w_lru_up, w_out, b_out, ln_g, ln_b, alpha, tm=256):
    t, d = x_rows.shape
    row = lambda i: (i, 0)
    const = lambda i: (0, 0)
    wspec = pl.BlockSpec((d, d), const, pipeline_mode=pl.Buffered(1))
    vspec = pl.BlockSpec((1, d), const)
    return pl.pallas_call(
        functools.partial(_merge_kernel, alpha=alpha),
        grid=(t // tm,),
        in_specs=[pl.BlockSpec((tm, d), row), pl.BlockSpec((tm, d), row),
                  pl.BlockSpec((tm, d), lambda i: (i, 0)), pl.BlockSpec((tm, d), lambda i: (i, 1)),
                  pl.BlockSpec((tm, d), row), wspec, wspec, wspec, vspec, vspec, vspec],
        out_specs=[pl.BlockSpec((tm, d), row), pl.BlockSpec((tm, d), row)],
        out_shape=[jax.ShapeDtypeStruct((t, d), F32), jax.ShapeDtypeStruct((t, d), BF16)],
        compiler_params=pltpu.CompilerParams(
            dimension_semantics=("arbitrary",), vmem_limit_bytes=VMEM_LIMIT),
        name="merge_out_ln",
    )(y_pool, y_lru, g_act, g_act, x_rows, w_pool_up, w_lru_up, w_out,
      b_out.reshape(1, d), ln_g.reshape(1, d), ln_b.reshape(1, d))


def _mlp_kernel(xb_ref, xf_ref, w1_ref, b1_ref, w2_ref, b2_ref, g_ref, b_ref, o_ref, *, alpha):
    f = pl.program_id(1)

    @pl.when(f == 0)
    def _():
        o_ref[...] = alpha * xf_ref[...] + b2_ref[...]

    h = jnp.dot(xb_ref[...], w1_ref[0].astype(BF16), preferred_element_type=F32) + b1_ref[...]
    h = jnp.square(jnp.maximum(h, 0.0))
    o_ref[...] += jnp.dot(h.astype(BF16), w2_ref[0].astype(BF16), preferred_element_type=F32)

    @pl.when(f == pl.num_programs(1) - 1)
    def _():
        o_ref[...] = _layer_norm(o_ref[...], g_ref[...], b_ref[...])


def _mlp(x_b, x_f, w1, layer, b1, w2, b2, ln_g, ln_b, alpha, tm=1024, tf=512):
    t, d = x_f.shape
    dff = w1.shape[2]
    return pl.pallas_call(
        functools.partial(_mlp_kernel, alpha=alpha),
        grid=(t // tm, dff // tf),
        in_specs=[pl.BlockSpec((tm, d), lambda i, f: (i, 0)),
                  pl.BlockSpec((tm, d), lambda i, f: (i, 0), pipeline_mode=pl.Buffered(1)),
                  pl.BlockSpec((1, d, tf), lambda i, f: (layer, 0, f)),
                  pl.BlockSpec((1, tf), lambda i, f: (0, f)),
                  pl.BlockSpec((1, tf, d), lambda i, f: (layer, f, 0)),
                  pl.BlockSpec((1, d), lambda i, f: (0, 0)),
                  pl.BlockSpec((1, d), lambda i, f: (0, 0)),
                  pl.BlockSpec((1, d), lambda i, f: (0, 0))],
        out_specs=pl.BlockSpec((tm, d), lambda i, f: (i, 0)),
        out_shape=jax.ShapeDtypeStruct((t, d), F32),
        compiler_params=pltpu.CompilerParams(
            dimension_semantics=("arbitrary", "arbitrary"), vmem_limit_bytes=VMEM_LIMIT),
        name="mlp_ln",
    )(x_b, x_f, w1, b1.reshape(1, dff), w2, b2.reshape(1, d), ln_g.reshape(1, d), ln_b.reshape(1, d))


def _layer(x_rows, batch, seq, alpha, layer, w_in, pool_w, pool_scale, conv_w, conv_b, lru_wa, lru_ba, lru_wx,
           lru_bx, lru_lambda, w_pool_up, w_lru_up, w_out, b_out, ln1_g, ln1_b, w_ff1, b_ff1, w_ff2, b_ff2,
           ln2_g, ln2_b):
    pool_width = pool_w.shape[0] * pool_w.shape[1]
    lru_width = conv_w.shape[1]
    o1, o2, o3 = pool_width, pool_width + lru_width, pool_width + 2 * lru_width

    z_a, = _mm_act(x_rows, w_in, layer, 0, o2, "none")
    g_act, = _mm_act(x_rows, w_in, layer, o3, w_in.shape[2] - o3, "sigmoid")
    gate_act, w_pool_up_b, w_lru_up_b, w_out_b = _mm_act(
        x_rows, w_in, layer, o2, o3 - o2, "gelu", casts=(w_pool_up, w_lru_up, w_out))

    y_pool = _pool_branch(z_a, pool_w.astype(BF16), pool_scale, batch, seq)

    w_gates = jnp.concatenate([lru_wa[0], lru_wx[0], lru_wa[1], lru_wx[1]], axis=-1).astype(BF16)
    params = jnp.concatenate(
        [conv_w, conv_b[None], lru_ba[0][None], lru_bx[0][None], lru_lambda[0][None],
         lru_ba[1][None], lru_bx[1][None], lru_lambda[1][None]], axis=0).astype(F32)
    params = jnp.pad(params, ((0, _P_ROWS - params.shape[0]), (0, 0)))
    y_lru = _lru_branch(z_a, gate_act, w_gates, params, batch, seq, o1)

    t = batch * seq
    x1_f, x1_b = _merge(y_pool.reshape(t, -1), y_lru.reshape(t, -1), g_act, x_rows,
                        w_pool_up_b, w_lru_up_b, w_out_b,
                        b_out, ln1_g, ln1_b, alpha)
    return _mlp(x1_b, x1_f, w_ff1, layer, b_ff1, w_ff2, b_ff2, ln2_g, ln2_b, alpha)


def kernel(x, w_in, pool_w, pool_scale, conv_w, conv_b, lru_wa, lru_ba, lru_wx, lru_bx, lru_lambda, w_pool_up, w_lru_up, w_out, b_out, ln1_g, ln1_b, w_ff1, b_ff1, w_ff2, b_ff2, ln2_g, ln2_b):
    batch, seq, d = x.shape
    depth = w_in.shape[0]
    alpha = (2.0 * depth) ** 0.25
    rows = x.reshape(batch * seq, d)
    for l in range(depth):
        rows = _layer(rows, batch, seq, alpha, l, w_in, pool_w[l], pool_scale[l], conv_w[l], conv_b[l],
                      lru_wa[l], lru_ba[l], lru_wx[l], lru_bx[l], lru_lambda[l], w_pool_up[l], w_lru_up[l],
                      w_out[l], b_out[l], ln1_g[l], ln1_b[l], w_ff1, b_ff1[l], w_ff2, b_ff2[l],
                      ln2_g[l], ln2_b[l])
    return rows.reshape(batch, seq, d)
```

```python
import functools

import jax
import jax.numpy as jnp
from jax import lax
from jax.experimental import pallas as pl
from jax.experimental.pallas import tpu as pltpu

SUBLANES = 8
LANES = 128
NSEG = SUBLANES

POOL_WINDOWS = (2, 4, 8, 16)
LRU_BLOCK = 256
CONV_WIDTH = 4
CONV_LEFT = CONV_WIDTH // 2
LRU_C = 8.0
LN_EPS = 1e-5
HALO = 8
REPLAY_CHUNKS = 2
PERM_ROWS = 32
COPY_ROWS = 512
LOG2_E = 1.4426950408889634
LN_2 = 0.6931471805599453
TINY = 1.1754944e-38

VMEM_LIMIT = 60 * 1024 * 1024

F32 = jnp.float32
BF16 = jnp.bfloat16


def _sigmoid(v):
    return 0.5 * jnp.tanh(0.5 * v) + 0.5


def _gelu_tanh(v):
    c = 0.7978845608028654
    return 0.5 * v * (1.0 + jnp.tanh(c * (v + 0.044715 * (v * v * v))))


def _layer_norm(y, g, b):
    mu = jnp.mean(y, axis=-1, keepdims=True)
    yc = y - mu
    var = jnp.mean(yc * yc, axis=-1, keepdims=True)
    return yc * lax.rsqrt(var + LN_EPS) * g + b


def _mm_act_kernel(*refs, act, n_casts):
    x_ref, w_ref = refs[:2]
    cast_in = refs[2:2 + n_casts]
    o_ref = refs[2 + n_casts]
    cast_out = refs[3 + n_casts:3 + 2 * n_casts]
    w_bf = refs[-1]

    @pl.when(pl.program_id(1) == 0)
    def _():
        w_bf[...] = w_ref[0].astype(BF16)

    acc = jnp.dot(x_ref[...].astype(BF16), w_bf[...], preferred_element_type=F32)
    if act == "gelu":
        acc = _gelu_tanh(acc)
    elif act == "sigmoid":
        acc = _sigmoid(acc)
    o_ref[...] = acc.astype(o_ref.dtype)
    for src, dst in zip(cast_in, cast_out):
        dst[...] = src[...].astype(BF16)


def _mm_act(x, w, layer, col0, n, act, casts=(), tm=512, tn=2048):
    m, k = x.shape
    j0 = col0 // tn
    n_i = m // tm
    steps = (n // tn) * n_i
    cast_specs = [pl.BlockSpec((a.shape[0] // steps, a.shape[1]), lambda j, i: (j * n_i + i, 0)) for a in casts]
    return pl.pallas_call(
        functools.partial(_mm_act_kernel, act=act, n_casts=len(casts)),
        grid=(n // tn, n_i),
        in_specs=[pl.BlockSpec((tm, k), lambda j, i: (i, 0)),
                  pl.BlockSpec((1, k, tn), lambda j, i: (layer, 0, j0 + j))] + cast_specs,
        out_specs=[pl.BlockSpec((tm, tn), lambda j, i: (i, j))] + cast_specs,
        out_shape=[jax.ShapeDtypeStruct((m, n), F32)] + [jax.ShapeDtypeStruct(a.shape, BF16) for a in casts],
        scratch_shapes=[pltpu.VMEM((k, tn), BF16)],
        compiler_params=pltpu.CompilerParams(
            dimension_semantics=("arbitrary", "arbitrary"),
            vmem_limit_bytes=VMEM_LIMIT),
        name=f"mm_in_{act}",
    )(x, w, *casts)


def _load_slabs(buf, r, n):
    return jnp.concatenate([buf[q, pl.ds(r, n), :] for q in range(buf.shape[0])], axis=1)


def _store_slabs(buf, r, n, val):
    for q in range(buf.shape[0]):
        buf[q, pl.ds(r, n), :] = val[:, q * LANES:(q + 1) * LANES]


def _fill_ext(ext, u_ref, seq):
    h_rows = HALO * SUBLANES
    c = u_ref.shape[-1]

    def copy(bi, carry):
        r0 = pl.multiple_of(bi * COPY_ROWS, COPY_ROWS)
        _store_slabs(ext, h_rows + r0, COPY_ROWS, u_ref[0, pl.ds(r0, COPY_ROWS), :])
        return carry

    lax.fori_loop(0, seq // COPY_ROWS, copy, 0)

    sub = lax.broadcasted_iota(jnp.int32, (SUBLANES, c), 0)
    for m in range(HALO):
        nxt = pltpu.roll(u_ref[0, pl.ds(m * SUBLANES, SUBLANES), :], SUBLANES - 1, 0)
        _store_slabs(ext, h_rows + seq + m * SUBLANES, SUBLANES, jnp.where(sub == SUBLANES - 1, 0.0, nxt))
        prv = pltpu.roll(u_ref[0, pl.ds(seq - (m + 1) * SUBLANES, SUBLANES), :], 1, 0)
        _store_slabs(ext, h_rows - (m + 1) * SUBLANES, SUBLANES, jnp.where(sub == 0, 0.0, prv))


def _emit_time_order(src, o_ref, seq, post):
    seg_len = seq // NSEG

    def gather(bi, carry):
        j0 = pl.multiple_of(bi * PERM_ROWS, PERM_ROWS)
        for s in range(NSEG):
            val = jnp.concatenate(
                [src[q, pl.ds(j0 * NSEG + s, PERM_ROWS, stride=NSEG), :] for q in range(src.shape[0])], axis=1)
            r_time = s * seg_len + j0
            o_ref[0, pl.ds(r_time, PERM_ROWS), :] = post(val, r_time).astype(o_ref.dtype)
        return carry

    lax.fori_loop(0, seg_len // PERM_ROWS, gather, 0)


def _pool_kernel(u_ref, w_ref, s_ref, o_ref, ext, *, seq, chunk):
    seg_len = seq // NSEG
    h_rows = HALO * SUBLANES
    nq = ext.shape[0]
    _fill_ext(ext, u_ref, seq)
    grp = pl.program_id(1)
    n_chunks = seq // chunk

    def body(win):
        half = win // 2

        def do_chunk(ci, carry=0):
            clipped = isinstance(ci, int)
            r0 = ci * chunk if clipped else pl.multiple_of(ci * chunk, chunk)
            base = r0 + h_rows
            if clipped:
                row = r0 + lax.broadcasted_iota(jnp.int32, (chunk, LANES), 0)
                t = (row & (SUBLANES - 1)) * seg_len + (row >> 3)
                cnt = jnp.minimum(t + half, seq) - jnp.maximum(t - half, 0)
                inv = 1.0 / cnt.astype(F32)
            else:
                inv = 1.0 / win
            d = []
            for q in range(nq):
                span_rows = (win - 1) * SUBLANES
                rows_in = ext[q, pl.ds(base - half * SUBLANES, chunk + span_rows), :]
                tot = rows_in
                step = SUBLANES
                while step <= half * SUBLANES:
                    keep = tot.shape[0] - step
                    tot = tot[:keep] + tot[step:step + keep]
                    step *= 2
                d.append(tot * inv - rows_in[half * SUBLANES:half * SUBLANES + chunk])
            d = jnp.concatenate(d, axis=1)
            y = jnp.dot(d.astype(BF16), w_ref[0], preferred_element_type=F32) * s_ref[0]
            o_ref[0, pl.ds(r0, chunk), :] = y.astype(o_ref.dtype)
            return carry

        assert chunk // SUBLANES >= half and n_chunks >= 2
        do_chunk(0)
        lax.fori_loop(1, n_chunks - 1, do_chunk, 0)
        do_chunk(n_chunks - 1)

    for gi, win in enumerate(POOL_WINDOWS):
        pl.when(grp == gi)(functools.partial(body, win))


def _pool_branch(z_seg, pool_w, pool_scale, batch, seq, chunk=512):
    n_groups, gw = pool_w.shape[0], pool_w.shape[1]
    z3 = z_seg
    nq = gw // LANES
    return pl.pallas_call(
        functools.partial(_pool_kernel, seq=seq, chunk=chunk),
        grid=(batch, n_groups),
        in_specs=[pl.BlockSpec((1, seq, gw), lambda b, g: (b, 0, g)),
                  pl.BlockSpec((1, gw, gw), lambda b, g: (g, 0, 0)),
                  pl.BlockSpec((1, 1, gw), lambda b, g: (g, 0, 0))],
        out_specs=pl.BlockSpec((1, seq, gw), lambda b, g: (b, 0, g)),
        out_shape=jax.ShapeDtypeStruct((batch, seq, n_groups * gw), BF16),
        scratch_shapes=[pltpu.VMEM((nq, seq + 2 * HALO * SUBLANES, LANES), F32)],
        compiler_params=pltpu.CompilerParams(
            dimension_semantics=("arbitrary", "arbitrary"),
            vmem_limit_bytes=VMEM_LIMIT),
        name="pool_branch",
    )(z3, pool_w, pool_scale.reshape(n_groups, 1, gw))


_P_CONV_W = 0
_P_CONV_B = CONV_WIDTH
_P_DIR = CONV_WIDTH + 1
_P_ROWS = 16


def _lru_kernel(u_ref, gate_ref, w_ref, p_ref, o_ref, ext, a_f, x_f, a_b, x_b, y_buf, summ, ent, *, seq, chunk):
    h_rows = HALO * SUBLANES
    c = LRU_BLOCK
    n_chunks = seq // chunk
    groups = chunk // SUBLANES
    _fill_ext(ext, u_ref, seq)

    conv_hw = [0.5 * p_ref[pl.ds(_P_CONV_W + k, 1), :] for k in range(CONV_WIDTH)]
    conv_hb = 0.5 * p_ref[pl.ds(_P_CONV_B, 1), :]
    hb_a, hb_x, c2 = [], [], []
    for d in range(2):
        hb_a.append(0.5 * p_ref[pl.ds(_P_DIR + 3 * d, 1), :])
        hb_x.append(0.5 * p_ref[pl.ds(_P_DIR + 3 * d + 1, 1), :])
        lam = p_ref[pl.ds(_P_DIR + 3 * d + 2, 1), :]
        c2.append((-0.5 * LRU_C * LOG2_E) * jax.nn.softplus(-lam))
    a_out = (a_f, a_b)
    x_out = (x_f, x_b)

    def rows(v, k):
        return v[k * SUBLANES:(k + 1) * SUBLANES]

    def summary_rows(ci):
        return pl.ds(pl.multiple_of(ci * SUBLANES, SUBLANES), SUBLANES)

    zeros = jnp.zeros((SUBLANES, c), F32)
    ones = jnp.ones((SUBLANES, c), F32)

    def gates(ci, carry):
        r0 = pl.multiple_of(ci * chunk, chunk)
        base = r0 + h_rows
        xh = conv_hb + _load_slabs(ext, base - CONV_LEFT * SUBLANES, chunk) * conv_hw[0]
        for k in range(1, CONV_WIDTH):
            xh = xh + _load_slabs(ext, base + (k - CONV_LEFT) * SUBLANES, chunk) * conv_hw[k]
        pre = jnp.dot(xh.astype(BF16), w_ref[0], preferred_element_type=F32)
        for d in range(2):
            t_r = jnp.tanh(pre[:, (2 * d) * c:(2 * d + 1) * c] + hb_a[d])
            t_i = jnp.tanh(pre[:, (2 * d + 1) * c:(2 * d + 2) * c] + hb_x[d])
            log2_a = c2[d] * t_r + c2[d]
            a = jnp.exp2(log2_a)
            one_m_a2 = (-1.0 - a * a) * jnp.tanh(LN_2 * log2_a)
            root = one_m_a2 * lax.rsqrt(jnp.maximum(one_m_a2, TINY))
            inp = root * (t_i * xh + xh)
            _store_slabs(a_out[d], r0, chunk, a)
            _store_slabs(x_out[d], r0, chunk, inp)
            h, p = zeros, ones
            for k in (range(groups) if d == 0 else range(groups - 1, -1, -1)):
                h = rows(a, k) * h + rows(inp, k)
                p = rows(a, k) * p
            summ[2 * d, summary_rows(ci), :] = h
            summ[2 * d + 1, summary_rows(ci), :] = p
        return carry

    lax.fori_loop(0, n_chunks, gates, 0)

    def summary(idx, ci):
        return summ[idx, ci * SUBLANES:(ci + 1) * SUBLANES, :]

    h_f, p_f, h_b, p_b = zeros, ones, zeros, ones
    for ci in range(n_chunks):
        cb = n_chunks - 1 - ci
        h_f = summary(0, ci) + summary(1, ci) * h_f
        p_f = summary(1, ci) * p_f
        h_b = summary(2, cb) + summary(3, cb) * h_b
        p_b = summary(3, cb) * p_b
    sub = lax.broadcasted_iota(jnp.int32, (SUBLANES, c), 0)
    e_f = zeros
    e_b = zeros
    for _ in range(NSEG - 1):
        e_f = jnp.where(sub == 0, 0.0, pltpu.roll(h_f + p_f * e_f, 1, 0))
        e_b = jnp.where(sub == SUBLANES - 1, 0.0, pltpu.roll(h_b + p_b * e_b, SUBLANES - 1, 0))
    for ci in range(n_chunks):
        cb = n_chunks - 1 - ci
        ent[0, ci * SUBLANES:(ci + 1) * SUBLANES, :] = e_f
        ent[1, cb * SUBLANES:(cb + 1) * SUBLANES, :] = e_b
        e_f = summary(0, ci) + summary(1, ci) * e_f
        e_b = summary(2, cb) + summary(3, cb) * e_b

    def replay(it, carry):
        for sub_i in range(REPLAY_CHUNKS):
            ci = it * REPLAY_CHUNKS + sub_i
            r0 = pl.multiple_of(ci * chunk, chunk)
            af, xf = _load_slabs(a_f, r0, chunk), _load_slabs(x_f, r0, chunk)
            ab, xb = _load_slabs(a_b, r0, chunk), _load_slabs(x_b, r0, chunk)
            h = ent[0, summary_rows(ci), :]
            fwd = []
            for k in range(groups):
                h = rows(af, k) * h + rows(xf, k)
                fwd.append(h)
            h = ent[1, summary_rows(ci), :]
            out = [None] * groups
            for k in range(groups - 1, -1, -1):
                h = rows(ab, k) * h + rows(xb, k)
                out[k] = h + fwd[k]
            _store_slabs(y_buf, r0, chunk, jnp.concatenate(out, axis=0))
        return carry

    lax.fori_loop(0, n_chunks // REPLAY_CHUNKS, replay, 0)

    _emit_time_order(y_buf, o_ref, seq, lambda val, r_time: val * gate_ref[0, pl.ds(r_time, PERM_ROWS), :])


def _lru_branch(z_seg, gate_act, w_gates, params, batch, seq, lru_col0, chunk=512):
    n_heads = w_gates.shape[0]
    c = LRU_BLOCK
    nq = c // LANES
    z3 = z_seg
    g3 = gate_act.reshape(batch, seq, gate_act.shape[1])
    col0 = lru_col0 // c
    n_chunks = seq // chunk
    seg_buf = pltpu.VMEM((nq, seq, LANES), F32)
    return pl.pallas_call(
        functools.partial(_lru_kernel, seq=seq, chunk=chunk),
        grid=(batch, n_heads),
        in_specs=[pl.BlockSpec((1, seq, c), lambda b, h: (b, 0, col0 + h)),
                  pl.BlockSpec((1, seq, c), lambda b, h: (b, 0, h)),
                  pl.BlockSpec((1, c, 4 * c), lambda b, h: (h, 0, 0)),
                  pl.BlockSpec((_P_ROWS, c), lambda b, h: (0, h))],
        out_specs=pl.BlockSpec((1, seq, c), lambda b, h: (b, 0, h)),
        out_shape=jax.ShapeDtypeStruct((batch, seq, n_heads * c), BF16),
        scratch_shapes=[pltpu.VMEM((nq, seq + 2 * HALO * SUBLANES, LANES), F32)]
        + [seg_buf] * 5
        + [pltpu.VMEM((4, n_chunks * SUBLANES, c), F32), pltpu.VMEM((2, n_chunks * SUBLANES, c), F32)],
        compiler_params=pltpu.CompilerParams(
            dimension_semantics=("arbitrary", "arbitrary"),
            vmem_limit_bytes=VMEM_LIMIT),
        name="lru_branch",
    )(z3, g3, w_gates, params)


def _merge_kernel(yp_ref, yl_ref, g0_ref, g1_ref, x_ref, wp_ref, wl_ref, wo_ref, bo_ref, g_ref, b_ref,
                  of_ref, ob_ref, *, alpha):
    up_p = jnp.dot(yp_ref[...], wp_ref[...], preferred_element_type=F32)
    up_l = jnp.dot(yl_ref[...], wl_ref[...], preferred_element_type=F32)
    m = g0_ref[...] * up_p + g1_ref[...] * up_l
    mix = jnp.dot(m.astype(BF16), wo_ref[...], preferred_element_type=F32) + bo_ref[...]
    y = _layer_norm(alpha * x_ref[...] + mix, g_ref[...], b_ref[...])
    of_ref[...] = y
    ob_ref[...] = y.astype(BF16)


def _merge(y_pool, y_lru, g_act, x_rows, w_pool_up, w_lru_up, w_out, b_out, ln_g, ln_b, alpha, tm=256):
    t, d = x_rows.shape
    row = lambda i: (i, 0)
    const = lambda i: (0, 0)
    wspec = pl.BlockSpec((d, d), const, pipeline_mode=pl.Buffered(1))
    vspec = pl.BlockSpec((1, d), const)
    return pl.pallas_call(
        functools.partial(_merge_kernel, alpha=alpha),
        grid=(t // tm,),
        in_specs=[pl.BlockSpec((tm, d), row), pl.BlockSpec((tm, d), row),
                  pl.BlockSpec((tm, d), lambda i: (i, 0)), pl.BlockSpec((tm, d), lambda i: (i, 1)),
                  pl.BlockSpec((tm, d), row), wspec, wspec, wspec, vspec, vspec, vspec],
        out_specs=[pl.BlockSpec((tm, d), row), pl.BlockSpec((tm, d), row)],
        out_shape=[jax.ShapeDtypeStruct((t, d), F32), jax.ShapeDtypeStruct((t, d), BF16)],
        compiler_params=pltpu.CompilerParams(
            dimension_semantics=("arbitrary",), vmem_limit_bytes=VMEM_LIMIT),
        name="merge_out_ln",
    )(y_pool, y_lru, g_act, g_act, x_rows, w_pool_up, w_lru_up, w_out,
      b_out.reshape(1, d), ln_g.reshape(1, d), ln_b.reshape(1, d))


def _mlp_kernel(xb_ref, xf_ref, w1_ref, b1_ref, w2_ref, b2_ref, g_ref, b_ref, o_ref, *, alpha):
    f = pl.program_id(1)

    @pl.when(f == 0)
    def _():
        o_ref[...] = alpha * xf_ref[...] + b2_ref[...]

    h = jnp.dot(xb_ref[...], w1_ref[0].astype(BF16), preferred_element_type=F32) + b1_ref[...]
    h = jnp.square(jnp.maximum(h, 0.0))
    o_ref[...] += jnp.dot(h.astype(BF16), w2_ref[0].astype(BF16), preferred_element_type=F32)

    @pl.when(f == pl.num_programs(1) - 1)
    def _():
        o_ref[...] = _layer_norm(o_ref[...], g_ref[...], b_ref[...])


def _mlp(x_b, x_f, w1, layer, b1, w2, b2, ln_g, ln_b, alpha, tm=1024, tf=512):
    t, d = x_f.shape
    dff = w1.shape[2]
    return pl.pallas_call(
        functools.partial(_mlp_kernel, alpha=alpha),
        grid=(t // tm, dff // tf),
        in_specs=[pl.BlockSpec((tm, d), lambda i, f: (i, 0)),
                  pl.BlockSpec((tm, d), lambda i, f: (i, 0), pipeline_mode=pl.Buffered(1)),
                  pl.BlockSpec((1, d, tf), lambda i, f: (layer, 0, f)),
                  pl.BlockSpec((1, tf), lambda i, f: (0, f)),
                  pl.BlockSpec((1, tf, d), lambda i, f: (layer, f, 0)),
                  pl.BlockSpec((1, d), lambda i, f: (0, 0)),
                  pl.BlockSpec((1, d), lambda i, f: (0, 0)),
                  pl.BlockSpec((1, d), lambda i, f: (0, 0))],
        out_specs=pl.BlockSpec((tm, d), lambda i, f: (i, 0)),
        out_shape=jax.ShapeDtypeStruct((t, d), F32),
        compiler_params=pltpu.CompilerParams(
            dimension_semantics=("arbitrary", "arbitrary"), vmem_limit_bytes=VMEM_LIMIT),
        name="mlp_ln",
    )(x_b, x_f, w1, b1.reshape(1, dff), w2, b2.reshape(1, d), ln_g.reshape(1, d), ln_b.reshape(1, d))


def _layer(x_rows, batch, seq, alpha, layer, w_in, pool_w, pool_scale, conv_w, conv_b, lru_wa, lru_ba, lru_wx,
           lru_bx, lru_lambda, w_pool_up, w_lru_up, w_out, b_out, ln1_g, ln1_b, w_ff1, b_ff1, w_ff2, b_ff2,
           ln2_g, ln2_b):
    pool_width = pool_w.shape[0] * pool_w.shape[1]
    lru_width = conv_w.shape[1]
    o1, o2, o3 = pool_width, pool_width + lru_width, pool_width + 2 * lru_width

    z_a, = _mm_act(x_rows, w_in, layer, 0, o2, "none")
    g_act, = _mm_act(x_rows, w_in, layer, o3, w_in.shape[2] - o3, "sigmoid")
    gate_act, w_pool_up_b, w_lru_up_b, w_out_b = _mm_act(
        x_rows, w_in, layer, o2, o3 - o2, "gelu", casts=(w_pool_up, w_lru_up, w_out))

    seg_len = seq // NSEG
    z_seg = z_a.reshape(batch, NSEG, seg_len, -1).transpose(0, 2, 1, 3).reshape(batch, seq, -1)
    y_pool = _pool_branch(z_seg, pool_w.astype(BF16), pool_scale, batch, seq)
    y_pool = y_pool.reshape(batch, seg_len, NSEG, -1).transpose(0, 2, 1, 3)

    w_gates = jnp.concatenate([lru_wa[0], lru_wx[0], lru_wa[1], lru_wx[1]], axis=-1).astype(BF16)
    params = jnp.concatenate(
        [conv_w, conv_b[None], lru_ba[0][None], lru_bx[0][None], lru_lambda[0][None],
         lru_ba[1][None], lru_bx[1][None], lru_lambda[1][None]], axis=0).astype(F32)
    params = jnp.pad(params, ((0, _P_ROWS - params.shape[0]), (0, 0)))
    y_lru = _lru_branch(z_seg, gate_act, w_gates, params, batch, seq, o1)

    t = batch * seq
    x1_f, x1_b = _merge(y_pool.reshape(t, -1), y_lru.reshape(t, -1), g_act, x_rows,
                        w_pool_up_b, w_lru_up_b, w_out_b,
                        b_out, ln1_g, ln1_b, alpha)
    return _mlp(x1_b, x1_f, w_ff1, layer, b_ff1, w_ff2, b_ff2, ln2_g, ln2_b, alpha)


def kernel(x, w_in, pool_w, pool_scale, conv_w, conv_b, lru_wa, lru_ba, lru_wx, lru_bx, lru_lambda, w_pool_up, w_lru_up, w_out, b_out, ln1_g, ln1_b, w_ff1, b_ff1, w_ff2, b_ff2, ln2_g, ln2_b):
    batch, seq, d = x.shape
    depth = w_in.shape[0]
    alpha = (2.0 * depth) ** 0.25
    rows = x.reshape(batch * seq, d)
    for l in range(depth):
        rows = _layer(rows, batch, seq, alpha, l, w_in, pool_w[l], pool_scale[l], conv_w[l], conv_b[l],
                      lru_wa[l], lru_ba[l], lru_wx[l], lru_bx[l], lru_lambda[l], w_pool_up[l], w_lru_up[l],
                      w_out[l], b_out[l], ln1_g[l], ln1_b[l], w_ff1, b_ff1[l], w_ff2, b_ff2[l],
                      ln2_g[l], ln2_b[l])
    return rows.reshape(batch, seq, d)
```

```python
import functools

import jax
import jax.numpy as jnp
from jax import lax
from jax.experimental import pallas as pl
from jax.experimental.pallas import tpu as pltpu

SUBLANES = 8
LANES = 128
NSEG = SUBLANES

POOL_WINDOWS = (2, 4, 8, 16)
LRU_BLOCK = 256
CONV_WIDTH = 4
CONV_LEFT = CONV_WIDTH // 2
LRU_C = 8.0
LN_EPS = 1e-5
HALO = 8
REPLAY_CHUNKS = 2
PERM_ROWS = 32
LOG2_E = 1.4426950408889634
LN_2 = 0.6931471805599453
TINY = 1.1754944e-38

VMEM_LIMIT = 60 * 1024 * 1024

F32 = jnp.float32
BF16 = jnp.bfloat16


def _sigmoid(v):
    return 0.5 * jnp.tanh(0.5 * v) + 0.5


def _gelu_tanh(v):
    c = 0.7978845608028654
    return 0.5 * v * (1.0 + jnp.tanh(c * (v + 0.044715 * (v * v * v))))


def _layer_norm(y, g, b):
    mu = jnp.mean(y, axis=-1, keepdims=True)
    yc = y - mu
    var = jnp.mean(yc * yc, axis=-1, keepdims=True)
    return yc * lax.rsqrt(var + LN_EPS) * g + b


def _mm_act_kernel(*refs, act, n_casts):
    x_ref, w_ref = refs[:2]
    cast_in = refs[2:2 + n_casts]
    o_ref = refs[2 + n_casts]
    cast_out = refs[3 + n_casts:3 + 2 * n_casts]
    w_bf = refs[-1]

    @pl.when(pl.program_id(1) == 0)
    def _():
        w_bf[...] = w_ref[0].astype(BF16)

    acc = jnp.dot(x_ref[...].astype(BF16), w_bf[...], preferred_element_type=F32)
    if act == "gelu":
        acc = _gelu_tanh(acc)
    elif act == "sigmoid":
        acc = _sigmoid(acc)
    o_ref[...] = acc.astype(o_ref.dtype)
    for src, dst in zip(cast_in, cast_out):
        dst[...] = src[...].astype(BF16)


def _mm_act(x, w, layer, col0, n, act, casts=(), tm=512, tn=2048):
    m, k = x.shape
    j0 = col0 // tn
    n_i = m // tm
    steps = (n // tn) * n_i
    cast_specs = [pl.BlockSpec((a.shape[0] // steps, a.shape[1]), lambda j, i: (j * n_i + i, 0)) for a in casts]
    return pl.pallas_call(
        functools.partial(_mm_act_kernel, act=act, n_casts=len(casts)),
        grid=(n // tn, n_i),
        in_specs=[pl.BlockSpec((tm, k), lambda j, i: (i, 0)),
                  pl.BlockSpec((1, k, tn), lambda j, i: (layer, 0, j0 + j))] + cast_specs,
        out_specs=[pl.BlockSpec((tm, tn), lambda j, i: (i, j))] + cast_specs,
        out_shape=[jax.ShapeDtypeStruct((m, n), F32)] + [jax.ShapeDtypeStruct(a.shape, BF16) for a in casts],
        scratch_shapes=[pltpu.VMEM((k, tn), BF16)],
        compiler_params=pltpu.CompilerParams(
            dimension_semantics=("arbitrary", "arbitrary"),
            vmem_limit_bytes=VMEM_LIMIT),
        name=f"mm_in_{act}",
    )(x, w, *casts)


def _load_slabs(buf, r, n):
    return jnp.concatenate([buf[q, pl.ds(r, n), :] for q in range(buf.shape[0])], axis=1)


def _store_slabs(buf, r, n, val):
    for q in range(buf.shape[0]):
        buf[q, pl.ds(r, n), :] = val[:, q * LANES:(q + 1) * LANES]


def _fill_ext(ext, u_ref, seq):
    seg_len = seq // NSEG
    h_rows = HALO * SUBLANES
    nq = ext.shape[0]

    def scatter(bi, carry):
        j0 = pl.multiple_of(bi * PERM_ROWS, PERM_ROWS)
        for s in range(NSEG):
            val = u_ref[0, pl.ds(s * seg_len + j0, PERM_ROWS), :]
            for q in range(nq):
                ext[q, pl.ds(h_rows + j0 * NSEG + s, PERM_ROWS, stride=NSEG), :] = val[:, q * LANES:(q + 1) * LANES]
        return carry

    lax.fori_loop(0, seg_len // PERM_ROWS, scatter, 0)

    zero = jnp.zeros((HALO, LANES), F32)
    for s in range(NSEG):
        nxt = u_ref[0, pl.ds((s + 1) * seg_len, HALO), :] if s + 1 < NSEG else None
        prv = u_ref[0, pl.ds(s * seg_len - HALO, HALO), :] if s > 0 else None
        for q in range(nq):
            lanes = slice(q * LANES, (q + 1) * LANES)
            ext[q, pl.ds(h_rows + seq + s, HALO, stride=NSEG), :] = zero if nxt is None else nxt[:, lanes]
            ext[q, pl.ds(s, HALO, stride=NSEG), :] = zero if prv is None else prv[:, lanes]


def _emit_time_order(src, o_ref, seq, post):
    seg_len = seq // NSEG

    def gather(bi, carry):
        j0 = pl.multiple_of(bi * PERM_ROWS, PERM_ROWS)
        for s in range(NSEG):
            val = jnp.concatenate(
                [src[q, pl.ds(j0 * NSEG + s, PERM_ROWS, stride=NSEG), :] for q in range(src.shape[0])], axis=1)
            r_time = s * seg_len + j0
            o_ref[0, pl.ds(r_time, PERM_ROWS), :] = post(val, r_time).astype(o_ref.dtype)
        return carry

    lax.fori_loop(0, seg_len // PERM_ROWS, gather, 0)


def _pool_kernel(u_ref, w_ref, s_ref, o_ref, ext, *, seq, chunk):
    seg_len = seq // NSEG
    h_rows = HALO * SUBLANES
    nq = ext.shape[0]
    _fill_ext(ext, u_ref, seq)
    grp = pl.program_id(1)
    n_chunks = seq // chunk

    def body(win):
        half = win // 2

        def do_chunk(ci, carry=0):
            clipped = isinstance(ci, int)
            r0 = ci * chunk if clipped else pl.multiple_of(ci * chunk, chunk)
            base = r0 + h_rows
            if clipped:
                row = r0 + lax.broadcasted_iota(jnp.int32, (chunk, LANES), 0)
                t = (row & (SUBLANES - 1)) * seg_len + (row >> 3)
                cnt = jnp.minimum(t + half, seq) - jnp.maximum(t - half, 0)
                inv = 1.0 / cnt.astype(F32)
            else:
                inv = 1.0 / win
            d = []
            for q in range(nq):
                span_rows = (win - 1) * SUBLANES
                rows_in = ext[q, pl.ds(base - half * SUBLANES, chunk + span_rows), :]
                tot = rows_in
                step = SUBLANES
                while step <= half * SUBLANES:
                    keep = tot.shape[0] - step
                    tot = tot[:keep] + tot[step:step + keep]
                    step *= 2
                d.append(tot * inv - rows_in[half * SUBLANES:half * SUBLANES + chunk])
            d = jnp.concatenate(d, axis=1)
            y = jnp.dot(d.astype(BF16), w_ref[0], preferred_element_type=F32) * s_ref[0]
            o_ref[0, pl.ds(r0, chunk), :] = y.astype(o_ref.dtype)
            return carry

        assert chunk // SUBLANES >= half and n_chunks >= 2
        do_chunk(0)
        lax.fori_loop(1, n_chunks - 1, do_chunk, 0)
        do_chunk(n_chunks - 1)

    for gi, win in enumerate(POOL_WINDOWS):
        pl.when(grp == gi)(functools.partial(body, win))


def _pool_branch(z_a, pool_w, pool_scale, batch, seq, chunk=512):
    n_groups, gw = pool_w.shape[0], pool_w.shape[1]
    z3 = z_a.reshape(batch, seq, z_a.shape[1])
    nq = gw // LANES
    return pl.pallas_call(
        functools.partial(_pool_kernel, seq=seq, chunk=chunk),
        grid=(batch, n_groups),
        in_specs=[pl.BlockSpec((1, seq, gw), lambda b, g: (b, 0, g)),
                  pl.BlockSpec((1, gw, gw), lambda b, g: (g, 0, 0)),
                  pl.BlockSpec((1, 1, gw), lambda b, g: (g, 0, 0))],
        out_specs=pl.BlockSpec((1, seq, gw), lambda b, g: (b, 0, g)),
        out_shape=jax.ShapeDtypeStruct((batch, seq, n_groups * gw), BF16),
        scratch_shapes=[pltpu.VMEM((nq, seq + 2 * HALO * SUBLANES, LANES), F32)],
        compiler_params=pltpu.CompilerParams(
            dimension_semantics=("arbitrary", "arbitrary"),
            vmem_limit_bytes=VMEM_LIMIT),
        name="pool_branch",
    )(z3, pool_w, pool_scale.reshape(n_groups, 1, gw))


_P_CONV_W = 0
_P_CONV_B = CONV_WIDTH
_P_DIR = CONV_WIDTH + 1
_P_ROWS = 16


def _lru_kernel(u_ref, gate_ref, w_ref, p_ref, o_ref, ext, a_f, x_f, a_b, x_b, y_buf, summ, ent, *, seq, chunk):
    h_rows = HALO * SUBLANES
    c = LRU_BLOCK
    n_chunks = seq // chunk
    groups = chunk // SUBLANES
    _fill_ext(ext, u_ref, seq)

    conv_hw = [0.5 * p_ref[pl.ds(_P_CONV_W + k, 1), :] for k in range(CONV_WIDTH)]
    conv_hb = 0.5 * p_ref[pl.ds(_P_CONV_B, 1), :]
    hb_a, hb_x, c2 = [], [], []
    for d in range(2):
        hb_a.append(0.5 * p_ref[pl.ds(_P_DIR + 3 * d, 1), :])
        hb_x.append(0.5 * p_ref[pl.ds(_P_DIR + 3 * d + 1, 1), :])
        lam = p_ref[pl.ds(_P_DIR + 3 * d + 2, 1), :]
        c2.append((-0.5 * LRU_C * LOG2_E) * jax.nn.softplus(-lam))
    a_out = (a_f, a_b)
    x_out = (x_f, x_b)

    def rows(v, k):
        return v[k * SUBLANES:(k + 1) * SUBLANES]

    def summary_rows(ci):
        return pl.ds(pl.multiple_of(ci * SUBLANES, SUBLANES), SUBLANES)

    zeros = jnp.zeros((SUBLANES, c), F32)
    ones = jnp.ones((SUBLANES, c), F32)

    def gates(ci, carry):
        r0 = pl.multiple_of(ci * chunk, chunk)
        base = r0 + h_rows
        xh = conv_hb + _load_slabs(ext, base - CONV_LEFT * SUBLANES, chunk) * conv_hw[0]
        for k in range(1, CONV_WIDTH):
            xh = xh + _load_slabs(ext, base + (k - CONV_LEFT) * SUBLANES, chunk) * conv_hw[k]
        pre = jnp.dot(xh.astype(BF16), w_ref[0], preferred_element_type=F32)
        for d in range(2):
            t_r = jnp.tanh(pre[:, (2 * d) * c:(2 * d + 1) * c] + hb_a[d])
            t_i = jnp.tanh(pre[:, (2 * d + 1) * c:(2 * d + 2) * c] + hb_x[d])
            log2_a = c2[d] * t_r + c2[d]
            a = jnp.exp2(log2_a)
            one_m_a2 = (-1.0 - a * a) * jnp.tanh(LN_2 * log2_a)
            root = one_m_a2 * lax.rsqrt(jnp.maximum(one_m_a2, TINY))
            inp = root * (t_i * xh + xh)
            _store_slabs(a_out[d], r0, chunk, a)
            _store_slabs(x_out[d], r0, chunk, inp)
            h, p = zeros, ones
            for k in (range(groups) if d == 0 else range(groups - 1, -1, -1)):
                h = rows(a, k) * h + rows(inp, k)
                p = rows(a, k) * p
            summ[2 * d, summary_rows(ci), :] = h
            summ[2 * d + 1, summary_rows(ci), :] = p
        return carry

    lax.fori_loop(0, n_chunks, gates, 0)

    def summary(idx, ci):
        return summ[idx, ci * SUBLANES:(ci + 1) * SUBLANES, :]

    h_f, p_f, h_b, p_b = zeros, ones, zeros, ones
    for ci in range(n_chunks):
        cb = n_chunks - 1 - ci
        h_f = summary(0, ci) + summary(1, ci) * h_f
        p_f = summary(1, ci) * p_f
        h_b = summary(2, cb) + summary(3, cb) * h_b
        p_b = summary(3, cb) * p_b
    sub = lax.broadcasted_iota(jnp.int32, (SUBLANES, c), 0)
    e_f = zeros
    e_b = zeros
    for _ in range(NSEG - 1):
        e_f = jnp.where(sub == 0, 0.0, pltpu.roll(h_f + p_f * e_f, 1, 0))
        e_b = jnp.where(sub == SUBLANES - 1, 0.0, pltpu.roll(h_b + p_b * e_b, SUBLANES - 1, 0))
    for ci in range(n_chunks):
        cb = n_chunks - 1 - ci
        ent[0, ci * SUBLANES:(ci + 1) * SUBLANES, :] = e_f
        ent[1, cb * SUBLANES:(cb + 1) * SUBLANES, :] = e_b
        e_f = summary(0, ci) + summary(1, ci) * e_f
        e_b = summary(2, cb) + summary(3, cb) * e_b

    def replay(it, carry):
        for sub_i in range(REPLAY_CHUNKS):
            ci = it * REPLAY_CHUNKS + sub_i
            r0 = pl.multiple_of(ci * chunk, chunk)
            af, xf = _load_slabs(a_f, r0, chunk), _load_slabs(x_f, r0, chunk)
            ab, xb = _load_slabs(a_b, r0, chunk), _load_slabs(x_b, r0, chunk)
            h = ent[0, summary_rows(ci), :]
            fwd = []
            for k in range(groups):
                h = rows(af, k) * h + rows(xf, k)
                fwd.append(h)
            h = ent[1, summary_rows(ci), :]
            out = [None] * groups
            for k in range(groups - 1, -1, -1):
                h = rows(ab, k) * h + rows(xb, k)
                out[k] = h + fwd[k]
            _store_slabs(y_buf, r0, chunk, jnp.concatenate(out, axis=0))
        return carry

    lax.fori_loop(0, n_chunks // REPLAY_CHUNKS, replay, 0)

    _emit_time_order(y_buf, o_ref, seq, lambda val, r_time: val * gate_ref[0, pl.ds(r_time, PERM_ROWS), :])


def _lru_branch(z_a, gate_act, w_gates, params, batch, seq, lru_col0, chunk=512):
    n_heads = w_gates.shape[0]
    c = LRU_BLOCK
    nq = c // LANES
    z3 = z_a.reshape(batch, seq, z_a.shape[1])
    g3 = gate_act.reshape(batch, seq, gate_act.shape[1])
    col0 = lru_col0 // c
    n_chunks = seq // chunk
    seg_buf = pltpu.VMEM((nq, seq, LANES), F32)
    return pl.pallas_call(
        functools.partial(_lru_kernel, seq=seq, chunk=chunk),
        grid=(batch, n_heads),
        in_specs=[pl.BlockSpec((1, seq, c), lambda b, h: (b, 0, col0 + h)),
                  pl.BlockSpec((1, seq, c), lambda b, h: (b, 0, h)),
                  pl.BlockSpec((1, c, 4 * c), lambda b, h: (h, 0, 0)),
                  pl.BlockSpec((_P_ROWS, c), lambda b, h: (0, h))],
        out_specs=pl.BlockSpec((1, seq, c), lambda b, h: (b, 0, h)),
        out_shape=jax.ShapeDtypeStruct((batch, seq, n_heads * c), BF16),
        scratch_shapes=[pltpu.VMEM((nq, seq + 2 * HALO * SUBLANES, LANES), F32)]
        + [seg_buf] * 5
        + [pltpu.VMEM((4, n_chunks * SUBLANES, c), F32), pltpu.VMEM((2, n_chunks * SUBLANES, c), F32)],
        compiler_params=pltpu.CompilerParams(
            dimension_semantics=("arbitrary", "arbitrary"),
            vmem_limit_bytes=VMEM_LIMIT),
        name="lru_branch",
    )(z3, g3, w_gates, params)


def _merge_kernel(yp_ref, yl_ref, g0_ref, g1_ref, x_ref, wp_ref, wl_ref, wo_ref, bo_ref, g_ref, b_ref,
                  of_ref, ob_ref, *, alpha):
    up_p = jnp.dot(yp_ref[...], wp_ref[...], preferred_element_type=F32)
    up_l = jnp.dot(yl_ref[...], wl_ref[...], preferred_element_type=F32)
    m = g0_ref[...] * up_p + g1_ref[...] * up_l
    mix = jnp.dot(m.astype(BF16), wo_ref[...], preferred_element_type=F32) + bo_ref[...]
    y = _layer_norm(alpha * x_ref[...] + mix, g_ref[...], b_ref[...])
    of_ref[...] = y
    ob_ref[...] = y.astype(BF16)


def _merge(y_pool, y_lru, g_act, x_rows, w_pool_up, w_lru_up, w_out, b_out, ln_g, ln_b, alpha, tm=256):
    t, d = x_rows.shape
    row = lambda i: (i, 0)
    const = lambda i: (0, 0)
    wspec = pl.BlockSpec((d, d), const, pipeline_mode=pl.Buffered(1))
    vspec = pl.BlockSpec((1, d), const)
    return pl.pallas_call(
        functools.partial(_merge_kernel, alpha=alpha),
        grid=(t // tm,),
        in_specs=[pl.BlockSpec((tm, d), row), pl.BlockSpec((tm, d), row),
                  pl.BlockSpec((tm, d), lambda i: (i, 0)), pl.BlockSpec((tm, d), lambda i: (i, 1)),
                  pl.BlockSpec((tm, d), row), wspec, wspec, wspec, vspec, vspec, vspec],
        out_specs=[pl.BlockSpec((tm, d), row), pl.BlockSpec((tm, d), row)],
        out_shape=[jax.ShapeDtypeStruct((t, d), F32), jax.ShapeDtypeStruct((t, d), BF16)],
        compiler_params=pltpu.CompilerParams(
            dimension_semantics=("arbitrary",), vmem_limit_bytes=VMEM_LIMIT),
        name="merge_out_ln",
    )(y_pool, y_lru, g_act, g_act, x_rows, w_pool_up, w_lru_up, w_out,
      b_out.reshape(1, d), ln_g.reshape(1, d), ln_b.reshape(1, d))


def _mlp_kernel(xb_ref, xf_ref, w1_ref, b1_ref, w2_ref, b2_ref, g_ref, b_ref, o_ref, *, alpha):
    f = pl.program_id(1)

    @pl.when(f == 0)
    def _():
        o_ref[...] = alpha * xf_ref[...] + b2_ref[...]

    h = jnp.dot(xb_ref[...], w1_ref[0].astype(BF16), preferred_element_type=F32) + b1_ref[...]
    h = jnp.square(jnp.maximum(h, 0.0))
    o_ref[...] += jnp.dot(h.astype(BF16), w2_ref[0].astype(BF16), preferred_element_type=F32)

    @pl.when(f == pl.num_programs(1) - 1)
    def _():
        o_ref[...] = _layer_norm(o_ref[...], g_ref[...], b_ref[...])


def _mlp(x_b, x_f, w1, layer, b1, w2, b2, ln_g, ln_b, alpha, tm=1024, tf=512):
    t, d = x_f.shape
    dff = w1.shape[2]
    return pl.pallas_call(
        functools.partial(_mlp_kernel, alpha=alpha),
        grid=(t // tm, dff // tf),
        in_specs=[pl.BlockSpec((tm, d), lambda i, f: (i, 0)),
                  pl.BlockSpec((tm, d), lambda i, f: (i, 0), pipeline_mode=pl.Buffered(1)),
                  pl.BlockSpec((1, d, tf), lambda i, f: (layer, 0, f)),
                  pl.BlockSpec((1, tf), lambda i, f: (0, f)),
                  pl.BlockSpec((1, tf, d), lambda i, f: (layer, f, 0)),
                  pl.BlockSpec((1, d), lambda i, f: (0, 0)),
                  pl.BlockSpec((1, d), lambda i, f: (0, 0)),
                  pl.BlockSpec((1, d), lambda i, f: (0, 0))],
        out_specs=pl.BlockSpec((tm, d), lambda i, f: (i, 0)),
        out_shape=jax.ShapeDtypeStruct((t, d), F32),
        compiler_params=pltpu.CompilerParams(
            dimension_semantics=("arbitrary", "arbitrary"), vmem_limit_bytes=VMEM_LIMIT),
        name="mlp_ln",
    )(x_b, x_f, w1, b1.reshape(1, dff), w2, b2.reshape(1, d), ln_g.reshape(1, d), ln_b.reshape(1, d))


def _layer(x_rows, batch, seq, alpha, layer, w_in, pool_w, pool_scale, conv_w, conv_b, lru_wa, lru_ba, lru_wx,
           lru_bx, lru_lambda, w_pool_up, w_lru_up, w_out, b_out, ln1_g, ln1_b, w_ff1, b_ff1, w_ff2, b_ff2,
           ln2_g, ln2_b):
    pool_width = pool_w.shape[0] * pool_w.shape[1]
    lru_width = conv_w.shape[1]
    o1, o2, o3 = pool_width, pool_width + lru_width, pool_width + 2 * lru_width

    z_a, = _mm_act(x_rows, w_in, layer, 0, o2, "none")
    g_act, = _mm_act(x_rows, w_in, layer, o3, w_in.shape[2] - o3, "sigmoid")
    gate_act, w_pool_up_b, w_lru_up_b, w_out_b = _mm_act(
        x_rows, w_in, layer, o2, o3 - o2, "gelu", casts=(w_pool_up, w_lru_up, w_out))

    y_pool = _pool_branch(z_a, pool_w.astype(BF16), pool_scale, batch, seq)
    y_pool = y_pool.reshape(batch, seq // NSEG, NSEG, -1).transpose(0, 2, 1, 3)

    w_gates = jnp.concatenate([lru_wa[0], lru_wx[0], lru_wa[1], lru_wx[1]], axis=-1).astype(BF16)
    params = jnp.concatenate(
        [conv_w, conv_b[None], lru_ba[0][None], lru_bx[0][None], lru_lambda[0][None],
         lru_ba[1][None], lru_bx[1][None], lru_lambda[1][None]], axis=0).astype(F32)
    params = jnp.pad(params, ((0, _P_ROWS - params.shape[0]), (0, 0)))
    y_lru = _lru_branch(z_a, gate_act, w_gates, params, batch, seq, o1)

    t = batch * seq
    x1_f, x1_b = _merge(y_pool.reshape(t, -1), y_lru.reshape(t, -1), g_act, x_rows,
                        w_pool_up_b, w_lru_up_b, w_out_b,
                        b_out, ln1_g, ln1_b, alpha)
    return _mlp(x1_b, x1_f, w_ff1, layer, b_ff1, w_ff2, b_ff2, ln2_g, ln2_b, alpha)


def kernel(x, w_in, pool_w, pool_scale, conv_w, conv_b, lru_wa, lru_ba, lru_wx, lru_bx, lru_lambda, w_pool_up, w_lru_up, w_out, b_out, ln1_g, ln1_b, w_ff1, b_ff1, w_ff2, b_ff2, ln2_g, ln2_b):
    batch, seq, d = x.shape
    depth = w_in.shape[0]
    alpha = (2.0 * depth) ** 0.25
    rows = x.reshape(batch * seq, d)
    for l in range(depth):
        rows = _layer(rows, batch, seq, alpha, l, w_in, pool_w[l], pool_scale[l], conv_w[l], conv_b[l],
                      lru_wa[l], lru_ba[l], lru_wx[l], lru_bx[l], lru_lambda[l], w_pool_up[l], w_lru_up[l],
                      w_out[l], b_out[l], ln1_g[l], ln1_b[l], w_ff1, b_ff1[l], w_ff2, b_ff2[l],
                      ln2_g[l], ln2_b[l])
    return rows.reshape(batch, seq, d)
```

```python
import functools

import jax
import jax.numpy as jnp
from jax import lax
from jax.experimental import pallas as pl
from jax.experimental.pallas import tpu as pltpu

SUBLANES = 8
LANES = 128
NSEG = SUBLANES

POOL_WINDOWS = (2, 4, 8, 16)
LRU_BLOCK = 256
CONV_WIDTH = 4
CONV_LEFT = CONV_WIDTH // 2
LRU_C = 8.0
LN_EPS = 1e-5
HALO = 8
REPLAY_CHUNKS = 2
PERM_ROWS = 32
COPY_ROWS = 512
LOG2_E = 1.4426950408889634
LN_2 = 0.6931471805599453
TINY = 1.1754944e-38

VMEM_LIMIT = 60 * 1024 * 1024

F32 = jnp.float32
BF16 = jnp.bfloat16


def _sigmoid(v):
    return 0.5 * jnp.tanh(0.5 * v) + 0.5


def _gelu_tanh(v):
    c = 0.7978845608028654
    return 0.5 * v * (1.0 + jnp.tanh(c * (v + 0.044715 * (v * v * v))))


def _layer_norm(y, g, b):
    mu = jnp.mean(y, axis=-1, keepdims=True)
    yc = y - mu
    var = jnp.mean(yc * yc, axis=-1, keepdims=True)
    return yc * lax.rsqrt(var + LN_EPS) * g + b


def _mm_act_kernel(*refs, act, n_casts, seg):
    x_ref, w_ref = refs[:2]
    cast_in = refs[2:2 + n_casts]
    o_ref = refs[2 + n_casts]
    cast_out = refs[3 + n_casts:3 + 2 * n_casts]
    w_bf = refs[-1]

    @pl.when(pl.program_id(1) == 0)
    def _():
        w_bf[...] = w_ref[0].astype(BF16)

    if seg:
        tj = x_ref.shape[2]
        x_tile = x_ref[0].reshape(NSEG * tj, x_ref.shape[3])
    else:
        x_tile = x_ref[...]
    acc = jnp.dot(x_tile.astype(BF16), w_bf[...], preferred_element_type=F32)
    if act == "gelu":
        acc = _gelu_tanh(acc)
    elif act == "sigmoid":
        acc = _sigmoid(acc)
    if seg:
        for q in range(o_ref.shape[0]):
            for s in range(NSEG):
                o_ref[q, pl.ds(s, tj, stride=NSEG), :] = acc[s * tj:(s + 1) * tj, q * LANES:(q + 1) * LANES]
    else:
        o_ref[...] = acc.astype(o_ref.dtype)
    for src, dst in zip(cast_in, cast_out):
        dst[...] = src[...].astype(BF16)


def _mm_act(x, w, layer, col0, n, act, casts=(), seg=None, tm=512, tn=2048):
    m, k = x.shape
    j0 = col0 // tn
    n_i = m // tm
    steps = (n // tn) * n_i
    cast_specs = [pl.BlockSpec((a.shape[0] // steps, a.shape[1]), lambda j, i: (j * n_i + i, 0)) for a in casts]
    if seg:
        batch, seq = seg
        tj = tm // NSEG
        n_jb = seq // NSEG // tj
        x = x.reshape(batch, NSEG, seq // NSEG, k)
        x_spec = pl.BlockSpec((1, NSEG, tj, k), lambda j, i: (i // n_jb, 0, i % n_jb, 0))
        o_spec = pl.BlockSpec((tn // LANES, tm, LANES), lambda j, i: (j, i, 0))
        o_shape = jax.ShapeDtypeStruct((n // LANES, m, LANES), F32)
    else:
        x_spec = pl.BlockSpec((tm, k), lambda j, i: (i, 0))
        o_spec = pl.BlockSpec((tm, tn), lambda j, i: (i, j))
        o_shape = jax.ShapeDtypeStruct((m, n), F32)
    return pl.pallas_call(
        functools.partial(_mm_act_kernel, act=act, n_casts=len(casts), seg=bool(seg)),
        grid=(n // tn, n_i),
        in_specs=[x_spec, pl.BlockSpec((1, k, tn), lambda j, i: (layer, 0, j0 + j))] + cast_specs,
        out_specs=[o_spec] + cast_specs,
        out_shape=[o_shape] + [jax.ShapeDtypeStruct(a.shape, BF16) for a in casts],
        scratch_shapes=[pltpu.VMEM((k, tn), BF16)],
        compiler_params=pltpu.CompilerParams(
            dimension_semantics=("arbitrary", "arbitrary"),
            vmem_limit_bytes=VMEM_LIMIT),
        name=f"mm_in_{act}",
    )(x, w, *casts)


def _load_slabs(buf, r, n):
    return jnp.concatenate([buf[q, pl.ds(r, n), :] for q in range(buf.shape[0])], axis=1)


def _store_slabs(buf, r, n, val):
    for q in range(buf.shape[0]):
        buf[q, pl.ds(r, n), :] = val[:, q * LANES:(q + 1) * LANES]


def _fill_ext(ext, u_ref, seq):
    h_rows = HALO * SUBLANES
    nq = ext.shape[0]

    def copy(bi, carry):
        r0 = pl.multiple_of(bi * COPY_ROWS, COPY_ROWS)
        for q in range(nq):
            ext[q, pl.ds(h_rows + r0, COPY_ROWS), :] = u_ref[q, pl.ds(r0, COPY_ROWS), :]
        return carry

    lax.fori_loop(0, seq // COPY_ROWS, copy, 0)

    sub = lax.broadcasted_iota(jnp.int32, (SUBLANES, LANES), 0)
    for q in range(nq):
        for m in range(HALO):
            nxt = pltpu.roll(u_ref[q, pl.ds(m * SUBLANES, SUBLANES), :], SUBLANES - 1, 0)
            ext[q, pl.ds(h_rows + seq + m * SUBLANES, SUBLANES), :] = jnp.where(sub == SUBLANES - 1, 0.0, nxt)
            prv = pltpu.roll(u_ref[q, pl.ds(seq - (m + 1) * SUBLANES, SUBLANES), :], 1, 0)
            ext[q, pl.ds(h_rows - (m + 1) * SUBLANES, SUBLANES), :] = jnp.where(sub == 0, 0.0, prv)


def _emit_time_order(src, o_ref, seq, post):
    seg_len = seq // NSEG

    def gather(bi, carry):
        j0 = pl.multiple_of(bi * PERM_ROWS, PERM_ROWS)
        for s in range(NSEG):
            val = jnp.concatenate(
                [src[q, pl.ds(j0 * NSEG + s, PERM_ROWS, stride=NSEG), :] for q in range(src.shape[0])], axis=1)
            r_time = s * seg_len + j0
            o_ref[0, pl.ds(r_time, PERM_ROWS), :] = post(val, r_time).astype(o_ref.dtype)
        return carry

    lax.fori_loop(0, seg_len // PERM_ROWS, gather, 0)


def _pool_kernel(u_ref, w_ref, s_ref, o_ref, ext, y_buf, *, seq, chunk):
    seg_len = seq // NSEG
    h_rows = HALO * SUBLANES
    nq = ext.shape[0]
    _fill_ext(ext, u_ref, seq)
    grp = pl.program_id(1)
    n_chunks = seq // chunk

    def body(win):
        half = win // 2

        def do_chunk(ci, carry=0):
            clipped = isinstance(ci, int)
            r0 = ci * chunk if clipped else pl.multiple_of(ci * chunk, chunk)
            base = r0 + h_rows
            if clipped:
                row = r0 + lax.broadcasted_iota(jnp.int32, (chunk, LANES), 0)
                t = (row & (SUBLANES - 1)) * seg_len + (row >> 3)
                cnt = jnp.minimum(t + half, seq) - jnp.maximum(t - half, 0)
                inv = 1.0 / cnt.astype(F32)
            else:
                inv = 1.0 / win
            d = []
            for q in range(nq):
                span_rows = (win - 1) * SUBLANES
                rows_in = ext[q, pl.ds(base - half * SUBLANES, chunk + span_rows), :]
                tot = rows_in
                step = SUBLANES
                while step <= half * SUBLANES:
                    keep = tot.shape[0] - step
                    tot = tot[:keep] + tot[step:step + keep]
                    step *= 2
                d.append(tot * inv - rows_in[half * SUBLANES:half * SUBLANES + chunk])
            d = jnp.concatenate(d, axis=1)
            y = jnp.dot(d.astype(BF16), w_ref[0], preferred_element_type=F32) * s_ref[0]
            _store_slabs(y_buf, r0, chunk, y)
            return carry

        assert chunk // SUBLANES >= half and n_chunks >= 2
        do_chunk(0)
        lax.fori_loop(1, n_chunks - 1, do_chunk, 0)
        do_chunk(n_chunks - 1)

    for gi, win in enumerate(POOL_WINDOWS):
        pl.when(grp == gi)(functools.partial(body, win))
    _emit_time_order(y_buf, o_ref, seq, lambda val, r_time: val)


def _pool_branch(z_seg, pool_w, pool_scale, batch, seq, chunk=512):
    n_groups, gw = pool_w.shape[0], pool_w.shape[1]
    nq = gw // LANES
    return pl.pallas_call(
        functools.partial(_pool_kernel, seq=seq, chunk=chunk),
        grid=(batch, n_groups),
        in_specs=[pl.BlockSpec((nq, seq, LANES), lambda b, g: (g, b, 0)),
                  pl.BlockSpec((1, gw, gw), lambda b, g: (g, 0, 0)),
                  pl.BlockSpec((1, 1, gw), lambda b, g: (g, 0, 0))],
        out_specs=pl.BlockSpec((1, seq, gw), lambda b, g: (b, 0, g)),
        out_shape=jax.ShapeDtypeStruct((batch, seq, n_groups * gw), BF16),
        scratch_shapes=[pltpu.VMEM((nq, seq + 2 * HALO * SUBLANES, LANES), F32),
                        pltpu.VMEM((nq, seq, LANES), F32)],
        compiler_params=pltpu.CompilerParams(
            dimension_semantics=("arbitrary", "arbitrary"),
            vmem_limit_bytes=VMEM_LIMIT),
        name="pool_branch",
    )(z_seg, pool_w, pool_scale.reshape(n_groups, 1, gw))


_P_CONV_W = 0
_P_CONV_B = CONV_WIDTH
_P_DIR = CONV_WIDTH + 1
_P_ROWS = 16


def _lru_kernel(u_ref, gate_ref, w_ref, p_ref, o_ref, ext, a_f, x_f, a_b, x_b, y_buf, summ, ent, *, seq, chunk):
    h_rows = HALO * SUBLANES
    c = LRU_BLOCK
    n_chunks = seq // chunk
    groups = chunk // SUBLANES
    _fill_ext(ext, u_ref, seq)

    conv_hw = [0.5 * p_ref[pl.ds(_P_CONV_W + k, 1), :] for k in range(CONV_WIDTH)]
    conv_hb = 0.5 * p_ref[pl.ds(_P_CONV_B, 1), :]
    hb_a, hb_x, c2 = [], [], []
    for d in range(2):
        hb_a.append(0.5 * p_ref[pl.ds(_P_DIR + 3 * d, 1), :])
        hb_x.append(0.5 * p_ref[pl.ds(_P_DIR + 3 * d + 1, 1), :])
        lam = p_ref[pl.ds(_P_DIR + 3 * d + 2, 1), :]
        c2.append((-0.5 * LRU_C * LOG2_E) * jax.nn.softplus(-lam))
    a_out = (a_f, a_b)
    x_out = (x_f, x_b)

    def rows(v, k):
        return v[k * SUBLANES:(k + 1) * SUBLANES]

    def summary_rows(ci):
        return pl.ds(pl.multiple_of(ci * SUBLANES, SUBLANES), SUBLANES)

    zeros = jnp.zeros((SUBLANES, c), F32)
    ones = jnp.ones((SUBLANES, c), F32)

    def gates(ci, carry):
        r0 = pl.multiple_of(ci * chunk, chunk)
        base = r0 + h_rows
        xh = conv_hb + _load_slabs(ext, base - CONV_LEFT * SUBLANES, chunk) * conv_hw[0]
        for k in range(1, CONV_WIDTH):
            xh = xh + _load_slabs(ext, base + (k - CONV_LEFT) * SUBLANES, chunk) * conv_hw[k]
        pre = jnp.dot(xh.astype(BF16), w_ref[0], preferred_element_type=F32)
        for d in range(2):
            t_r = jnp.tanh(pre[:, (2 * d) * c:(2 * d + 1) * c] + hb_a[d])
            t_i = jnp.tanh(pre[:, (2 * d + 1) * c:(2 * d + 2) * c] + hb_x[d])
            log2_a = c2[d] * t_r + c2[d]
            a = jnp.exp2(log2_a)
            one_m_a2 = (-1.0 - a * a) * jnp.tanh(LN_2 * log2_a)
            root = one_m_a2 * lax.rsqrt(jnp.maximum(one_m_a2, TINY))
            inp = root * (t_i * xh + xh)
            _store_slabs(a_out[d], r0, chunk, a)
            _store_slabs(x_out[d], r0, chunk, inp)
            h, p = zeros, ones
            for k in (range(groups) if d == 0 else range(groups - 1, -1, -1)):
                h = rows(a, k) * h + rows(inp, k)
                p = rows(a, k) * p
            summ[2 * d, summary_rows(ci), :] = h
            summ[2 * d + 1, summary_rows(ci), :] = p
        return carry

    lax.fori_loop(0, n_chunks, gates, 0)

    def summary(idx, ci):
        return summ[idx, ci * SUBLANES:(ci + 1) * SUBLANES, :]

    h_f, p_f, h_b, p_b = zeros, ones, zeros, ones
    for ci in range(n_chunks):
        cb = n_chunks - 1 - ci
        h_f = summary(0, ci) + summary(1, ci) * h_f
        p_f = summary(1, ci) * p_f
        h_b = summary(2, cb) + summary(3, cb) * h_b
        p_b = summary(3, cb) * p_b
    sub = lax.broadcasted_iota(jnp.int32, (SUBLANES, c), 0)
    e_f = zeros
    e_b = zeros
    for _ in range(NSEG - 1):
        e_f = jnp.where(sub == 0, 0.0, pltpu.roll(h_f + p_f * e_f, 1, 0))
        e_b = jnp.where(sub == SUBLANES - 1, 0.0, pltpu.roll(h_b + p_b * e_b, SUBLANES - 1, 0))
    for ci in range(n_chunks):
        cb = n_chunks - 1 - ci
        ent[0, ci * SUBLANES:(ci + 1) * SUBLANES, :] = e_f
        ent[1, cb * SUBLANES:(cb + 1) * SUBLANES, :] = e_b
        e_f = summary(0, ci) + summary(1, ci) * e_f
        e_b = summary(2, cb) + summary(3, cb) * e_b

    def replay(it, carry):
        for sub_i in range(REPLAY_CHUNKS):
            ci = it * REPLAY_CHUNKS + sub_i
            r0 = pl.multiple_of(ci * chunk, chunk)
            af, xf = _load_slabs(a_f, r0, chunk), _load_slabs(x_f, r0, chunk)
            ab, xb = _load_slabs(a_b, r0, chunk), _load_slabs(x_b, r0, chunk)
            h = ent[0, summary_rows(ci), :]
            fwd = []
            for k in range(groups):
                h = rows(af, k) * h + rows(xf, k)
                fwd.append(h)
            h = ent[1, summary_rows(ci), :]
            out = [None] * groups
            for k in range(groups - 1, -1, -1):
                h = rows(ab, k) * h + rows(xb, k)
                out[k] = h + fwd[k]
            _store_slabs(y_buf, r0, chunk, jnp.concatenate(out, axis=0))
        return carry

    lax.fori_loop(0, n_chunks // REPLAY_CHUNKS, replay, 0)

    _emit_time_order(y_buf, o_ref, seq, lambda val, r_time: val * gate_ref[0, pl.ds(r_time, PERM_ROWS), :])


def _lru_branch(z_seg, gate_act, w_gates, params, batch, seq, lru_col0, chunk=512):
    n_heads = w_gates.shape[0]
    c = LRU_BLOCK
    nq = c // LANES
    z3 = z_seg
    g3 = gate_act.reshape(batch, seq, gate_act.shape[1])
    col0 = lru_col0 // c
    n_chunks = seq // chunk
    seg_buf = pltpu.VMEM((nq, seq, LANES), F32)
    return pl.pallas_call(
        functools.partial(_lru_kernel, seq=seq, chunk=chunk),
        grid=(batch, n_heads),
        in_specs=[pl.BlockSpec((nq, seq, LANES), lambda b, h: (col0 + h, b, 0)),
                  pl.BlockSpec((1, seq, c), lambda b, h: (b, 0, h)),
                  pl.BlockSpec((1, c, 4 * c), lambda b, h: (h, 0, 0)),
                  pl.BlockSpec((_P_ROWS, c), lambda b, h: (0, h))],
        out_specs=pl.BlockSpec((1, seq, c), lambda b, h: (b, 0, h)),
        out_shape=jax.ShapeDtypeStruct((batch, seq, n_heads * c), BF16),
        scratch_shapes=[pltpu.VMEM((nq, seq + 2 * HALO * SUBLANES, LANES), F32)]
        + [seg_buf] * 5
        + [pltpu.VMEM((4, n_chunks * SUBLANES, c), F32), pltpu.VMEM((2, n_chunks * SUBLANES, c), F32)],
        compiler_params=pltpu.CompilerParams(
            dimension_semantics=("arbitrary", "arbitrary"),
            vmem_limit_bytes=VMEM_LIMIT),
        name="lru_branch",
    )(z3, g3, w_gates, params)


def _merge_kernel(yp_ref, yl_ref, g0_ref, g1_ref, x_ref, wp_ref, wl_ref, wo_ref, bo_ref, g_ref, b_ref,
                  of_ref, ob_ref, *, alpha):
    up_p = jnp.dot(yp_ref[...], wp_ref[...], preferred_element_type=F32)
    up_l = jnp.dot(yl_ref[...], wl_ref[...], preferred_element_type=F32)
    m = g0_ref[...] * up_p + g1_ref[...] * up_l
    mix = jnp.dot(m.astype(BF16), wo_ref[...], preferred_element_type=F32) + bo_ref[...]
    y = _layer_norm(alpha * x_ref[...] + mix, g_ref[...], b_ref[...])
    of_ref[...] = y
    ob_ref[...] = y.astype(BF16)


def _merge(y_pool, y_lru, g_act, x_rows, w_pool_up, w_lru_up, w_out, b_out, ln_g, ln_b, alpha, tm=256):
    t, d = x_rows.shape
    row = lambda i: (i, 0)
    const = lambda i: (0, 0)
    wspec = pl.BlockSpec((d, d), const, pipeline_mode=pl.Buffered(1))
    vspec = pl.BlockSpec((1, d), const)
    return pl.pallas_call(
        functools.partial(_merge_kernel, alpha=alpha),
        grid=(t // tm,),
        in_specs=[pl.BlockSpec((tm, d), row), pl.BlockSpec((tm, d), row),
                  pl.BlockSpec((tm, d), lambda i: (i, 0)), pl.BlockSpec((tm, d), lambda i: (i, 1)),
                  pl.BlockSpec((tm, d), row), wspec, wspec, wspec, vspec, vspec, vspec],
        out_specs=[pl.BlockSpec((tm, d), row), pl.BlockSpec((tm, d), row)],
        out_shape=[jax.ShapeDtypeStruct((t, d), F32), jax.ShapeDtypeStruct((t, d), BF16)],
        compiler_params=pltpu.CompilerParams(
            dimension_semantics=("arbitrary",), vmem_limit_bytes=VMEM_LIMIT),
        name="merge_out_ln",
    )(y_pool, y_lru, g_act, g_act, x_rows, w_pool_up, w_lru_up, w_out,
      b_out.reshape(1, d), ln_g.reshape(1, d), ln_b.reshape(1, d))


def _mlp_kernel(xb_ref, xf_ref, w1_ref, b1_ref, w2_ref, b2_ref, g_ref, b_ref, o_ref, *, alpha):
    f = pl.program_id(1)

    @pl.when(f == 0)
    def _():
        o_ref[...] = alpha * xf_ref[...] + b2_ref[...]

    h = jnp.dot(xb_ref[...], w1_ref[0].astype(BF16), preferred_element_type=F32) + b1_ref[...]
    h = jnp.square(jnp.maximum(h, 0.0))
    o_ref[...] += jnp.dot(h.astype(BF16), w2_ref[0].astype(BF16), preferred_element_type=F32)

    @pl.when(f == pl.num_programs(1) - 1)
    def _():
        o_ref[...] = _layer_norm(o_ref[...], g_ref[...], b_ref[...])


def _mlp(x_b, x_f, w1, layer, b1, w2, b2, ln_g, ln_b, alpha, tm=1024, tf=512):
    t, d = x_f.shape
    dff = w1.shape[2]
    return pl.pallas_call(
        functools.partial(_mlp_kernel, alpha=alpha),
        grid=(t // tm, dff // tf),
        in_specs=[pl.BlockSpec((tm, d), lambda i, f: (i, 0)),
                  pl.BlockSpec((tm, d), lambda i, f: (i, 0), pipeline_mode=pl.Buffered(1)),
                  pl.BlockSpec((1, d, tf), lambda i, f: (layer, 0, f)),
                  pl.BlockSpec((1, tf), lambda i, f: (0, f)),
                  pl.BlockSpec((1, tf, d), lambda i, f: (layer, f, 0)),
                  pl.BlockSpec((1, d), lambda i, f: (0, 0)),
                  pl.BlockSpec((1, d), lambda i, f: (0, 0)),
                  pl.BlockSpec((1, d), lambda i, f: (0, 0))],
        out_specs=pl.BlockSpec((tm, d), lambda i, f: (i, 0)),
        out_shape=jax.ShapeDtypeStruct((t, d), F32),
        compiler_params=pltpu.CompilerParams(
            dimension_semantics=("arbitrary", "arbitrary"), vmem_limit_bytes=VMEM_LIMIT),
        name="mlp_ln",
    )(x_b, x_f, w1, b1.reshape(1, dff), w2, b2.reshape(1, d), ln_g.reshape(1, d), ln_b.reshape(1, d))


def _layer(x_rows, batch, seq, alpha, layer, w_in, pool_w, pool_scale, conv_w, conv_b, lru_wa, lru_ba, lru_wx,
           lru_bx, lru_lambda, w_pool_up, w_lru_up, w_out, b_out, ln1_g, ln1_b, w_ff1, b_ff1, w_ff2, b_ff2,
           ln2_g, ln2_b):
    pool_width = pool_w.shape[0] * pool_w.shape[1]
    lru_width = conv_w.shape[1]
    o1, o2, o3 = pool_width, pool_width + lru_width, pool_width + 2 * lru_width

    z_a, = _mm_act(x_rows, w_in, layer, 0, o2, "none", seg=(batch, seq))
    g_act, = _mm_act(x_rows, w_in, layer, o3, w_in.shape[2] - o3, "sigmoid")
    gate_act, w_pool_up_b, w_lru_up_b, w_out_b = _mm_act(
        x_rows, w_in, layer, o2, o3 - o2, "gelu", casts=(w_pool_up, w_lru_up, w_out))

    y_pool = _pool_branch(z_a, pool_w.astype(BF16), pool_scale, batch, seq)

    w_gates = jnp.concatenate([lru_wa[0], lru_wx[0], lru_wa[1], lru_wx[1]], axis=-1).astype(BF16)
    params = jnp.concatenate(
        [conv_w, conv_b[None], lru_ba[0][None], lru_bx[0][None], lru_lambda[0][None],
         lru_ba[1][None], lru_bx[1][None], lru_lambda[1][None]], axis=0).astype(F32)
    params = jnp.pad(params, ((0, _P_ROWS - params.shape[0]), (0, 0)))
    y_lru = _lru_branch(z_a, gate_act, w_gates, params, batch, seq, o1)

    t = batch * seq
    x1_f, x1_b = _merge(y_pool.reshape(t, -1), y_lru.reshape(t, -1), g_act, x_rows,
                        w_pool_up_b, w_lru_up_b, w_out_b,
                        b_out, ln1_g, ln1_b, alpha)
    return _mlp(x1_b, x1_f, w_ff1, layer, b_ff1, w_ff2, b_ff2, ln2_g, ln2_b, alpha)


def kernel(x, w_in, pool_w, pool_scale, conv_w, conv_b, lru_wa, lru_ba, lru_wx, lru_bx, lru_lambda, w_pool_up, w_lru_up, w_out, b_out, ln1_g, ln1_b, w_ff1, b_ff1, w_ff2, b_ff2, ln2_g, ln2_b):
    batch, seq, d = x.shape
    depth = w_in.shape[0]
    alpha = (2.0 * depth) ** 0.25
    rows = x.reshape(batch * seq, d)
    for l in range(depth):
        rows = _layer(rows, batch, seq, alpha, l, w_in, pool_w[l], pool_scale[l], conv_w[l], conv_b[l],
                      lru_wa[l], lru_ba[l], lru_wx[l], lru_bx[l], lru_lambda[l], w_pool_up[l], w_lru_up[l],
                      w_out[l], b_out[l], ln1_g[l], ln1_b[l], w_ff1, b_ff1[l], w_ff2, b_ff2[l],
                      ln2_g[l], ln2_b[l])
    return rows.reshape(batch, seq, d)
```

```python
import functools

import jax
import jax.numpy as jnp
from jax import lax
from jax.experimental import pallas as pl
from jax.experimental.pallas import tpu as pltpu

SUBLANES = 8
LANES = 128
NSEG = SUBLANES

POOL_WINDOWS = (2, 4, 8, 16)
LRU_BLOCK = 256
CONV_WIDTH = 4
CONV_LEFT = CONV_WIDTH // 2
LRU_C = 8.0
LN_EPS = 1e-5
HALO = 8
REPLAY_CHUNKS = 2
PERM_ROWS = 32
COPY_ROWS = 512
LOG2_E = 1.4426950408889634
LN_2 = 0.6931471805599453
TINY = 1.1754944e-38

VMEM_LIMIT = 60 * 1024 * 1024

F32 = jnp.float32
BF16 = jnp.bfloat16


def _sigmoid(v):
    return 0.5 * jnp.tanh(0.5 * v) + 0.5


def _gelu_tanh(v):
    c = 0.7978845608028654
    return 0.5 * v * (1.0 + jnp.tanh(c * (v + 0.044715 * (v * v * v))))


def _layer_norm(y, g, b):
    mu = jnp.mean(y, axis=-1, keepdims=True)
    yc = y - mu
    var = jnp.mean(yc * yc, axis=-1, keepdims=True)
    return yc * lax.rsqrt(var + LN_EPS) * g + b


def _mm_act_kernel(*refs, act, n_casts, seg):
    x_ref, w_ref = refs[:2]
    cast_in = refs[2:2 + n_casts]
    o_ref = refs[2 + n_casts]
    cast_out = refs[3 + n_casts:3 + 2 * n_casts]
    w_bf = refs[-1]

    @pl.when(pl.program_id(1) == 0)
    def _():
        w_bf[...] = w_ref[0].astype(BF16)

    if seg:
        tj = x_ref.shape[2]
        x_tile = x_ref[0].reshape(NSEG * tj, x_ref.shape[3])
    else:
        x_tile = x_ref[...]
    acc = jnp.dot(x_tile.astype(BF16), w_bf[...], preferred_element_type=F32)
    if act == "gelu":
        acc = _gelu_tanh(acc)
    elif act == "sigmoid":
        acc = _sigmoid(acc)
    if seg:
        for q in range(o_ref.shape[0]):
            slab = acc[:, q * LANES:(q + 1) * LANES].reshape(NSEG, tj, LANES)
            o_ref[q] = pltpu.einshape("sjc->jsc", slab).reshape(NSEG * tj, LANES)
    else:
        o_ref[...] = acc.astype(o_ref.dtype)
    for src, dst in zip(cast_in, cast_out):
        dst[...] = src[...].astype(BF16)


def _mm_act(x, w, layer, col0, n, act, casts=(), seg=None, tm=512, tn=2048):
    m, k = x.shape
    j0 = col0 // tn
    n_i = m // tm
    steps = (n // tn) * n_i
    cast_specs = [pl.BlockSpec((a.shape[0] // steps, a.shape[1]), lambda j, i: (j * n_i + i, 0)) for a in casts]
    if seg:
        batch, seq = seg
        tj = tm // NSEG
        n_jb = seq // NSEG // tj
        x = x.reshape(batch, NSEG, seq // NSEG, k)
        x_spec = pl.BlockSpec((1, NSEG, tj, k), lambda j, i: (i // n_jb, 0, i % n_jb, 0))
        o_spec = pl.BlockSpec((tn // LANES, tm, LANES), lambda j, i: (j, i, 0))
        o_shape = jax.ShapeDtypeStruct((n // LANES, m, LANES), F32)
    else:
        x_spec = pl.BlockSpec((tm, k), lambda j, i: (i, 0))
        o_spec = pl.BlockSpec((tm, tn), lambda j, i: (i, j))
        o_shape = jax.ShapeDtypeStruct((m, n), F32)
    return pl.pallas_call(
        functools.partial(_mm_act_kernel, act=act, n_casts=len(casts), seg=bool(seg)),
        grid=(n // tn, n_i),
        in_specs=[x_spec, pl.BlockSpec((1, k, tn), lambda j, i: (layer, 0, j0 + j))] + cast_specs,
        out_specs=[o_spec] + cast_specs,
        out_shape=[o_shape] + [jax.ShapeDtypeStruct(a.shape, BF16) for a in casts],
        scratch_shapes=[pltpu.VMEM((k, tn), BF16)],
        compiler_params=pltpu.CompilerParams(
            dimension_semantics=("arbitrary", "arbitrary"),
            vmem_limit_bytes=VMEM_LIMIT),
        name=f"mm_in_{act}",
    )(x, w, *casts)


def _load_slabs(buf, r, n):
    return jnp.concatenate([buf[q, pl.ds(r, n), :] for q in range(buf.shape[0])], axis=1)


def _store_slabs(buf, r, n, val):
    for q in range(buf.shape[0]):
        buf[q, pl.ds(r, n), :] = val[:, q * LANES:(q + 1) * LANES]


def _fill_ext(ext, u_ref, seq):
    h_rows = HALO * SUBLANES
    nq = ext.shape[0]

    def copy(bi, carry):
        r0 = pl.multiple_of(bi * COPY_ROWS, COPY_ROWS)
        for q in range(nq):
            ext[q, pl.ds(h_rows + r0, COPY_ROWS), :] = u_ref[q, pl.ds(r0, COPY_ROWS), :]
        return carry

    lax.fori_loop(0, seq // COPY_ROWS, copy, 0)

    sub = lax.broadcasted_iota(jnp.int32, (SUBLANES, LANES), 0)
    for q in range(nq):
        for m in range(HALO):
            nxt = pltpu.roll(u_ref[q, pl.ds(m * SUBLANES, SUBLANES), :], SUBLANES - 1, 0)
            ext[q, pl.ds(h_rows + seq + m * SUBLANES, SUBLANES), :] = jnp.where(sub == SUBLANES - 1, 0.0, nxt)
            prv = pltpu.roll(u_ref[q, pl.ds(seq - (m + 1) * SUBLANES, SUBLANES), :], 1, 0)
            ext[q, pl.ds(h_rows - (m + 1) * SUBLANES, SUBLANES), :] = jnp.where(sub == 0, 0.0, prv)


def _emit_time_order(src, o_ref, seq, post):
    seg_len = seq // NSEG

    def gather(bi, carry):
        j0 = pl.multiple_of(bi * PERM_ROWS, PERM_ROWS)
        for s in range(NSEG):
            val = jnp.concatenate(
                [src[q, pl.ds(j0 * NSEG + s, PERM_ROWS, stride=NSEG), :] for q in range(src.shape[0])], axis=1)
            r_time = s * seg_len + j0
            o_ref[0, pl.ds(r_time, PERM_ROWS), :] = post(val, r_time).astype(o_ref.dtype)
        return carry

    lax.fori_loop(0, seg_len // PERM_ROWS, gather, 0)


def _pool_kernel(u_ref, w_ref, s_ref, o_ref, ext, y_buf, *, seq, chunk):
    seg_len = seq // NSEG
    h_rows = HALO * SUBLANES
    nq = ext.shape[0]
    _fill_ext(ext, u_ref, seq)
    grp = pl.program_id(1)
    n_chunks = seq // chunk

    def body(win):
        half = win // 2

        def do_chunk(ci, carry=0):
            clipped = isinstance(ci, int)
            r0 = ci * chunk if clipped else pl.multiple_of(ci * chunk, chunk)
            base = r0 + h_rows
            if clipped:
                row = r0 + lax.broadcasted_iota(jnp.int32, (chunk, LANES), 0)
                t = (row & (SUBLANES - 1)) * seg_len + (row >> 3)
                cnt = jnp.minimum(t + half, seq) - jnp.maximum(t - half, 0)
                inv = 1.0 / cnt.astype(F32)
            else:
                inv = 1.0 / win
            d = []
            for q in range(nq):
                span_rows = (win - 1) * SUBLANES
                rows_in = ext[q, pl.ds(base - half * SUBLANES, chunk + span_rows), :]
                tot = rows_in
                step = SUBLANES
                while step <= half * SUBLANES:
                    keep = tot.shape[0] - step
                    tot = tot[:keep] + tot[step:step + keep]
                    step *= 2
                d.append(tot * inv - rows_in[half * SUBLANES:half * SUBLANES + chunk])
            d = jnp.concatenate(d, axis=1)
            y = jnp.dot(d.astype(BF16), w_ref[0], preferred_element_type=F32) * s_ref[0]
            _store_slabs(y_buf, r0, chunk, y)
            return carry

        assert chunk // SUBLANES >= half and n_chunks >= 2
        do_chunk(0)
        lax.fori_loop(1, n_chunks - 1, do_chunk, 0)
        do_chunk(n_chunks - 1)

    for gi, win in enumerate(POOL_WINDOWS):
        pl.when(grp == gi)(functools.partial(body, win))
    _emit_time_order(y_buf, o_ref, seq, lambda val, r_time: val)


def _pool_branch(z_seg, pool_w, pool_scale, batch, seq, chunk=512):
    n_groups, gw = pool_w.shape[0], pool_w.shape[1]
    nq = gw // LANES
    return pl.pallas_call(
        functools.partial(_pool_kernel, seq=seq, chunk=chunk),
        grid=(batch, n_groups),
        in_specs=[pl.BlockSpec((nq, seq, LANES), lambda b, g: (g, b, 0)),
                  pl.BlockSpec((1, gw, gw), lambda b, g: (g, 0, 0)),
                  pl.BlockSpec((1, 1, gw), lambda b, g: (g, 0, 0))],
        out_specs=pl.BlockSpec((1, seq, gw), lambda b, g: (b, 0, g)),
        out_shape=jax.ShapeDtypeStruct((batch, seq, n_groups * gw), BF16),
        scratch_shapes=[pltpu.VMEM((nq, seq + 2 * HALO * SUBLANES, LANES), F32),
                        pltpu.VMEM((nq, seq, LANES), F32)],
        compiler_params=pltpu.CompilerParams(
            dimension_semantics=("arbitrary", "arbitrary"),
            vmem_limit_bytes=VMEM_LIMIT),
        name="pool_branch",
    )(z_seg, pool_w, pool_scale.reshape(n_groups, 1, gw))


_P_CONV_W = 0
_P_CONV_B = CONV_WIDTH
_P_DIR = CONV_WIDTH + 1
_P_ROWS = 16


def _lru_kernel(u_ref, gate_ref, w_ref, p_ref, o_ref, ext, a_f, x_f, a_b, x_b, y_buf, summ, ent, *, seq, chunk):
    h_rows = HALO * SUBLANES
    c = LRU_BLOCK
    n_chunks = seq // chunk
    groups = chunk // SUBLANES
    _fill_ext(ext, u_ref, seq)

    conv_hw = [0.5 * p_ref[pl.ds(_P_CONV_W + k, 1), :] for k in range(CONV_WIDTH)]
    conv_hb = 0.5 * p_ref[pl.ds(_P_CONV_B, 1), :]
    hb_a, hb_x, c2 = [], [], []
    for d in range(2):
        hb_a.append(0.5 * p_ref[pl.ds(_P_DIR + 3 * d, 1), :])
        hb_x.append(0.5 * p_ref[pl.ds(_P_DIR + 3 * d + 1, 1), :])
        lam = p_ref[pl.ds(_P_DIR + 3 * d + 2, 1), :]
        c2.append((-0.5 * LRU_C * LOG2_E) * jax.nn.softplus(-lam))
    a_out = (a_f, a_b)
    x_out = (x_f, x_b)

    def rows(v, k):
        return v[k * SUBLANES:(k + 1) * SUBLANES]

    def summary_rows(ci):
        return pl.ds(pl.multiple_of(ci * SUBLANES, SUBLANES), SUBLANES)

    zeros = jnp.zeros((SUBLANES, c), F32)
    ones = jnp.ones((SUBLANES, c), F32)

    def gates(ci, carry):
        r0 = pl.multiple_of(ci * chunk, chunk)
        base = r0 + h_rows
        xh = conv_hb + _load_slabs(ext, base - CONV_LEFT * SUBLANES, chunk) * conv_hw[0]
        for k in range(1, CONV_WIDTH):
            xh = xh + _load_slabs(ext, base + (k - CONV_LEFT) * SUBLANES, chunk) * conv_hw[k]
        pre = jnp.dot(xh.astype(BF16), w_ref[0], preferred_element_type=F32)
        for d in range(2):
            t_r = jnp.tanh(pre[:, (2 * d) * c:(2 * d + 1) * c] + hb_a[d])
            t_i = jnp.tanh(pre[:, (2 * d + 1) * c:(2 * d + 2) * c] + hb_x[d])
            log2_a = c2[d] * t_r + c2[d]
            a = jnp.exp2(log2_a)
            one_m_a2 = (-1.0 - a * a) * jnp.tanh(LN_2 * log2_a)
            root = one_m_a2 * lax.rsqrt(jnp.maximum(one_m_a2, TINY))
            inp = root * (t_i * xh + xh)
            _store_slabs(a_out[d], r0, chunk, a)
            _store_slabs(x_out[d], r0, chunk, inp)
            h, p = zeros, ones
            for k in (range(groups) if d == 0 else range(groups - 1, -1, -1)):
                h = rows(a, k) * h + rows(inp, k)
                p = rows(a, k) * p
            summ[2 * d, summary_rows(ci), :] = h
            summ[2 * d + 1, summary_rows(ci), :] = p
        return carry

    lax.fori_loop(0, n_chunks, gates, 0)

    def summary(idx, ci):
        return summ[idx, ci * SUBLANES:(ci + 1) * SUBLANES, :]

    h_f, p_f, h_b, p_b = zeros, ones, zeros, ones
    for ci in range(n_chunks):
        cb = n_chunks - 1 - ci
        h_f = summary(0, ci) + summary(1, ci) * h_f
        p_f = summary(1, ci) * p_f
        h_b = summary(2, cb) + summary(3, cb) * h_b
        p_b = summary(3, cb) * p_b
    sub = lax.broadcasted_iota(jnp.int32, (SUBLANES, c), 0)
    e_f = zeros
    e_b = zeros
    for _ in range(NSEG - 1):
        e_f = jnp.where(sub == 0, 0.0, pltpu.roll(h_f + p_f * e_f, 1, 0))
        e_b = jnp.where(sub == SUBLANES - 1, 0.0, pltpu.roll(h_b + p_b * e_b, SUBLANES - 1, 0))
    for ci in range(n_chunks):
        cb = n_chunks - 1 - ci
        ent[0, ci * SUBLANES:(ci + 1) * SUBLANES, :] = e_f
        ent[1, cb * SUBLANES:(cb + 1) * SUBLANES, :] = e_b
        e_f = summary(0, ci) + summary(1, ci) * e_f
        e_b = summary(2, cb) + summary(3, cb) * e_b

    def replay(it, carry):
        for sub_i in range(REPLAY_CHUNKS):
            ci = it * REPLAY_CHUNKS + sub_i
            r0 = pl.multiple_of(ci * chunk, chunk)
            af, xf = _load_slabs(a_f, r0, chunk), _load_slabs(x_f, r0, chunk)
            ab, xb = _load_slabs(a_b, r0, chunk), _load_slabs(x_b, r0, chunk)
            h = ent[0, summary_rows(ci), :]
            fwd = []
            for k in range(groups):
                h = rows(af, k) * h + rows(xf, k)
                fwd.append(h)
            h = ent[1, summary_rows(ci), :]
            out = [None] * groups
            for k in range(groups - 1, -1, -1):
                h = rows(ab, k) * h + rows(xb, k)
                out[k] = h + fwd[k]
            _store_slabs(y_buf, r0, chunk, jnp.concatenate(out, axis=0))
        return carry

    lax.fori_loop(0, n_chunks // REPLAY_CHUNKS, replay, 0)

    _emit_time_order(y_buf, o_ref, seq, lambda val, r_time: val * gate_ref[0, pl.ds(r_time, PERM_ROWS), :])


def _lru_branch(z_seg, gate_act, w_gates, params, batch, seq, lru_col0, chunk=512):
    n_heads = w_gates.shape[0]
    c = LRU_BLOCK
    nq = c // LANES
    z3 = z_seg
    g3 = gate_act.reshape(batch, seq, gate_act.shape[1])
    col0 = lru_col0 // c
    n_chunks = seq // chunk
    seg_buf = pltpu.VMEM((nq, seq, LANES), F32)
    return pl.pallas_call(
        functools.partial(_lru_kernel, seq=seq, chunk=chunk),
        grid=(batch, n_heads),
        in_specs=[pl.BlockSpec((nq, seq, LANES), lambda b, h: (col0 + h, b, 0)),
                  pl.BlockSpec((1, seq, c), lambda b, h: (b, 0, h)),
                  pl.BlockSpec((1, c, 4 * c), lambda b, h: (h, 0, 0)),
                  pl.BlockSpec((_P_ROWS, c), lambda b, h: (0, h))],
        out_specs=pl.BlockSpec((1, seq, c), lambda b, h: (b, 0, h)),
        out_shape=jax.ShapeDtypeStruct((batch, seq, n_heads * c), BF16),
        scratch_shapes=[pltpu.VMEM((nq, seq + 2 * HALO * SUBLANES, LANES), F32)]
        + [seg_buf] * 5
        + [pltpu.VMEM((4, n_chunks * SUBLANES, c), F32), pltpu.VMEM((2, n_chunks * SUBLANES, c), F32)],
        compiler_params=pltpu.CompilerParams(
            dimension_semantics=("arbitrary", "arbitrary"),
            vmem_limit_bytes=VMEM_LIMIT),
        name="lru_branch",
    )(z3, g3, w_gates, params)


def _merge_kernel(yp_ref, yl_ref, g0_ref, g1_ref, x_ref, wp_ref, wl_ref, wo_ref, bo_ref, g_ref, b_ref,
                  of_ref, ob_ref, *, alpha):
    up_p = jnp.dot(yp_ref[...], wp_ref[...], preferred_element_type=F32)
    up_l = jnp.dot(yl_ref[...], wl_ref[...], preferred_element_type=F32)
    m = g0_ref[...] * up_p + g1_ref[...] * up_l
    mix = jnp.dot(m.astype(BF16), wo_ref[...], preferred_element_type=F32) + bo_ref[...]
    y = _layer_norm(alpha * x_ref[...] + mix, g_ref[...], b_ref[...])
    of_ref[...] = y
    ob_ref[...] = y.astype(BF16)


def _merge(y_pool, y_lru, g_act, x_rows, w_pool_up, w_lru_up, w_out, b_out, ln_g, ln_b, alpha, tm=256):
    t, d = x_rows.shape
    row = lambda i: (i, 0)
    const = lambda i: (0, 0)
    wspec = pl.BlockSpec((d, d), const, pipeline_mode=pl.Buffered(1))
    vspec = pl.BlockSpec((1, d), const)
    return pl.pallas_call(
        functools.partial(_merge_kernel, alpha=alpha),
        grid=(t // tm,),
        in_specs=[pl.BlockSpec((tm, d), row), pl.BlockSpec((tm, d), row),
                  pl.BlockSpec((tm, d), lambda i: (i, 0)), pl.BlockSpec((tm, d), lambda i: (i, 1)),
                  pl.BlockSpec((tm, d), row), wspec, wspec, wspec, vspec, vspec, vspec],
        out_specs=[pl.BlockSpec((tm, d), row), pl.BlockSpec((tm, d), row)],
        out_shape=[jax.ShapeDtypeStruct((t, d), F32), jax.ShapeDtypeStruct((t, d), BF16)],
        compiler_params=pltpu.CompilerParams(
            dimension_semantics=("arbitrary",), vmem_limit_bytes=VMEM_LIMIT),
        name="merge_out_ln",
    )(y_pool, y_lru, g_act, g_act, x_rows, w_pool_up, w_lru_up, w_out,
      b_out.reshape(1, d), ln_g.reshape(1, d), ln_b.reshape(1, d))


def _mlp_kernel(xb_ref, xf_ref, w1_ref, b1_ref, w2_ref, b2_ref, g_ref, b_ref, o_ref, *, alpha):
    f = pl.program_id(1)

    @pl.when(f == 0)
    def _():
        o_ref[...] = alpha * xf_ref[...] + b2_ref[...]

    h = jnp.dot(xb_ref[...], w1_ref[0].astype(BF16), preferred_element_type=F32) + b1_ref[...]
    h = jnp.square(jnp.maximum(h, 0.0))
    o_ref[...] += jnp.dot(h.astype(BF16), w2_ref[0].astype(BF16), preferred_element_type=F32)

    @pl.when(f == pl.num_programs(1) - 1)
    def _():
        o_ref[...] = _layer_norm(o_ref[...], g_ref[...], b_ref[...])


def _mlp(x_b, x_f, w1, layer, b1, w2, b2, ln_g, ln_b, alpha, tm=1024, tf=512):
    t, d = x_f.shape
    dff = w1.shape[2]
    return pl.pallas_call(
        functools.partial(_mlp_kernel, alpha=alpha),
        grid=(t // tm, dff // tf),
        in_specs=[pl.BlockSpec((tm, d), lambda i, f: (i, 0)),
                  pl.BlockSpec((tm, d), lambda i, f: (i, 0), pipeline_mode=pl.Buffered(1)),
                  pl.BlockSpec((1, d, tf), lambda i, f: (layer, 0, f)),
                  pl.BlockSpec((1, tf), lambda i, f: (0, f)),
                  pl.BlockSpec((1, tf, d), lambda i, f: (layer, f, 0)),
                  pl.BlockSpec((1, d), lambda i, f: (0, 0)),
                  pl.BlockSpec((1, d), lambda i, f: (0, 0)),
                  pl.BlockSpec((1, d), lambda i, f: (0, 0))],
        out_specs=pl.BlockSpec((tm, d), lambda i, f: (i, 0)),
        out_shape=jax.ShapeDtypeStruct((t, d), F32),
        compiler_params=pltpu.CompilerParams(
            dimension_semantics=("arbitrary", "arbitrary"), vmem_limit_bytes=VMEM_LIMIT),
        name="mlp_ln",
    )(x_b, x_f, w1, b1.reshape(1, dff), w2, b2.reshape(1, d), ln_g.reshape(1, d), ln_b.reshape(1, d))


def _layer(x_rows, batch, seq, alpha, layer, w_in, pool_w, pool_scale, conv_w, conv_b, lru_wa, lru_ba, lru_wx,
           lru_bx, lru_lambda, w_pool_up, w_lru_up, w_out, b_out, ln1_g, ln1_b, w_ff1, b_ff1, w_ff2, b_ff2,
           ln2_g, ln2_b):
    pool_width = pool_w.shape[0] * pool_w.shape[1]
    lru_width = conv_w.shape[1]
    o1, o2, o3 = pool_width, pool_width + lru_width, pool_width + 2 * lru_width

    z_a, = _mm_act(x_rows, w_in, layer, 0, o2, "none", seg=(batch, seq))
    g_act, = _mm_act(x_rows, w_in, layer, o3, w_in.shape[2] - o3, "sigmoid")
    gate_act, w_pool_up_b, w_lru_up_b, w_out_b = _mm_act(
        x_rows, w_in, layer, o2, o3 - o2, "gelu", casts=(w_pool_up, w_lru_up, w_out))

    y_pool = _pool_branch(z_a, pool_w.astype(BF16), pool_scale, batch, seq)

    w_gates = jnp.concatenate([lru_wa[0], lru_wx[0], lru_wa[1], lru_wx[1]], axis=-1).astype(BF16)
    params = jnp.concatenate(
        [conv_w, conv_b[None], lru_ba[0][None], lru_bx[0][None], lru_lambda[0][None],
         lru_ba[1][None], lru_bx[1][None], lru_lambda[1][None]], axis=0).astype(F32)
    params = jnp.pad(params, ((0, _P_ROWS - params.shape[0]), (0, 0)))
    y_lru = _lru_branch(z_a, gate_act, w_gates, params, batch, seq, o1)

    t = batch * seq
    x1_f, x1_b = _merge(y_pool.reshape(t, -1), y_lru.reshape(t, -1), g_act, x_rows,
                        w_pool_up_b, w_lru_up_b, w_out_b,
                        b_out, ln1_g, ln1_b, alpha)
    return _mlp(x1_b, x1_f, w_ff1, layer, b_ff1, w_ff2, b_ff2, ln2_g, ln2_b, alpha)


def kernel(x, w_in, pool_w, pool_scale, conv_w, conv_b, lru_wa, lru_ba, lru_wx, lru_bx, lru_lambda, w_pool_up, w_lru_up, w_out, b_out, ln1_g, ln1_b, w_ff1, b_ff1, w_ff2, b_ff2, ln2_g, ln2_b):
    batch, seq, d = x.shape
    depth = w_in.shape[0]
    alpha = (2.0 * depth) ** 0.25
    rows = x.reshape(batch * seq, d)
    for l in range(depth):
        rows = _layer(rows, batch, seq, alpha, l, w_in, pool_w[l], pool_scale[l], conv_w[l], conv_b[l],
                      lru_wa[l], lru_ba[l], lru_wx[l], lru_bx[l], lru_lambda[l], w_pool_up[l], w_lru_up[l],
                      w_out[l], b_out[l], ln1_g[l], ln1_b[l], w_ff1, b_ff1[l], w_ff2, b_ff2[l],
                      ln2_g[l], ln2_b[l])
    return rows.reshape(batch, seq, d)
```

```python
import functools

import jax
import jax.numpy as jnp
from jax import lax
from jax.experimental import pallas as pl
from jax.experimental.pallas import tpu as pltpu

SUBLANES = 8
LANES = 128
NSEG = SUBLANES

POOL_WINDOWS = (2, 4, 8, 16)
LRU_BLOCK = 256
CONV_WIDTH = 4
CONV_LEFT = CONV_WIDTH // 2
LRU_C = 8.0
LN_EPS = 1e-5
HALO = 8
REPLAY_CHUNKS = 2
PERM_ROWS = 32
COPY_ROWS = 512
LOG2_E = 1.4426950408889634
LN_2 = 0.6931471805599453
TINY = 1.1754944e-38

VMEM_LIMIT = 60 * 1024 * 1024

F32 = jnp.float32
BF16 = jnp.bfloat16


def _sigmoid(v):
    return 0.5 * jnp.tanh(0.5 * v) + 0.5


def _gelu_tanh(v):
    c = 0.7978845608028654
    return 0.5 * v * (1.0 + jnp.tanh(c * (v + 0.044715 * (v * v * v))))


def _layer_norm(y, g, b):
    mu = jnp.mean(y, axis=-1, keepdims=True)
    yc = y - mu
    var = jnp.mean(yc * yc, axis=-1, keepdims=True)
    return yc * lax.rsqrt(var + LN_EPS) * g + b


def _mm_act_kernel(*refs, act, n_casts, seg):
    x_ref, w_ref = refs[:2]
    cast_in = refs[2:2 + n_casts]
    o_ref = refs[2 + n_casts]
    cast_out = refs[3 + n_casts:3 + 2 * n_casts]
    w_bf = refs[-1]

    @pl.when(pl.program_id(1) == 0)
    def _():
        w_bf[...] = w_ref[0].astype(BF16)

    if seg:
        tj = x_ref.shape[2]
        x_tile = jnp.swapaxes(x_ref[0], 0, 1).reshape(NSEG * tj, x_ref.shape[3])
    else:
        x_tile = x_ref[...]
    acc = jnp.dot(x_tile.astype(BF16), w_bf[...], preferred_element_type=F32)
    if act == "gelu":
        acc = _gelu_tanh(acc)
    elif act == "sigmoid":
        acc = _sigmoid(acc)
    if seg:
        for q in range(o_ref.shape[0]):
            o_ref[q] = acc[:, q * LANES:(q + 1) * LANES]
    else:
        o_ref[...] = acc.astype(o_ref.dtype)
    for src, dst in zip(cast_in, cast_out):
        dst[...] = src[...].astype(BF16)


def _mm_act(x, w, layer, col0, n, act, casts=(), seg=None, tm=512, tn=2048):
    m, k = x.shape
    j0 = col0 // tn
    n_i = m // tm
    steps = (n // tn) * n_i
    cast_specs = [pl.BlockSpec((a.shape[0] // steps, a.shape[1]), lambda j, i: (j * n_i + i, 0)) for a in casts]
    if seg:
        batch, seq = seg
        tj = tm // NSEG
        n_jb = seq // NSEG // tj
        x = x.reshape(batch, NSEG, seq // NSEG, k)
        x_spec = pl.BlockSpec((1, NSEG, tj, k), lambda j, i: (i // n_jb, 0, i % n_jb, 0))
        o_spec = pl.BlockSpec((tn // LANES, tm, LANES), lambda j, i: (j, i, 0))
        o_shape = jax.ShapeDtypeStruct((n // LANES, m, LANES), F32)
    else:
        x_spec = pl.BlockSpec((tm, k), lambda j, i: (i, 0))
        o_spec = pl.BlockSpec((tm, tn), lambda j, i: (i, j))
        o_shape = jax.ShapeDtypeStruct((m, n), F32)
    return pl.pallas_call(
        functools.partial(_mm_act_kernel, act=act, n_casts=len(casts), seg=bool(seg)),
        grid=(n // tn, n_i),
        in_specs=[x_spec, pl.BlockSpec((1, k, tn), lambda j, i: (layer, 0, j0 + j))] + cast_specs,
        out_specs=[o_spec] + cast_specs,
        out_shape=[o_shape] + [jax.ShapeDtypeStruct(a.shape, BF16) for a in casts],
        scratch_shapes=[pltpu.VMEM((k, tn), BF16)],
        compiler_params=pltpu.CompilerParams(
            dimension_semantics=("arbitrary", "arbitrary"),
            vmem_limit_bytes=VMEM_LIMIT),
        name=f"mm_in_{act}",
    )(x, w, *casts)


def _load_slabs(buf, r, n):
    return jnp.concatenate([buf[q, pl.ds(r, n), :] for q in range(buf.shape[0])], axis=1)


def _store_slabs(buf, r, n, val):
    for q in range(buf.shape[0]):
        buf[q, pl.ds(r, n), :] = val[:, q * LANES:(q + 1) * LANES]


def _fill_ext(ext, u_ref, seq):
    h_rows = HALO * SUBLANES
    nq = ext.shape[0]

    def copy(bi, carry):
        r0 = pl.multiple_of(bi * COPY_ROWS, COPY_ROWS)
        for q in range(nq):
            ext[q, pl.ds(h_rows + r0, COPY_ROWS), :] = u_ref[q, pl.ds(r0, COPY_ROWS), :]
        return carry

    lax.fori_loop(0, seq // COPY_ROWS, copy, 0)

    sub = lax.broadcasted_iota(jnp.int32, (SUBLANES, LANES), 0)
    for q in range(nq):
        for m in range(HALO):
            nxt = pltpu.roll(u_ref[q, pl.ds(m * SUBLANES, SUBLANES), :], SUBLANES - 1, 0)
            ext[q, pl.ds(h_rows + seq + m * SUBLANES, SUBLANES), :] = jnp.where(sub == SUBLANES - 1, 0.0, nxt)
            prv = pltpu.roll(u_ref[q, pl.ds(seq - (m + 1) * SUBLANES, SUBLANES), :], 1, 0)
            ext[q, pl.ds(h_rows - (m + 1) * SUBLANES, SUBLANES), :] = jnp.where(sub == 0, 0.0, prv)


def _emit_time_order(src, o_ref, seq, post):
    seg_len = seq // NSEG

    def gather(bi, carry):
        j0 = pl.multiple_of(bi * PERM_ROWS, PERM_ROWS)
        for s in range(NSEG):
            val = jnp.concatenate(
                [src[q, pl.ds(j0 * NSEG + s, PERM_ROWS, stride=NSEG), :] for q in range(src.shape[0])], axis=1)
            r_time = s * seg_len + j0
            o_ref[0, pl.ds(r_time, PERM_ROWS), :] = post(val, r_time).astype(o_ref.dtype)
        return carry

    lax.fori_loop(0, seg_len // PERM_ROWS, gather, 0)


def _pool_kernel(u_ref, w_ref, s_ref, o_ref, ext, y_buf, *, seq, chunk):
    seg_len = seq // NSEG
    h_rows = HALO * SUBLANES
    nq = ext.shape[0]
    _fill_ext(ext, u_ref, seq)
    grp = pl.program_id(1)
    n_chunks = seq // chunk

    def body(win):
        half = win // 2

        def do_chunk(ci, carry=0):
            clipped = isinstance(ci, int)
            r0 = ci * chunk if clipped else pl.multiple_of(ci * chunk, chunk)
            base = r0 + h_rows
            if clipped:
                row = r0 + lax.broadcasted_iota(jnp.int32, (chunk, LANES), 0)
                t = (row & (SUBLANES - 1)) * seg_len + (row >> 3)
                cnt = jnp.minimum(t + half, seq) - jnp.maximum(t - half, 0)
                inv = 1.0 / cnt.astype(F32)
            else:
                inv = 1.0 / win
            d = []
            for q in range(nq):
                span_rows = (win - 1) * SUBLANES
                rows_in = ext[q, pl.ds(base - half * SUBLANES, chunk + span_rows), :]
                tot = rows_in
                step = SUBLANES
                while step <= half * SUBLANES:
                    keep = tot.shape[0] - step
                    tot = tot[:keep] + tot[step:step + keep]
                    step *= 2
                d.append(tot * inv - rows_in[half * SUBLANES:half * SUBLANES + chunk])
            d = jnp.concatenate(d, axis=1)
            y = jnp.dot(d.astype(BF16), w_ref[0], preferred_element_type=F32) * s_ref[0]
            _store_slabs(y_buf, r0, chunk, y)
            return carry

        assert chunk // SUBLANES >= half and n_chunks >= 2
        do_chunk(0)
        lax.fori_loop(1, n_chunks - 1, do_chunk, 0)
        do_chunk(n_chunks - 1)

    for gi, win in enumerate(POOL_WINDOWS):
        pl.when(grp == gi)(functools.partial(body, win))
    _emit_time_order(y_buf, o_ref, seq, lambda val, r_time: val)


def _pool_branch(z_seg, pool_w, pool_scale, batch, seq, chunk=512):
    n_groups, gw = pool_w.shape[0], pool_w.shape[1]
    nq = gw // LANES
    return pl.pallas_call(
        functools.partial(_pool_kernel, seq=seq, chunk=chunk),
        grid=(batch, n_groups),
        in_specs=[pl.BlockSpec((nq, seq, LANES), lambda b, g: (g, b, 0)),
                  pl.BlockSpec((1, gw, gw), lambda b, g: (g, 0, 0)),
                  pl.BlockSpec((1, 1, gw), lambda b, g: (g, 0, 0))],
        out_specs=pl.BlockSpec((1, seq, gw), lambda b, g: (b, 0, g)),
        out_shape=jax.ShapeDtypeStruct((batch, seq, n_groups * gw), BF16),
        scratch_shapes=[pltpu.VMEM((nq, seq + 2 * HALO * SUBLANES, LANES), F32),
                        pltpu.VMEM((nq, seq, LANES), F32)],
        compiler_params=pltpu.CompilerParams(
            dimension_semantics=("arbitrary", "arbitrary"),
            vmem_limit_bytes=VMEM_LIMIT),
        name="pool_branch",
    )(z_seg, pool_w, pool_scale.reshape(n_groups, 1, gw))


_P_CONV_W = 0
_P_CONV_B = CONV_WIDTH
_P_DIR = CONV_WIDTH + 1
_P_ROWS = 16


def _lru_kernel(u_ref, gate_ref, w_ref, p_ref, o_ref, ext, a_f, x_f, a_b, x_b, y_buf, summ, ent, *, seq, chunk):
    h_rows = HALO * SUBLANES
    c = LRU_BLOCK
    n_chunks = seq // chunk
    groups = chunk // SUBLANES
    _fill_ext(ext, u_ref, seq)

    conv_hw = [0.5 * p_ref[pl.ds(_P_CONV_W + k, 1), :] for k in range(CONV_WIDTH)]
    conv_hb = 0.5 * p_ref[pl.ds(_P_CONV_B, 1), :]
    hb_a, hb_x, c2 = [], [], []
    for d in range(2):
        hb_a.append(0.5 * p_ref[pl.ds(_P_DIR + 3 * d, 1), :])
        hb_x.append(0.5 * p_ref[pl.ds(_P_DIR + 3 * d + 1, 1), :])
        lam = p_ref[pl.ds(_P_DIR + 3 * d + 2, 1), :]
        c2.append((-0.5 * LRU_C * LOG2_E) * jax.nn.softplus(-lam))
    a_out = (a_f, a_b)
    x_out = (x_f, x_b)

    def rows(v, k):
        return v[k * SUBLANES:(k + 1) * SUBLANES]

    def summary_rows(ci):
        return pl.ds(pl.multiple_of(ci * SUBLANES, SUBLANES), SUBLANES)

    zeros = jnp.zeros((SUBLANES, c), F32)
    ones = jnp.ones((SUBLANES, c), F32)

    def gates(ci, carry):
        r0 = pl.multiple_of(ci * chunk, chunk)
        base = r0 + h_rows
        xh = conv_hb + _load_slabs(ext, base - CONV_LEFT * SUBLANES, chunk) * conv_hw[0]
        for k in range(1, CONV_WIDTH):
            xh = xh + _load_slabs(ext, base + (k - CONV_LEFT) * SUBLANES, chunk) * conv_hw[k]
        pre = jnp.dot(xh.astype(BF16), w_ref[0], preferred_element_type=F32)
        for d in range(2):
            t_r = jnp.tanh(pre[:, (2 * d) * c:(2 * d + 1) * c] + hb_a[d])
            t_i = jnp.tanh(pre[:, (2 * d + 1) * c:(2 * d + 2) * c] + hb_x[d])
            log2_a = c2[d] * t_r + c2[d]
            a = jnp.exp2(log2_a)
            one_m_a2 = (-1.0 - a * a) * jnp.tanh(LN_2 * log2_a)
            root = one_m_a2 * lax.rsqrt(jnp.maximum(one_m_a2, TINY))
            inp = root * (t_i * xh + xh)
            _store_slabs(a_out[d], r0, chunk, a)
            _store_slabs(x_out[d], r0, chunk, inp)
            h, p = zeros, ones
            for k in (range(groups) if d == 0 else range(groups - 1, -1, -1)):
                h = rows(a, k) * h + rows(inp, k)
                p = rows(a, k) * p
            summ[2 * d, summary_rows(ci), :] = h
            summ[2 * d + 1, summary_rows(ci), :] = p
        return carry

    lax.fori_loop(0, n_chunks, gates, 0)

    def summary(idx, ci):
        return summ[idx, ci * SUBLANES:(ci + 1) * SUBLANES, :]

    h_f, p_f, h_b, p_b = zeros, ones, zeros, ones
    for ci in range(n_chunks):
        cb = n_chunks - 1 - ci
        h_f = summary(0, ci) + summary(1, ci) * h_f
        p_f = summary(1, ci) * p_f
        h_b = summary(2, cb) + summary(3, cb) * h_b
        p_b = summary(3, cb) * p_b
    sub = lax.broadcasted_iota(jnp.int32, (SUBLANES, c), 0)
    e_f = zeros
    e_b = zeros
    for _ in range(NSEG - 1):
        e_f = jnp.where(sub == 0, 0.0, pltpu.roll(h_f + p_f * e_f, 1, 0))
        e_b = jnp.where(sub == SUBLANES - 1, 0.0, pltpu.roll(h_b + p_b * e_b, SUBLANES - 1, 0))
    for ci in range(n_chunks):
        cb = n_chunks - 1 - ci
        ent[0, ci * SUBLANES:(ci + 1) * SUBLANES, :] = e_f
        ent[1, cb * SUBLANES:(cb + 1) * SUBLANES, :] = e_b
        e_f = summary(0, ci) + summary(1, ci) * e_f
        e_b = summary(2, cb) + summary(3, cb) * e_b

    def replay(it, carry):
        for sub_i in range(REPLAY_CHUNKS):
            ci = it * REPLAY_CHUNKS + sub_i
            r0 = pl.multiple_of(ci * chunk, chunk)
            af, xf = _load_slabs(a_f, r0, chunk), _load_slabs(x_f, r0, chunk)
            ab, xb = _load_slabs(a_b, r0, chunk), _load_slabs(x_b, r0, chunk)
            h = ent[0, summary_rows(ci), :]
            fwd = []
            for k in range(groups):
                h = rows(af, k) * h + rows(xf, k)
                fwd.append(h)
            h = ent[1, summary_rows(ci), :]
            out = [None] * groups
            for k in range(groups - 1, -1, -1):
                h = rows(ab, k) * h + rows(xb, k)
                out[k] = h + fwd[k]
            _store_slabs(y_buf, r0, chunk, jnp.concatenate(out, axis=0))
        return carry

    lax.fori_loop(0, n_chunks // REPLAY_CHUNKS, replay, 0)

    _emit_time_order(y_buf, o_ref, seq, lambda val, r_time: val * gate_ref[0, pl.ds(r_time, PERM_ROWS), :])


def _lru_branch(z_seg, gate_act, w_gates, params, batch, seq, lru_col0, chunk=512):
    n_heads = w_gates.shape[0]
    c = LRU_BLOCK
    nq = c // LANES
    z3 = z_seg
    g3 = gate_act.reshape(batch, seq, gate_act.shape[1])
    col0 = lru_col0 // c
    n_chunks = seq // chunk
    seg_buf = pltpu.VMEM((nq, seq, LANES), F32)
    return pl.pallas_call(
        functools.partial(_lru_kernel, seq=seq, chunk=chunk),
        grid=(batch, n_heads),
        in_specs=[pl.BlockSpec((nq, seq, LANES), lambda b, h: (col0 + h, b, 0)),
                  pl.BlockSpec((1, seq, c), lambda b, h: (b, 0, h)),
                  pl.BlockSpec((1, c, 4 * c), lambda b, h: (h, 0, 0)),
                  pl.BlockSpec((_P_ROWS, c), lambda b, h: (0, h))],
        out_specs=pl.BlockSpec((1, seq, c), lambda b, h: (b, 0, h)),
        out_shape=jax.ShapeDtypeStruct((batch, seq, n_heads * c), BF16),
        scratch_shapes=[pltpu.VMEM((nq, seq + 2 * HALO * SUBLANES, LANES), F32)]
        + [seg_buf] * 5
        + [pltpu.VMEM((4, n_chunks * SUBLANES, c), F32), pltpu.VMEM((2, n_chunks * SUBLANES, c), F32)],
        compiler_params=pltpu.CompilerParams(
            dimension_semantics=("arbitrary", "arbitrary"),
            vmem_limit_bytes=VMEM_LIMIT),
        name="lru_branch",
    )(z3, g3, w_gates, params)


def _merge_kernel(yp_ref, yl_ref, g0_ref, g1_ref, x_ref, wp_ref, wl_ref, wo_ref, bo_ref, g_ref, b_ref,
                  of_ref, ob_ref, *, alpha):
    up_p = jnp.dot(yp_ref[...], wp_ref[...], preferred_element_type=F32)
    up_l = jnp.dot(yl_ref[...], wl_ref[...], preferred_element_type=F32)
    m = g0_ref[...] * up_p + g1_ref[...] * up_l
    mix = jnp.dot(m.astype(BF16), wo_ref[...], preferred_element_type=F32) + bo_ref[...]
    y = _layer_norm(alpha * x_ref[...] + mix, g_ref[...], b_ref[...])
    of_ref[...] = y
    ob_ref[...] = y.astype(BF16)


def _merge(y_pool, y_lru, g_act, x_rows, w_pool_up, w_lru_up, w_out, b_out, ln_g, ln_b, alpha, tm=256):
    t, d = x_rows.shape
    row = lambda i: (i, 0)
    const = lambda i: (0, 0)
    wspec = pl.BlockSpec((d, d), const, pipeline_mode=pl.Buffered(1))
    vspec = pl.BlockSpec((1, d), const)
    return pl.pallas_call(
        functools.partial(_merge_kernel, alpha=alpha),
        grid=(t // tm,),
        in_specs=[pl.BlockSpec((tm, d), row), pl.BlockSpec((tm, d), row),
                  pl.BlockSpec((tm, d), lambda i: (i, 0)), pl.BlockSpec((tm, d), lambda i: (i, 1)),
                  pl.BlockSpec((tm, d), row), wspec, wspec, wspec, vspec, vspec, vspec],
        out_specs=[pl.BlockSpec((tm, d), row), pl.BlockSpec((tm, d), row)],
        out_shape=[jax.ShapeDtypeStruct((t, d), F32), jax.ShapeDtypeStruct((t, d), BF16)],
        compiler_params=pltpu.CompilerParams(
            dimension_semantics=("arbitrary",), vmem_limit_bytes=VMEM_LIMIT),
        name="merge_out_ln",
    )(y_pool, y_lru, g_act, g_act, x_rows, w_pool_up, w_lru_up, w_out,
      b_out.reshape(1, d), ln_g.reshape(1, d), ln_b.reshape(1, d))


def _mlp_kernel(xb_ref, xf_ref, w1_ref, b1_ref, w2_ref, b2_ref, g_ref, b_ref, o_ref, *, alpha):
    f = pl.program_id(1)

    @pl.when(f == 0)
    def _():
        o_ref[...] = alpha * xf_ref[...] + b2_ref[...]

    h = jnp.dot(xb_ref[...], w1_ref[0].astype(BF16), preferred_element_type=F32) + b1_ref[...]
    h = jnp.square(jnp.maximum(h, 0.0))
    o_ref[...] += jnp.dot(h.astype(BF16), w2_ref[0].astype(BF16), preferred_element_type=F32)

    @pl.when(f == pl.num_programs(1) - 1)
    def _():
        o_ref[...] = _layer_norm(o_ref[...], g_ref[...], b_ref[...])


def _mlp(x_b, x_f, w1, layer, b1, w2, b2, ln_g, ln_b, alpha, tm=1024, tf=512):
    t, d = x_f.shape
    dff = w1.shape[2]
    return pl.pallas_call(
        functools.partial(_mlp_kernel, alpha=alpha),
        grid=(t // tm, dff // tf),
        in_specs=[pl.BlockSpec((tm, d), lambda i, f: (i, 0)),
                  pl.BlockSpec((tm, d), lambda i, f: (i, 0), pipeline_mode=pl.Buffered(1)),
                  pl.BlockSpec((1, d, tf), lambda i, f: (layer, 0, f)),
                  pl.BlockSpec((1, tf), lambda i, f: (0, f)),
                  pl.BlockSpec((1, tf, d), lambda i, f: (layer, f, 0)),
                  pl.BlockSpec((1, d), lambda i, f: (0, 0)),
                  pl.BlockSpec((1, d), lambda i, f: (0, 0)),
                  pl.BlockSpec((1, d), lambda i, f: (0, 0))],
        out_specs=pl.BlockSpec((tm, d), lambda i, f: (i, 0)),
        out_shape=jax.ShapeDtypeStruct((t, d), F32),
        compiler_params=pltpu.CompilerParams(
            dimension_semantics=("arbitrary", "arbitrary"), vmem_limit_bytes=VMEM_LIMIT),
        name="mlp_ln",
    )(x_b, x_f, w1, b1.reshape(1, dff), w2, b2.reshape(1, d), ln_g.reshape(1, d), ln_b.reshape(1, d))


def _layer(x_rows, batch, seq, alpha, layer, w_in, pool_w, pool_scale, conv_w, conv_b, lru_wa, lru_ba, lru_wx,
           lru_bx, lru_lambda, w_pool_up, w_lru_up, w_out, b_out, ln1_g, ln1_b, w_ff1, b_ff1, w_ff2, b_ff2,
           ln2_g, ln2_b):
    pool_width = pool_w.shape[0] * pool_w.shape[1]
    lru_width = conv_w.shape[1]
    o1, o2, o3 = pool_width, pool_width + lru_width, pool_width + 2 * lru_width

    z_a, = _mm_act(x_rows, w_in, layer, 0, o2, "none", seg=(batch, seq))
    g_act, = _mm_act(x_rows, w_in, layer, o3, w_in.shape[2] - o3, "sigmoid")
    gate_act, w_pool_up_b, w_lru_up_b, w_out_b = _mm_act(
        x_rows, w_in, layer, o2, o3 - o2, "gelu", casts=(w_pool_up, w_lru_up, w_out))

    y_pool = _pool_branch(z_a, pool_w.astype(BF16), pool_scale, batch, seq)

    w_gates = jnp.concatenate([lru_wa[0], lru_wx[0], lru_wa[1], lru_wx[1]], axis=-1).astype(BF16)
    params = jnp.concatenate(
        [conv_w, conv_b[None], lru_ba[0][None], lru_bx[0][None], lru_lambda[0][None],
         lru_ba[1][None], lru_bx[1][None], lru_lambda[1][None]], axis=0).astype(F32)
    params = jnp.pad(params, ((0, _P_ROWS - params.shape[0]), (0, 0)))
    y_lru = _lru_branch(z_a, gate_act, w_gates, params, batch, seq, o1)

    t = batch * seq
    x1_f, x1_b = _merge(y_pool.reshape(t, -1), y_lru.reshape(t, -1), g_act, x_rows,
                        w_pool_up_b, w_lru_up_b, w_out_b,
                        b_out, ln1_g, ln1_b, alpha)
    return _mlp(x1_b, x1_f, w_ff1, layer, b_ff1, w_ff2, b_ff2, ln2_g, ln2_b, alpha)


def kernel(x, w_in, pool_w, pool_scale, conv_w, conv_b, lru_wa, lru_ba, lru_wx, lru_bx, lru_lambda, w_pool_up, w_lru_up, w_out, b_out, ln1_g, ln1_b, w_ff1, b_ff1, w_ff2, b_ff2, ln2_g, ln2_b):
    batch, seq, d = x.shape
    depth = w_in.shape[0]
    alpha = (2.0 * depth) ** 0.25
    rows = x.reshape(batch * seq, d)
    for l in range(depth):
        rows = _layer(rows, batch, seq, alpha, l, w_in, pool_w[l], pool_scale[l], conv_w[l], conv_b[l],
                      lru_wa[l], lru_ba[l], lru_wx[l], lru_bx[l], lru_lambda[l], w_pool_up[l], w_lru_up[l],
                      w_out[l], b_out[l], ln1_g[l], ln1_b[l], w_ff1, b_ff1[l], w_ff2, b_ff2[l],
                      ln2_g[l], ln2_b[l])
    return rows.reshape(batch, seq, d)
```

```python
import functools

import jax
import jax.numpy as jnp
from jax import lax
from jax.experimental import pallas as pl
from jax.experimental.pallas import tpu as pltpu

SUBLANES = 8
LANES = 128
NSEG = SUBLANES

POOL_WINDOWS = (2, 4, 8, 16)
LRU_BLOCK = 256
CONV_WIDTH = 4
CONV_LEFT = CONV_WIDTH // 2
LRU_C = 8.0
LN_EPS = 1e-5
HALO = 8
REPLAY_CHUNKS = 2
COPY_ROWS = 512
LOG2_E = 1.4426950408889634
LN_2 = 0.6931471805599453
TINY = 1.1754944e-38

VMEM_LIMIT = 60 * 1024 * 1024
MERGE_VMEM_LIMIT = 62 * 1024 * 1024

F32 = jnp.float32
BF16 = jnp.bfloat16


def _sigmoid(v):
    return 0.5 * jnp.tanh(0.5 * v) + 0.5


def _gelu_tanh(v):
    c = 0.7978845608028654
    return 0.5 * v * (1.0 + jnp.tanh(c * (v + 0.044715 * (v * v * v))))


def _layer_norm(y, g, b):
    mu = jnp.mean(y, axis=-1, keepdims=True)
    yc = y - mu
    var = jnp.mean(yc * yc, axis=-1, keepdims=True)
    return yc * lax.rsqrt(var + LN_EPS) * g + b


def _mm_act_kernel(*refs, act, n_casts, seg):
    x_ref, w_ref = refs[:2]
    cast_in = refs[2:2 + n_casts]
    o_ref = refs[2 + n_casts]
    cast_out = refs[3 + n_casts:3 + 2 * n_casts]
    w_bf = refs[-1]

    @pl.when(pl.program_id(1) == 0)
    def _():
        w_bf[...] = w_ref[0].astype(BF16)

    if seg:
        tj = x_ref.shape[2]
        x_tile = jnp.swapaxes(x_ref[0], 0, 1).reshape(NSEG * tj, x_ref.shape[3])
    else:
        x_tile = x_ref[...]
    acc = jnp.dot(x_tile.astype(BF16), w_bf[...], preferred_element_type=F32)
    if act == "gelu":
        acc = _gelu_tanh(acc)
    elif act == "sigmoid":
        acc = _sigmoid(acc)
    if seg:
        for q in range(o_ref.shape[0]):
            o_ref[q] = acc[:, q * LANES:(q + 1) * LANES]
    else:
        o_ref[...] = acc.astype(o_ref.dtype)
    for src, dst in zip(cast_in, cast_out):
        dst[...] = src[...].astype(BF16)


def _mm_act(x, w, layer, col0, n, act, casts=(), seg=None, tm=512, tn=2048):
    m, k = x.shape
    j0 = col0 // tn
    n_i = m // tm
    steps = (n // tn) * n_i
    cast_specs = [pl.BlockSpec((a.shape[0] // steps, a.shape[1]), lambda j, i: (j * n_i + i, 0)) for a in casts]
    if seg:
        batch, seq = seg
        tj = tm // NSEG
        n_jb = seq // NSEG // tj
        x = x.reshape(batch, NSEG, seq // NSEG, k)
        x_spec = pl.BlockSpec((1, NSEG, tj, k), lambda j, i: (i // n_jb, 0, i % n_jb, 0))
        o_spec = pl.BlockSpec((tn // LANES, tm, LANES), lambda j, i: (j, i, 0))
        o_shape = jax.ShapeDtypeStruct((n // LANES, m, LANES), F32)
    else:
        x_spec = pl.BlockSpec((tm, k), lambda j, i: (i, 0))
        o_spec = pl.BlockSpec((tm, tn), lambda j, i: (i, j))
        o_shape = jax.ShapeDtypeStruct((m, n), F32)
    return pl.pallas_call(
        functools.partial(_mm_act_kernel, act=act, n_casts=len(casts), seg=bool(seg)),
        grid=(n // tn, n_i),
        in_specs=[x_spec, pl.BlockSpec((1, k, tn), lambda j, i: (layer, 0, j0 + j))] + cast_specs,
        out_specs=[o_spec] + cast_specs,
        out_shape=[o_shape] + [jax.ShapeDtypeStruct(a.shape, BF16) for a in casts],
        scratch_shapes=[pltpu.VMEM((k, tn), BF16)],
        compiler_params=pltpu.CompilerParams(
            dimension_semantics=("arbitrary", "arbitrary"),
            vmem_limit_bytes=VMEM_LIMIT),
        name=f"mm_in_{act}",
    )(x, w, *casts)


def _load_slabs(buf, r, n):
    return jnp.concatenate([buf[q, pl.ds(r, n), :] for q in range(buf.shape[0])], axis=1)


def _store_slabs(buf, r, n, val):
    for q in range(buf.shape[0]):
        buf[q, pl.ds(r, n), :] = val[:, q * LANES:(q + 1) * LANES]


def _fill_ext(ext, u_ref, seq):
    h_rows = HALO * SUBLANES
    nq = ext.shape[0]

    def copy(bi, carry):
        r0 = pl.multiple_of(bi * COPY_ROWS, COPY_ROWS)
        for q in range(nq):
            ext[q, pl.ds(h_rows + r0, COPY_ROWS), :] = u_ref[q, pl.ds(r0, COPY_ROWS), :]
        return carry

    lax.fori_loop(0, seq // COPY_ROWS, copy, 0)

    sub = lax.broadcasted_iota(jnp.int32, (SUBLANES, LANES), 0)
    for q in range(nq):
        for m in range(HALO):
            nxt = pltpu.roll(u_ref[q, pl.ds(m * SUBLANES, SUBLANES), :], SUBLANES - 1, 0)
            ext[q, pl.ds(h_rows + seq + m * SUBLANES, SUBLANES), :] = jnp.where(sub == SUBLANES - 1, 0.0, nxt)
            prv = pltpu.roll(u_ref[q, pl.ds(seq - (m + 1) * SUBLANES, SUBLANES), :], 1, 0)
            ext[q, pl.ds(h_rows - (m + 1) * SUBLANES, SUBLANES), :] = jnp.where(sub == 0, 0.0, prv)


def _pool_kernel(u_ref, w_ref, s_ref, o_ref, ext, *, seq, chunk):
    seg_len = seq // NSEG
    h_rows = HALO * SUBLANES
    nq = ext.shape[0]
    _fill_ext(ext, u_ref, seq)
    grp = pl.program_id(1)
    n_chunks = seq // chunk

    def body(win):
        half = win // 2

        def do_chunk(ci, carry=0):
            clipped = isinstance(ci, int)
            r0 = ci * chunk if clipped else pl.multiple_of(ci * chunk, chunk)
            base = r0 + h_rows
            if clipped:
                row = r0 + lax.broadcasted_iota(jnp.int32, (chunk, LANES), 0)
                t = (row & (SUBLANES - 1)) * seg_len + (row >> 3)
                cnt = jnp.minimum(t + half, seq) - jnp.maximum(t - half, 0)
                inv = 1.0 / cnt.astype(F32)
            else:
                inv = 1.0 / win
            d = []
            for q in range(nq):
                span_rows = (win - 1) * SUBLANES
                rows_in = ext[q, pl.ds(base - half * SUBLANES, chunk + span_rows), :]
                tot = rows_in
                step = SUBLANES
                while step <= half * SUBLANES:
                    keep = tot.shape[0] - step
                    tot = tot[:keep] + tot[step:step + keep]
                    step *= 2
                d.append(tot * inv - rows_in[half * SUBLANES:half * SUBLANES + chunk])
            d = jnp.concatenate(d, axis=1)
            y = jnp.dot(d.astype(BF16), w_ref[0], preferred_element_type=F32) * s_ref[0]
            o_ref[0, pl.ds(r0, chunk), :] = y.astype(o_ref.dtype)
            return carry

        assert chunk // SUBLANES >= half and n_chunks >= 2
        do_chunk(0)
        lax.fori_loop(1, n_chunks - 1, do_chunk, 0)
        do_chunk(n_chunks - 1)

    for gi, win in enumerate(POOL_WINDOWS):
        pl.when(grp == gi)(functools.partial(body, win))


def _pool_branch(z_seg, pool_w, pool_scale, batch, seq, chunk=512):
    n_groups, gw = pool_w.shape[0], pool_w.shape[1]
    nq = gw // LANES
    return pl.pallas_call(
        functools.partial(_pool_kernel, seq=seq, chunk=chunk),
        grid=(batch, n_groups),
        in_specs=[pl.BlockSpec((nq, seq, LANES), lambda b, g: (g, b, 0)),
                  pl.BlockSpec((1, gw, gw), lambda b, g: (g, 0, 0)),
                  pl.BlockSpec((1, 1, gw), lambda b, g: (g, 0, 0))],
        out_specs=pl.BlockSpec((1, seq, gw), lambda b, g: (b, 0, g)),
        out_shape=jax.ShapeDtypeStruct((batch, seq, n_groups * gw), F32),
        scratch_shapes=[pltpu.VMEM((nq, seq + 2 * HALO * SUBLANES, LANES), F32)],
        compiler_params=pltpu.CompilerParams(
            dimension_semantics=("arbitrary", "arbitrary"),
            vmem_limit_bytes=VMEM_LIMIT),
        name="pool_branch",
    )(z_seg, pool_w, pool_scale.reshape(n_groups, 1, gw))


_P_CONV_W = 0
_P_CONV_B = CONV_WIDTH
_P_DIR = CONV_WIDTH + 1
_P_ROWS = 16


def _lru_kernel(u_ref, w_ref, p_ref, o_ref, ext, a_f, x_f, a_b, x_b, summ, ent, *, seq, chunk):
    h_rows = HALO * SUBLANES
    c = LRU_BLOCK
    n_chunks = seq // chunk
    groups = chunk // SUBLANES
    _fill_ext(ext, u_ref, seq)

    conv_hw = [0.5 * p_ref[pl.ds(_P_CONV_W + k, 1), :] for k in range(CONV_WIDTH)]
    conv_hb = 0.5 * p_ref[pl.ds(_P_CONV_B, 1), :]
    hb_a, hb_x, c2 = [], [], []
    for d in range(2):
        hb_a.append(0.5 * p_ref[pl.ds(_P_DIR + 3 * d, 1), :])
        hb_x.append(0.5 * p_ref[pl.ds(_P_DIR + 3 * d + 1, 1), :])
        lam = p_ref[pl.ds(_P_DIR + 3 * d + 2, 1), :]
        c2.append((-0.5 * LRU_C * LOG2_E) * jax.nn.softplus(-lam))
    a_out = (a_f, a_b)
    x_out = (x_f, x_b)

    def rows(v, k):
        return v[k * SUBLANES:(k + 1) * SUBLANES]

    def summary_rows(ci):
        return pl.ds(pl.multiple_of(ci * SUBLANES, SUBLANES), SUBLANES)

    zeros = jnp.zeros((SUBLANES, c), F32)
    ones = jnp.ones((SUBLANES, c), F32)

    def gates(ci, carry):
        r0 = pl.multiple_of(ci * chunk, chunk)
        base = r0 + h_rows
        xh = conv_hb + _load_slabs(ext, base - CONV_LEFT * SUBLANES, chunk) * conv_hw[0]
        for k in range(1, CONV_WIDTH):
            xh = xh + _load_slabs(ext, base + (k - CONV_LEFT) * SUBLANES, chunk) * conv_hw[k]
        pre = jnp.dot(xh.astype(BF16), w_ref[0], preferred_element_type=F32)
        for d in range(2):
            t_r = jnp.tanh(pre[:, (2 * d) * c:(2 * d + 1) * c] + hb_a[d])
            t_i = jnp.tanh(pre[:, (2 * d + 1) * c:(2 * d + 2) * c] + hb_x[d])
            log2_a = c2[d] * t_r + c2[d]
            a = jnp.exp2(log2_a)
            one_m_a2 = (-1.0 - a * a) * jnp.tanh(LN_2 * log2_a)
            root = one_m_a2 * lax.rsqrt(jnp.maximum(one_m_a2, TINY))
            inp = root * (t_i * xh + xh)
            _store_slabs(a_out[d], r0, chunk, a)
            _store_slabs(x_out[d], r0, chunk, inp)
            h, p = zeros, ones
            for k in (range(groups) if d == 0 else range(groups - 1, -1, -1)):
                h = rows(a, k) * h + rows(inp, k)
                p = rows(a, k) * p
            summ[2 * d, summary_rows(ci), :] = h
            summ[2 * d + 1, summary_rows(ci), :] = p
        return carry

    lax.fori_loop(0, n_chunks, gates, 0)

    def summary(idx, ci):
        return summ[idx, ci * SUBLANES:(ci + 1) * SUBLANES, :]

    h_f, p_f, h_b, p_b = zeros, ones, zeros, ones
    for ci in range(n_chunks):
        cb = n_chunks - 1 - ci
        h_f = summary(0, ci) + summary(1, ci) * h_f
        p_f = summary(1, ci) * p_f
        h_b = summary(2, cb) + summary(3, cb) * h_b
        p_b = summary(3, cb) * p_b
    sub = lax.broadcasted_iota(jnp.int32, (SUBLANES, c), 0)
    e_f = zeros
    e_b = zeros
    for _ in range(NSEG - 1):
        e_f = jnp.where(sub == 0, 0.0, pltpu.roll(h_f + p_f * e_f, 1, 0))
        e_b = jnp.where(sub == SUBLANES - 1, 0.0, pltpu.roll(h_b + p_b * e_b, SUBLANES - 1, 0))
    for ci in range(n_chunks):
        cb = n_chunks - 1 - ci
        ent[0, ci * SUBLANES:(ci + 1) * SUBLANES, :] = e_f
        ent[1, cb * SUBLANES:(cb + 1) * SUBLANES, :] = e_b
        e_f = summary(0, ci) + summary(1, ci) * e_f
        e_b = summary(2, cb) + summary(3, cb) * e_b

    def replay(it, carry):
        for sub_i in range(REPLAY_CHUNKS):
            ci = it * REPLAY_CHUNKS + sub_i
            r0 = pl.multiple_of(ci * chunk, chunk)
            af, xf = _load_slabs(a_f, r0, chunk), _load_slabs(x_f, r0, chunk)
            ab, xb = _load_slabs(a_b, r0, chunk), _load_slabs(x_b, r0, chunk)
            h = ent[0, summary_rows(ci), :]
            fwd = []
            for k in range(groups):
                h = rows(af, k) * h + rows(xf, k)
                fwd.append(h)
            h = ent[1, summary_rows(ci), :]
            out = [None] * groups
            for k in range(groups - 1, -1, -1):
                h = rows(ab, k) * h + rows(xb, k)
                out[k] = h + fwd[k]
            o_ref[0, pl.ds(r0, chunk), :] = jnp.concatenate(out, axis=0).astype(o_ref.dtype)
        return carry

    lax.fori_loop(0, n_chunks // REPLAY_CHUNKS, replay, 0)


def _lru_branch(z_seg, w_gates, params, batch, seq, lru_col0, chunk=512):
    n_heads = w_gates.shape[0]
    c = LRU_BLOCK
    nq = c // LANES
    z3 = z_seg
    col0 = lru_col0 // c
    n_chunks = seq // chunk
    seg_buf = pltpu.VMEM((nq, seq, LANES), F32)
    return pl.pallas_call(
        functools.partial(_lru_kernel, seq=seq, chunk=chunk),
        grid=(batch, n_heads),
        in_specs=[pl.BlockSpec((nq, seq, LANES), lambda b, h: (col0 + h, b, 0)),
                  pl.BlockSpec((1, c, 4 * c), lambda b, h: (h, 0, 0)),
                  pl.BlockSpec((_P_ROWS, c), lambda b, h: (0, h))],
        out_specs=pl.BlockSpec((1, seq, c), lambda b, h: (b, 0, h)),
        out_shape=jax.ShapeDtypeStruct((batch, seq, n_heads * c), F32),
        scratch_shapes=[pltpu.VMEM((nq, seq + 2 * HALO * SUBLANES, LANES), F32)]
        + [seg_buf] * 4
        + [pltpu.VMEM((4, n_chunks * SUBLANES, c), F32), pltpu.VMEM((2, n_chunks * SUBLANES, c), F32)],
        compiler_params=pltpu.CompilerParams(
            dimension_semantics=("arbitrary", "arbitrary"),
            vmem_limit_bytes=VMEM_LIMIT),
        name="lru_branch",
    )(z3, w_gates, params)


def _merge_kernel(yp_ref, yl_ref, gate_ref, g0_ref, g1_ref, x_ref, wp_ref, wl_ref, wo_ref, bo_ref, g_ref, b_ref,
                  of_ref, ob_ref, *, alpha):
    _, n_seg, tj, d = x_ref.shape
    tm = n_seg * tj

    def to_time(v):
        return jnp.swapaxes(v.reshape(tj, n_seg, d), 0, 1).reshape(tm, d)

    y_lru = to_time(yl_ref[...]) * gate_ref[0].reshape(tm, d)
    up_p = jnp.dot(to_time(yp_ref[...]).astype(BF16), wp_ref[...], preferred_element_type=F32)
    up_l = jnp.dot(y_lru.astype(BF16), wl_ref[...], preferred_element_type=F32)
    m = g0_ref[0].reshape(tm, d) * up_p + g1_ref[0].reshape(tm, d) * up_l
    mix = jnp.dot(m.astype(BF16), wo_ref[...], preferred_element_type=F32) + bo_ref[...]
    y = _layer_norm(alpha * x_ref[0].reshape(tm, d) + mix, g_ref[...], b_ref[...])
    of_ref[0] = y.reshape(n_seg, tj, d)
    ob_ref[0] = y.astype(BF16).reshape(n_seg, tj, d)


def _merge(y_pool, y_lru, gate_act, g_act, x_rows, w_pool_up, w_lru_up, w_out, b_out, ln_g, ln_b, alpha, batch, seq,
           tm=256):
    t, d = x_rows.shape
    seg_len = seq // NSEG
    tj = tm // NSEG
    n_jb = seg_len // tj
    row = lambda i: (i, 0)
    const = lambda i: (0, 0)
    wspec = pl.BlockSpec((d, d), const, pipeline_mode=pl.Buffered(1))
    vspec = pl.BlockSpec((1, d), const)

    def tspec(col):
        return pl.BlockSpec((1, NSEG, tj, d), lambda i: (i // n_jb, 0, i % n_jb, col))

    g4 = g_act.reshape(batch, NSEG, seg_len, g_act.shape[1])
    x1_f, x1_b = pl.pallas_call(
        functools.partial(_merge_kernel, alpha=alpha),
        grid=(t // tm,),
        in_specs=[pl.BlockSpec((tm, d), row), pl.BlockSpec((tm, d), row),
                  tspec(0), tspec(0), tspec(1), tspec(0), wspec, wspec, wspec, vspec, vspec, vspec],
        out_specs=[tspec(0), tspec(0)],
        out_shape=[jax.ShapeDtypeStruct((batch, NSEG, seg_len, d), F32),
                   jax.ShapeDtypeStruct((batch, NSEG, seg_len, d), BF16)],
        compiler_params=pltpu.CompilerParams(
            dimension_semantics=("arbitrary",), vmem_limit_bytes=MERGE_VMEM_LIMIT),
        name="merge_out_ln",
    )(y_pool, y_lru, gate_act.reshape(batch, NSEG, seg_len, d), g4, g4, x_rows.reshape(batch, NSEG, seg_len, d),
      w_pool_up, w_lru_up, w_out,
      b_out.reshape(1, d), ln_g.reshape(1, d), ln_b.reshape(1, d))
    return x1_f.reshape(t, d), x1_b.reshape(t, d)


def _mlp_kernel(xb_ref, xf_ref, w1_ref, b1_ref, w2_ref, b2_ref, g_ref, b_ref, o_ref, *, alpha):
    f = pl.program_id(1)

    @pl.when(f == 0)
    def _():
        o_ref[...] = alpha * xf_ref[...] + b2_ref[...]

    h = jnp.dot(xb_ref[...], w1_ref[0].astype(BF16), preferred_element_type=F32) + b1_ref[...]
    h = jnp.square(jnp.maximum(h, 0.0))
    o_ref[...] += jnp.dot(h.astype(BF16), w2_ref[0].astype(BF16), preferred_element_type=F32)

    @pl.when(f == pl.num_programs(1) - 1)
    def _():
        o_ref[...] = _layer_norm(o_ref[...], g_ref[...], b_ref[...])


def _mlp(x_b, x_f, w1, layer, b1, w2, b2, ln_g, ln_b, alpha, tm=1024, tf=512):
    t, d = x_f.shape
    dff = w1.shape[2]
    return pl.pallas_call(
        functools.partial(_mlp_kernel, alpha=alpha),
        grid=(t // tm, dff // tf),
        in_specs=[pl.BlockSpec((tm, d), lambda i, f: (i, 0)),
                  pl.BlockSpec((tm, d), lambda i, f: (i, 0), pipeline_mode=pl.Buffered(1)),
                  pl.BlockSpec((1, d, tf), lambda i, f: (layer, 0, f)),
                  pl.BlockSpec((1, tf), lambda i, f: (0, f)),
                  pl.BlockSpec((1, tf, d), lambda i, f: (layer, f, 0)),
                  pl.BlockSpec((1, d), lambda i, f: (0, 0)),
                  pl.BlockSpec((1, d), lambda i, f: (0, 0)),
                  pl.BlockSpec((1, d), lambda i, f: (0, 0))],
        out_specs=pl.BlockSpec((tm, d), lambda i, f: (i, 0)),
        out_shape=jax.ShapeDtypeStruct((t, d), F32),
        compiler_params=pltpu.CompilerParams(
            dimension_semantics=("arbitrary", "arbitrary"), vmem_limit_bytes=VMEM_LIMIT),
        name="mlp_ln",
    )(x_b, x_f, w1, b1.reshape(1, dff), w2, b2.reshape(1, d), ln_g.reshape(1, d), ln_b.reshape(1, d))


def _layer(x_rows, batch, seq, alpha, layer, w_in, pool_w, pool_scale, conv_w, conv_b, lru_wa, lru_ba, lru_wx,
           lru_bx, lru_lambda, w_pool_up, w_lru_up, w_out, b_out, ln1_g, ln1_b, w_ff1, b_ff1, w_ff2, b_ff2,
           ln2_g, ln2_b):
    pool_width = pool_w.shape[0] * pool_w.shape[1]
    lru_width = conv_w.shape[1]
    o1, o2, o3 = pool_width, pool_width + lru_width, pool_width + 2 * lru_width

    z_a, = _mm_act(x_rows, w_in, layer, 0, o2, "none", seg=(batch, seq))
    g_act, = _mm_act(x_rows, w_in, layer, o3, w_in.shape[2] - o3, "sigmoid")
    gate_act, w_pool_up_b, w_lru_up_b, w_out_b = _mm_act(
        x_rows, w_in, layer, o2, o3 - o2, "gelu", casts=(w_pool_up, w_lru_up, w_out))

    y_pool = _pool_branch(z_a, pool_w.astype(BF16), pool_scale, batch, seq)

    w_gates = jnp.concatenate([lru_wa[0], lru_wx[0], lru_wa[1], lru_wx[1]], axis=-1).astype(BF16)
    params = jnp.concatenate(
        [conv_w, conv_b[None], lru_ba[0][None], lru_bx[0][None], lru_lambda[0][None],
         lru_ba[1][None], lru_bx[1][None], lru_lambda[1][None]], axis=0).astype(F32)
    params = jnp.pad(params, ((0, _P_ROWS - params.shape[0]), (0, 0)))
    y_lru = _lru_branch(z_a, w_gates, params, batch, seq, o1)

    t = batch * seq
    x1_f, x1_b = _merge(y_pool.reshape(t, -1), y_lru.reshape(t, -1), gate_act, g_act, x_rows,
                        w_pool_up_b, w_lru_up_b, w_out_b,
                        b_out, ln1_g, ln1_b, alpha, batch, seq)
    return _mlp(x1_b, x1_f, w_ff1, layer, b_ff1, w_ff2, b_ff2, ln2_g, ln2_b, alpha)


def kernel(x, w_in, pool_w, pool_scale, conv_w, conv_b, lru_wa, lru_ba, lru_wx, lru_bx, lru_lambda, w_pool_up, w_lru_up, w_out, b_out, ln1_g, ln1_b, w_ff1, b_ff1, w_ff2, b_ff2, ln2_g, ln2_b):
    batch, seq, d = x.shape
    depth = w_in.shape[0]
    alpha = (2.0 * depth) ** 0.25
    rows = x.reshape(batch * seq, d)
    for l in range(depth):
        rows = _layer(rows, batch, seq, alpha, l, w_in, pool_w[l], pool_scale[l], conv_w[l], conv_b[l],
                      lru_wa[l], lru_ba[l], lru_wx[l], lru_bx[l], lru_lambda[l], w_pool_up[l], w_lru_up[l],
                      w_out[l], b_out[l], ln1_g[l], ln1_b[l], w_ff1, b_ff1[l], w_ff2, b_ff2[l],
                      ln2_g[l], ln2_b[l])
    return rows.reshape(batch, seq, d)
```

```python
import functools

import jax
import jax.numpy as jnp
from jax import lax
from jax.experimental import pallas as pl
from jax.experimental.pallas import tpu as pltpu

SUBLANES = 8
LANES = 128
NSEG = SUBLANES

POOL_WINDOWS = (2, 4, 8, 16)
LRU_BLOCK = 256
CONV_WIDTH = 4
CONV_LEFT = CONV_WIDTH // 2
LRU_C = 8.0
LN_EPS = 1e-5
HALO = 8
REPLAY_CHUNKS = 2
COPY_ROWS = 512
LOG2_E = 1.4426950408889634
LN_2 = 0.6931471805599453
TINY = 1.1754944e-38

VMEM_LIMIT = 60 * 1024 * 1024
WIDE_VMEM_LIMIT = 62 * 1024 * 1024

F32 = jnp.float32
BF16 = jnp.bfloat16


def _sigmoid(v):
    return 0.5 * jnp.tanh(0.5 * v) + 0.5


def _gelu_tanh(v):
    c = 0.7978845608028654
    return 0.5 * v * (1.0 + jnp.tanh(c * (v + 0.044715 * (v * v * v))))


def _layer_norm(y, g, b):
    mu = jnp.mean(y, axis=-1, keepdims=True)
    yc = y - mu
    var = jnp.mean(yc * yc, axis=-1, keepdims=True)
    return yc * lax.rsqrt(var + LN_EPS) * g + b


def _mm_act_kernel(*refs, act, n_casts, seg):
    x_ref, w_ref = refs[:2]
    cast_in = refs[2:2 + n_casts]
    o_ref = refs[2 + n_casts]
    cast_out = refs[3 + n_casts:3 + 2 * n_casts]
    w_bf = refs[-1]

    @pl.when(pl.program_id(1) == 0)
    def _():
        w_bf[...] = w_ref[0].astype(BF16)

    if seg:
        tj = x_ref.shape[2]
        x_tile = jnp.swapaxes(x_ref[0], 0, 1).reshape(NSEG * tj, x_ref.shape[3])
    else:
        x_tile = x_ref[...]
    acc = jnp.dot(x_tile.astype(BF16), w_bf[...], preferred_element_type=F32)
    if act == "gelu":
        acc = _gelu_tanh(acc)
    elif act == "sigmoid":
        acc = _sigmoid(acc)
    if seg:
        for q in range(o_ref.shape[0]):
            o_ref[q] = acc[:, q * LANES:(q + 1) * LANES]
    else:
        o_ref[...] = acc.astype(o_ref.dtype)
    for src, dst in zip(cast_in, cast_out):
        dst[...] = src[...].astype(BF16)


def _mm_act(x, w, layer, col0, n, act, casts=(), seg=None, tm=512, tn=2048):
    m, k = x.shape
    j0 = col0 // tn
    n_i = m // tm
    steps = (n // tn) * n_i
    cast_specs = [pl.BlockSpec((a.shape[0] // steps, a.shape[1]), lambda j, i: (j * n_i + i, 0)) for a in casts]
    if seg:
        batch, seq = seg
        tj = tm // NSEG
        n_jb = seq // NSEG // tj
        x = x.reshape(batch, NSEG, seq // NSEG, k)
        x_spec = pl.BlockSpec((1, NSEG, tj, k), lambda j, i: (i // n_jb, 0, i % n_jb, 0))
        o_spec = pl.BlockSpec((tn // LANES, tm, LANES), lambda j, i: (j, i, 0))
        o_shape = jax.ShapeDtypeStruct((n // LANES, m, LANES), F32)
    else:
        x_spec = pl.BlockSpec((tm, k), lambda j, i: (i, 0))
        o_spec = pl.BlockSpec((tm, tn), lambda j, i: (i, j))
        o_shape = jax.ShapeDtypeStruct((m, n), F32)
    return pl.pallas_call(
        functools.partial(_mm_act_kernel, act=act, n_casts=len(casts), seg=bool(seg)),
        grid=(n // tn, n_i),
        in_specs=[x_spec, pl.BlockSpec((1, k, tn), lambda j, i: (layer, 0, j0 + j))] + cast_specs,
        out_specs=[o_spec] + cast_specs,
        out_shape=[o_shape] + [jax.ShapeDtypeStruct(a.shape, BF16) for a in casts],
        scratch_shapes=[pltpu.VMEM((k, tn), BF16)],
        compiler_params=pltpu.CompilerParams(
            dimension_semantics=("arbitrary", "arbitrary"),
            vmem_limit_bytes=VMEM_LIMIT),
        name=f"mm_in_{act}",
    )(x, w, *casts)


def _load_slabs(buf, r, n):
    return jnp.concatenate([buf[q, pl.ds(r, n), :] for q in range(buf.shape[0])], axis=1)


def _store_slabs(buf, r, n, val):
    for q in range(buf.shape[0]):
        buf[q, pl.ds(r, n), :] = val[:, q * LANES:(q + 1) * LANES]


def _fill_ext(ext, u_ref, seq):
    h_rows = HALO * SUBLANES
    nq = ext.shape[0]

    def copy(bi, carry):
        r0 = pl.multiple_of(bi * COPY_ROWS, COPY_ROWS)
        for q in range(nq):
            ext[q, pl.ds(h_rows + r0, COPY_ROWS), :] = u_ref[q, pl.ds(r0, COPY_ROWS), :]
        return carry

    lax.fori_loop(0, seq // COPY_ROWS, copy, 0)

    sub = lax.broadcasted_iota(jnp.int32, (SUBLANES, LANES), 0)
    for q in range(nq):
        for m in range(HALO):
            nxt = pltpu.roll(u_ref[q, pl.ds(m * SUBLANES, SUBLANES), :], SUBLANES - 1, 0)
            ext[q, pl.ds(h_rows + seq + m * SUBLANES, SUBLANES), :] = jnp.where(sub == SUBLANES - 1, 0.0, nxt)
            prv = pltpu.roll(u_ref[q, pl.ds(seq - (m + 1) * SUBLANES, SUBLANES), :], 1, 0)
            ext[q, pl.ds(h_rows - (m + 1) * SUBLANES, SUBLANES), :] = jnp.where(sub == 0, 0.0, prv)


def _pool_kernel(u_ref, w_ref, s_ref, o_ref, ext, *, seq, chunk):
    seg_len = seq // NSEG
    h_rows = HALO * SUBLANES
    nq = ext.shape[0]
    _fill_ext(ext, u_ref, seq)
    grp = pl.program_id(1)
    n_chunks = seq // chunk

    def body(win):
        half = win // 2

        def do_chunk(ci, carry=0):
            clipped = isinstance(ci, int)
            r0 = ci * chunk if clipped else pl.multiple_of(ci * chunk, chunk)
            base = r0 + h_rows
            if clipped:
                row = r0 + lax.broadcasted_iota(jnp.int32, (chunk, LANES), 0)
                t = (row & (SUBLANES - 1)) * seg_len + (row >> 3)
                cnt = jnp.minimum(t + half, seq) - jnp.maximum(t - half, 0)
                inv = 1.0 / cnt.astype(F32)
            else:
                inv = 1.0 / win
            d = []
            for q in range(nq):
                span_rows = (win - 1) * SUBLANES
                rows_in = ext[q, pl.ds(base - half * SUBLANES, chunk + span_rows), :]
                tot = rows_in
                step = SUBLANES
                while step <= half * SUBLANES:
                    keep = tot.shape[0] - step
                    tot = tot[:keep] + tot[step:step + keep]
                    step *= 2
                d.append(tot * inv - rows_in[half * SUBLANES:half * SUBLANES + chunk])
            d = jnp.concatenate(d, axis=1)
            y = jnp.dot(d.astype(BF16), w_ref[0], preferred_element_type=F32) * s_ref[0]
            o_ref[0, pl.ds(r0, chunk), :] = y.astype(o_ref.dtype)
            return carry

        assert chunk // SUBLANES >= half and n_chunks >= 2
        do_chunk(0)
        lax.fori_loop(1, n_chunks - 1, do_chunk, 0)
        do_chunk(n_chunks - 1)

    for gi, win in enumerate(POOL_WINDOWS):
        pl.when(grp == gi)(functools.partial(body, win))


def _pool_branch(z_seg, pool_w, pool_scale, batch, seq, chunk=512):
    n_groups, gw = pool_w.shape[0], pool_w.shape[1]
    nq = gw // LANES
    return pl.pallas_call(
        functools.partial(_pool_kernel, seq=seq, chunk=chunk),
        grid=(batch, n_groups),
        in_specs=[pl.BlockSpec((nq, seq, LANES), lambda b, g: (g, b, 0)),
                  pl.BlockSpec((1, gw, gw), lambda b, g: (g, 0, 0)),
                  pl.BlockSpec((1, 1, gw), lambda b, g: (g, 0, 0))],
        out_specs=pl.BlockSpec((1, seq, gw), lambda b, g: (b, 0, g)),
        out_shape=jax.ShapeDtypeStruct((batch, seq, n_groups * gw), F32),
        scratch_shapes=[pltpu.VMEM((nq, seq + 2 * HALO * SUBLANES, LANES), F32)],
        compiler_params=pltpu.CompilerParams(
            dimension_semantics=("arbitrary", "arbitrary"),
            vmem_limit_bytes=VMEM_LIMIT),
        name="pool_branch",
    )(z_seg, pool_w, pool_scale.reshape(n_groups, 1, gw))


_P_CONV_W = 0
_P_CONV_B = CONV_WIDTH
_P_DIR = CONV_WIDTH + 1
_P_ROWS = 16


def _lru_kernel(u_ref, w_ref, p_ref, o_ref, ext, a_f, x_f, a_b, x_b, summ, ent, *, seq, chunk):
    h_rows = HALO * SUBLANES
    c = LRU_BLOCK
    n_chunks = seq // chunk
    groups = chunk // SUBLANES
    _fill_ext(ext, u_ref, seq)

    conv_hw = [0.5 * p_ref[pl.ds(_P_CONV_W + k, 1), :] for k in range(CONV_WIDTH)]
    conv_hb = 0.5 * p_ref[pl.ds(_P_CONV_B, 1), :]
    hb_a, hb_x, c2 = [], [], []
    for d in range(2):
        hb_a.append(0.5 * p_ref[pl.ds(_P_DIR + 3 * d, 1), :])
        hb_x.append(0.5 * p_ref[pl.ds(_P_DIR + 3 * d + 1, 1), :])
        lam = p_ref[pl.ds(_P_DIR + 3 * d + 2, 1), :]
        c2.append((-0.5 * LRU_C * LOG2_E) * jax.nn.softplus(-lam))
    a_out = (a_f, a_b)
    x_out = (x_f, x_b)

    def rows(v, k):
        return v[k * SUBLANES:(k + 1) * SUBLANES]

    def summary_rows(ci):
        return pl.ds(pl.multiple_of(ci * SUBLANES, SUBLANES), SUBLANES)

    zeros = jnp.zeros((SUBLANES, c), F32)
    ones = jnp.ones((SUBLANES, c), F32)

    def gates(ci, carry):
        r0 = pl.multiple_of(ci * chunk, chunk)
        base = r0 + h_rows
        xh = conv_hb + _load_slabs(ext, base - CONV_LEFT * SUBLANES, chunk) * conv_hw[0]
        for k in range(1, CONV_WIDTH):
            xh = xh + _load_slabs(ext, base + (k - CONV_LEFT) * SUBLANES, chunk) * conv_hw[k]
        pre = jnp.dot(xh.astype(BF16), w_ref[0], preferred_element_type=F32)
        for d in range(2):
            t_r = jnp.tanh(pre[:, (2 * d) * c:(2 * d + 1) * c] + hb_a[d])
            t_i = jnp.tanh(pre[:, (2 * d + 1) * c:(2 * d + 2) * c] + hb_x[d])
            log2_a = c2[d] * t_r + c2[d]
            a = jnp.exp2(log2_a)
            one_m_a2 = (-1.0 - a * a) * jnp.tanh(LN_2 * log2_a)
            root = one_m_a2 * lax.rsqrt(jnp.maximum(one_m_a2, TINY))
            inp = root * (t_i * xh + xh)
            _store_slabs(a_out[d], r0, chunk, a)
            _store_slabs(x_out[d], r0, chunk, inp)
            h, p = zeros, ones
            for k in (range(groups) if d == 0 else range(groups - 1, -1, -1)):
                h = rows(a, k) * h + rows(inp, k)
                p = rows(a, k) * p
            summ[2 * d, summary_rows(ci), :] = h
            summ[2 * d + 1, summary_rows(ci), :] = p
        return carry

    lax.fori_loop(0, n_chunks, gates, 0)

    def summary(idx, ci):
        return summ[idx, ci * SUBLANES:(ci + 1) * SUBLANES, :]

    h_f, p_f, h_b, p_b = zeros, ones, zeros, ones
    for ci in range(n_chunks):
        cb = n_chunks - 1 - ci
        h_f = summary(0, ci) + summary(1, ci) * h_f
        p_f = summary(1, ci) * p_f
        h_b = summary(2, cb) + summary(3, cb) * h_b
        p_b = summary(3, cb) * p_b
    sub = lax.broadcasted_iota(jnp.int32, (SUBLANES, c), 0)
    e_f = zeros
    e_b = zeros
    for _ in range(NSEG - 1):
        e_f = jnp.where(sub == 0, 0.0, pltpu.roll(h_f + p_f * e_f, 1, 0))
        e_b = jnp.where(sub == SUBLANES - 1, 0.0, pltpu.roll(h_b + p_b * e_b, SUBLANES - 1, 0))
    for ci in range(n_chunks):
        cb = n_chunks - 1 - ci
        ent[0, ci * SUBLANES:(ci + 1) * SUBLANES, :] = e_f
        ent[1, cb * SUBLANES:(cb + 1) * SUBLANES, :] = e_b
        e_f = summary(0, ci) + summary(1, ci) * e_f
        e_b = summary(2, cb) + summary(3, cb) * e_b

    def replay(it, carry):
        for sub_i in range(REPLAY_CHUNKS):
            ci = it * REPLAY_CHUNKS + sub_i
            r0 = pl.multiple_of(ci * chunk, chunk)
            af, xf = _load_slabs(a_f, r0, chunk), _load_slabs(x_f, r0, chunk)
            ab, xb = _load_slabs(a_b, r0, chunk), _load_slabs(x_b, r0, chunk)
            h = ent[0, summary_rows(ci), :]
            fwd = []
            for k in range(groups):
                h = rows(af, k) * h + rows(xf, k)
                fwd.append(h)
            h = ent[1, summary_rows(ci), :]
            out = [None] * groups
            for k in range(groups - 1, -1, -1):
                h = rows(ab, k) * h + rows(xb, k)
                out[k] = h + fwd[k]
            o_ref[0, pl.ds(r0, chunk), :] = jnp.concatenate(out, axis=0).astype(o_ref.dtype)
        return carry

    lax.fori_loop(0, n_chunks // REPLAY_CHUNKS, replay, 0)


def _lru_branch(z_seg, w_gates, params, batch, seq, lru_col0, chunk=512):
    n_heads = w_gates.shape[0]
    c = LRU_BLOCK
    nq = c // LANES
    z3 = z_seg
    col0 = lru_col0 // c
    n_chunks = seq // chunk
    seg_buf = pltpu.VMEM((nq, seq, LANES), F32)
    return pl.pallas_call(
        functools.partial(_lru_kernel, seq=seq, chunk=chunk),
        grid=(batch, n_heads),
        in_specs=[pl.BlockSpec((nq, seq, LANES), lambda b, h: (col0 + h, b, 0)),
                  pl.BlockSpec((1, c, 4 * c), lambda b, h: (h, 0, 0)),
                  pl.BlockSpec((_P_ROWS, c), lambda b, h: (0, h))],
        out_specs=pl.BlockSpec((1, seq, c), lambda b, h: (b, 0, h)),
        out_shape=jax.ShapeDtypeStruct((batch, seq, n_heads * c), F32),
        scratch_shapes=[pltpu.VMEM((nq, seq + 2 * HALO * SUBLANES, LANES), F32)]
        + [seg_buf] * 4
        + [pltpu.VMEM((4, n_chunks * SUBLANES, c), F32), pltpu.VMEM((2, n_chunks * SUBLANES, c), F32)],
        compiler_params=pltpu.CompilerParams(
            dimension_semantics=("arbitrary", "arbitrary"),
            vmem_limit_bytes=VMEM_LIMIT),
        name="lru_branch",
    )(z3, w_gates, params)


def _merge_kernel(yp_ref, yl_ref, gate_ref, g0_ref, g1_ref, x_ref, wp_ref, wl_ref, wo_ref, bo_ref, g_ref, b_ref,
                  of_ref, ob_ref, *, alpha):
    _, n_seg, tj, d = x_ref.shape
    tm = n_seg * tj

    def to_time(v):
        return jnp.swapaxes(v.reshape(tj, n_seg, d), 0, 1).reshape(tm, d)

    y_lru = to_time(yl_ref[...]) * gate_ref[0].reshape(tm, d)
    up_p = jnp.dot(to_time(yp_ref[...]).astype(BF16), wp_ref[...], preferred_element_type=F32)
    up_l = jnp.dot(y_lru.astype(BF16), wl_ref[...], preferred_element_type=F32)
    m = g0_ref[0].reshape(tm, d) * up_p + g1_ref[0].reshape(tm, d) * up_l
    mix = jnp.dot(m.astype(BF16), wo_ref[...], preferred_element_type=F32) + bo_ref[...]
    y = _layer_norm(alpha * x_ref[0].reshape(tm, d) + mix, g_ref[...], b_ref[...])
    of_ref[0] = y.reshape(n_seg, tj, d)
    ob_ref[0] = y.astype(BF16).reshape(n_seg, tj, d)


def _merge(y_pool, y_lru, gate_act, g_act, x_rows, w_pool_up, w_lru_up, w_out, b_out, ln_g, ln_b, alpha, batch, seq,
           tm=256):
    t, d = x_rows.shape
    seg_len = seq // NSEG
    tj = tm // NSEG
    n_jb = seg_len // tj
    row = lambda i: (i, 0)
    const = lambda i: (0, 0)
    wspec = pl.BlockSpec((d, d), const, pipeline_mode=pl.Buffered(1))
    vspec = pl.BlockSpec((1, d), const)

    def tspec(col):
        return pl.BlockSpec((1, NSEG, tj, d), lambda i: (i // n_jb, 0, i % n_jb, col))

    g4 = g_act.reshape(batch, NSEG, seg_len, g_act.shape[1])
    x1_f, x1_b = pl.pallas_call(
        functools.partial(_merge_kernel, alpha=alpha),
        grid=(t // tm,),
        in_specs=[pl.BlockSpec((tm, d), row), pl.BlockSpec((tm, d), row),
                  tspec(0), tspec(0), tspec(1), tspec(0), wspec, wspec, wspec, vspec, vspec, vspec],
        out_specs=[tspec(0), tspec(0)],
        out_shape=[jax.ShapeDtypeStruct((batch, NSEG, seg_len, d), F32),
                   jax.ShapeDtypeStruct((batch, NSEG, seg_len, d), BF16)],
        compiler_params=pltpu.CompilerParams(
            dimension_semantics=("arbitrary",), vmem_limit_bytes=WIDE_VMEM_LIMIT),
        name="merge_out_ln",
    )(y_pool, y_lru, gate_act.reshape(batch, NSEG, seg_len, d), g4, g4, x_rows.reshape(batch, NSEG, seg_len, d),
      w_pool_up, w_lru_up, w_out,
      b_out.reshape(1, d), ln_g.reshape(1, d), ln_b.reshape(1, d))
    return x1_f.reshape(t, d), x1_b.reshape(t, d)


def _mlp_kernel(xb_ref, xf_ref, w1_ref, b1_ref, w2_ref, b2_ref, g_ref, b_ref, o_ref, *, alpha):
    f = pl.program_id(1)

    def contribution():
        h = jnp.dot(xb_ref[...], w1_ref[0].astype(BF16), preferred_element_type=F32) + b1_ref[...]
        h = jnp.square(jnp.maximum(h, 0.0))
        return jnp.dot(h.astype(BF16), w2_ref[0].astype(BF16), preferred_element_type=F32)

    @pl.when(f == 0)
    def _():
        o_ref[...] = (alpha * xf_ref[...] + b2_ref[...]) + contribution()

    @pl.when(f > 0)
    def _():
        o_ref[...] += contribution()

    @pl.when(f == pl.num_programs(1) - 1)
    def _():
        o_ref[...] = _layer_norm(o_ref[...], g_ref[...], b_ref[...])


def _mlp(x_b, x_f, w1, layer, b1, w2, b2, ln_g, ln_b, alpha, tm=1024, tf=512):
    t, d = x_f.shape
    dff = w1.shape[2]
    return pl.pallas_call(
        functools.partial(_mlp_kernel, alpha=alpha),
        grid=(t // tm, dff // tf),
        in_specs=[pl.BlockSpec((tm, d), lambda i, f: (i, 0)),
                  pl.BlockSpec((tm, d), lambda i, f: (i, 0)),
                  pl.BlockSpec((1, d, tf), lambda i, f: (layer, 0, f)),
                  pl.BlockSpec((1, tf), lambda i, f: (0, f)),
                  pl.BlockSpec((1, tf, d), lambda i, f: (layer, f, 0)),
                  pl.BlockSpec((1, d), lambda i, f: (0, 0)),
                  pl.BlockSpec((1, d), lambda i, f: (0, 0)),
                  pl.BlockSpec((1, d), lambda i, f: (0, 0))],
        out_specs=pl.BlockSpec((tm, d), lambda i, f: (i, 0)),
        out_shape=jax.ShapeDtypeStruct((t, d), F32),
        compiler_params=pltpu.CompilerParams(
            dimension_semantics=("arbitrary", "arbitrary"), vmem_limit_bytes=WIDE_VMEM_LIMIT),
        name="mlp_ln",
    )(x_b, x_f, w1, b1.reshape(1, dff), w2, b2.reshape(1, d), ln_g.reshape(1, d), ln_b.reshape(1, d))


def _layer(x_rows, batch, seq, alpha, layer, w_in, pool_w, pool_scale, conv_w, conv_b, lru_wa, lru_ba, lru_wx,
           lru_bx, lru_lambda, w_pool_up, w_lru_up, w_out, b_out, ln1_g, ln1_b, w_ff1, b_ff1, w_ff2, b_ff2,
           ln2_g, ln2_b):
    pool_width = pool_w.shape[0] * pool_w.shape[1]
    lru_width = conv_w.shape[1]
    o1, o2, o3 = pool_width, pool_width + lru_width, pool_width + 2 * lru_width

    z_a, = _mm_act(x_rows, w_in, layer, 0, o2, "none", seg=(batch, seq))
    g_act, = _mm_act(x_rows, w_in, layer, o3, w_in.shape[2] - o3, "sigmoid")
    gate_act, w_pool_up_b, w_lru_up_b, w_out_b = _mm_act(
        x_rows, w_in, layer, o2, o3 - o2, "gelu", casts=(w_pool_up, w_lru_up, w_out))

    y_pool = _pool_branch(z_a, pool_w.astype(BF16), pool_scale, batch, seq)

    w_gates = jnp.concatenate([lru_wa[0], lru_wx[0], lru_wa[1], lru_wx[1]], axis=-1).astype(BF16)
    params = jnp.concatenate(
        [conv_w, conv_b[None], lru_ba[0][None], lru_bx[0][None], lru_lambda[0][None],
         lru_ba[1][None], lru_bx[1][None], lru_lambda[1][None]], axis=0).astype(F32)
    params = jnp.pad(params, ((0, _P_ROWS - params.shape[0]), (0, 0)))
    y_lru = _lru_branch(z_a, w_gates, params, batch, seq, o1)

    t = batch * seq
    x1_f, x1_b = _merge(y_pool.reshape(t, -1), y_lru.reshape(t, -1), gate_act, g_act, x_rows,
                        w_pool_up_b, w_lru_up_b, w_out_b,
                        b_out, ln1_g, ln1_b, alpha, batch, seq)
    return _mlp(x1_b, x1_f, w_ff1, layer, b_ff1, w_ff2, b_ff2, ln2_g, ln2_b, alpha)


def kernel(x, w_in, pool_w, pool_scale, conv_w, conv_b, lru_wa, lru_ba, lru_wx, lru_bx, lru_lambda, w_pool_up, w_lru_up, w_out, b_out, ln1_g, ln1_b, w_ff1, b_ff1, w_ff2, b_ff2, ln2_g, ln2_b):
    batch, seq, d = x.shape
    depth = w_in.shape[0]
    alpha = (2.0 * depth) ** 0.25
    rows = x.reshape(batch * seq, d)
    for l in range(depth):
        rows = _layer(rows, batch, seq, alpha, l, w_in, pool_w[l], pool_scale[l], conv_w[l], conv_b[l],
                      lru_wa[l], lru_ba[l], lru_wx[l], lru_bx[l], lru_lambda[l], w_pool_up[l], w_lru_up[l],
                      w_out[l], b_out[l], ln1_g[l], ln1_b[l], w_ff1, b_ff1[l], w_ff2, b_ff2[l],
                      ln2_g[l], ln2_b[l])
    return rows.reshape(batch, seq, d)
```

```python
import functools

import jax
import jax.numpy as jnp
from jax import lax
from jax.experimental import pallas as pl
from jax.experimental.pallas import tpu as pltpu

SUBLANES = 8
LANES = 128
NSEG = SUBLANES

POOL_WINDOWS = (2, 4, 8, 16)
LRU_BLOCK = 256
CONV_WIDTH = 4
CONV_LEFT = CONV_WIDTH // 2
LRU_C = 8.0
LN_EPS = 1e-5
HALO = 8
REPLAY_CHUNKS = 2
COPY_ROWS = 512
LOG2_E = 1.4426950408889634
LN_2 = 0.6931471805599453
TINY = 1.1754944e-38

VMEM_LIMIT = 60 * 1024 * 1024
WIDE_VMEM_LIMIT = 62 * 1024 * 1024

F32 = jnp.float32
BF16 = jnp.bfloat16


def _sigmoid(v):
    return 0.5 * jnp.tanh(0.5 * v) + 0.5


def _gelu_tanh(v):
    c = 0.7978845608028654
    return 0.5 * v * (1.0 + jnp.tanh(c * (v + 0.044715 * (v * v * v))))


def _layer_norm(y, g, b):
    mu = jnp.mean(y, axis=-1, keepdims=True)
    yc = y - mu
    var = jnp.mean(yc * yc, axis=-1, keepdims=True)
    return yc * lax.rsqrt(var + LN_EPS) * g + b


def _mm_act_kernel(*refs, act, n_casts, seg):
    x_ref, w_ref = refs[:2]
    cast_in = refs[2:2 + n_casts]
    o_ref = refs[2 + n_casts]
    cast_out = refs[3 + n_casts:3 + 2 * n_casts]
    w_bf = refs[-1]

    @pl.when(pl.program_id(1) == 0)
    def _():
        w_bf[...] = w_ref[0].astype(BF16)

    if seg:
        tj = x_ref.shape[2]
        x_tile = jnp.swapaxes(x_ref[0], 0, 1).reshape(NSEG * tj, x_ref.shape[3])
    else:
        x_tile = x_ref[...]
    acc = jnp.dot(x_tile.astype(BF16), w_bf[...], preferred_element_type=F32)
    if act == "gelu":
        acc = _gelu_tanh(acc)
    elif act == "sigmoid":
        acc = _sigmoid(acc)
    if seg:
        for q in range(o_ref.shape[0]):
            o_ref[q] = acc[:, q * LANES:(q + 1) * LANES]
    else:
        o_ref[...] = acc.astype(o_ref.dtype)
    for src, dst in zip(cast_in, cast_out):
        dst[...] = src[...].astype(BF16)


def _mm_act(x, w, layer, col0, n, act, casts=(), seg=None, tm=512, tn=2048):
    m, k = x.shape
    j0 = col0 // tn
    n_i = m // tm
    steps = (n // tn) * n_i
    cast_specs = [pl.BlockSpec((a.shape[0] // steps, a.shape[1]), lambda j, i: (j * n_i + i, 0)) for a in casts]
    if seg:
        batch, seq = seg
        tj = tm // NSEG
        n_jb = seq // NSEG // tj
        x = x.reshape(batch, NSEG, seq // NSEG, k)
        x_spec = pl.BlockSpec((1, NSEG, tj, k), lambda j, i: (i // n_jb, 0, i % n_jb, 0))
        o_spec = pl.BlockSpec((tn // LANES, tm, LANES), lambda j, i: (j, i, 0))
        o_shape = jax.ShapeDtypeStruct((n // LANES, m, LANES), F32)
    else:
        x_spec = pl.BlockSpec((tm, k), lambda j, i: (i, 0))
        o_spec = pl.BlockSpec((tm, tn), lambda j, i: (i, j))
        o_shape = jax.ShapeDtypeStruct((m, n), F32)
    return pl.pallas_call(
        functools.partial(_mm_act_kernel, act=act, n_casts=len(casts), seg=bool(seg)),
        grid=(n // tn, n_i),
        in_specs=[x_spec, pl.BlockSpec((1, k, tn), lambda j, i: (layer, 0, j0 + j))] + cast_specs,
        out_specs=[o_spec] + cast_specs,
        out_shape=[o_shape] + [jax.ShapeDtypeStruct(a.shape, BF16) for a in casts],
        scratch_shapes=[pltpu.VMEM((k, tn), BF16)],
        compiler_params=pltpu.CompilerParams(
            dimension_semantics=("arbitrary", "arbitrary"),
            vmem_limit_bytes=VMEM_LIMIT),
        name=f"mm_in_{act}",
    )(x, w, *casts)


def _load_slabs(buf, r, n):
    return jnp.concatenate([buf[q, pl.ds(r, n), :] for q in range(buf.shape[0])], axis=1)


def _store_slabs(buf, r, n, val):
    for q in range(buf.shape[0]):
        buf[q, pl.ds(r, n), :] = val[:, q * LANES:(q + 1) * LANES]


def _fill_ext(ext, u_ref, seq):
    h_rows = HALO * SUBLANES
    nq = ext.shape[0]

    def copy(bi, carry):
        r0 = pl.multiple_of(bi * COPY_ROWS, COPY_ROWS)
        for q in range(nq):
            ext[q, pl.ds(h_rows + r0, COPY_ROWS), :] = u_ref[q, pl.ds(r0, COPY_ROWS), :]
        return carry

    lax.fori_loop(0, seq // COPY_ROWS, copy, 0)

    sub = lax.broadcasted_iota(jnp.int32, (SUBLANES, LANES), 0)
    for q in range(nq):
        for m in range(HALO):
            nxt = pltpu.roll(u_ref[q, pl.ds(m * SUBLANES, SUBLANES), :], SUBLANES - 1, 0)
            ext[q, pl.ds(h_rows + seq + m * SUBLANES, SUBLANES), :] = jnp.where(sub == SUBLANES - 1, 0.0, nxt)
            prv = pltpu.roll(u_ref[q, pl.ds(seq - (m + 1) * SUBLANES, SUBLANES), :], 1, 0)
            ext[q, pl.ds(h_rows - (m + 1) * SUBLANES, SUBLANES), :] = jnp.where(sub == 0, 0.0, prv)


def _pool_kernel(u_ref, w_ref, s_ref, o_ref, ext, *, seq, chunk):
    seg_len = seq // NSEG
    h_rows = HALO * SUBLANES
    nq = ext.shape[0]
    _fill_ext(ext, u_ref, seq)
    grp = pl.program_id(1)
    n_chunks = seq // chunk

    def body(win):
        half = win // 2

        def do_chunk(ci, carry=0):
            clipped = isinstance(ci, int)
            r0 = ci * chunk if clipped else pl.multiple_of(ci * chunk, chunk)
            base = r0 + h_rows
            if clipped:
                row = r0 + lax.broadcasted_iota(jnp.int32, (chunk, LANES), 0)
                t = (row & (SUBLANES - 1)) * seg_len + (row >> 3)
                cnt = jnp.minimum(t + half, seq) - jnp.maximum(t - half, 0)
                inv = 1.0 / cnt.astype(F32)
            else:
                inv = 1.0 / win
            d = []
            for q in range(nq):
                span_rows = (win - 1) * SUBLANES
                rows_in = ext[q, pl.ds(base - half * SUBLANES, chunk + span_rows), :]
                tot = rows_in
                step = SUBLANES
                while step <= half * SUBLANES:
                    keep = tot.shape[0] - step
                    tot = tot[:keep] + tot[step:step + keep]
                    step *= 2
                d.append(tot * inv - rows_in[half * SUBLANES:half * SUBLANES + chunk])
            d = jnp.concatenate(d, axis=1)
            y = jnp.dot(d.astype(BF16), w_ref[0], preferred_element_type=F32) * s_ref[0]
            o_ref[0, pl.ds(r0, chunk), :] = y.astype(o_ref.dtype)
            return carry

        assert chunk // SUBLANES >= half and n_chunks >= 2
        do_chunk(0)
        lax.fori_loop(1, n_chunks - 1, do_chunk, 0)
        do_chunk(n_chunks - 1)

    for gi, win in enumerate(POOL_WINDOWS):
        pl.when(grp == gi)(functools.partial(body, win))


def _pool_branch(z_seg, pool_w, pool_scale, batch, seq, chunk=512):
    n_groups, gw = pool_w.shape[0], pool_w.shape[1]
    nq = gw // LANES
    return pl.pallas_call(
        functools.partial(_pool_kernel, seq=seq, chunk=chunk),
        grid=(batch, n_groups),
        in_specs=[pl.BlockSpec((nq, seq, LANES), lambda b, g: (g, b, 0)),
                  pl.BlockSpec((1, gw, gw), lambda b, g: (g, 0, 0)),
                  pl.BlockSpec((1, 1, gw), lambda b, g: (g, 0, 0))],
        out_specs=pl.BlockSpec((1, seq, gw), lambda b, g: (b, 0, g)),
        out_shape=jax.ShapeDtypeStruct((batch, seq, n_groups * gw), BF16),
        scratch_shapes=[pltpu.VMEM((nq, seq + 2 * HALO * SUBLANES, LANES), F32)],
        compiler_params=pltpu.CompilerParams(
            dimension_semantics=("arbitrary", "arbitrary"),
            vmem_limit_bytes=VMEM_LIMIT),
        name="pool_branch",
    )(z_seg, pool_w, pool_scale.reshape(n_groups, 1, gw))


_P_CONV_W = 0
_P_CONV_B = CONV_WIDTH
_P_DIR = CONV_WIDTH + 1
_P_ROWS = 16


def _lru_kernel(u_ref, w_ref, p_ref, o_ref, ext, a_f, x_f, a_b, x_b, summ, ent, *, seq, chunk):
    h_rows = HALO * SUBLANES
    c = LRU_BLOCK
    n_chunks = seq // chunk
    groups = chunk // SUBLANES
    _fill_ext(ext, u_ref, seq)

    conv_hw = [0.5 * p_ref[pl.ds(_P_CONV_W + k, 1), :] for k in range(CONV_WIDTH)]
    conv_hb = 0.5 * p_ref[pl.ds(_P_CONV_B, 1), :]
    hb_a, hb_x, c2 = [], [], []
    for d in range(2):
        hb_a.append(0.5 * p_ref[pl.ds(_P_DIR + 3 * d, 1), :])
        hb_x.append(0.5 * p_ref[pl.ds(_P_DIR + 3 * d + 1, 1), :])
        lam = p_ref[pl.ds(_P_DIR + 3 * d + 2, 1), :]
        c2.append((-0.5 * LRU_C * LOG2_E) * jax.nn.softplus(-lam))
    a_out = (a_f, a_b)
    x_out = (x_f, x_b)

    def rows(v, k):
        return v[k * SUBLANES:(k + 1) * SUBLANES]

    def summary_rows(ci):
        return pl.ds(pl.multiple_of(ci * SUBLANES, SUBLANES), SUBLANES)

    zeros = jnp.zeros((SUBLANES, c), F32)
    ones = jnp.ones((SUBLANES, c), F32)

    def gates(ci, carry):
        r0 = pl.multiple_of(ci * chunk, chunk)
        base = r0 + h_rows
        xh = conv_hb + _load_slabs(ext, base - CONV_LEFT * SUBLANES, chunk) * conv_hw[0]
        for k in range(1, CONV_WIDTH):
            xh = xh + _load_slabs(ext, base + (k - CONV_LEFT) * SUBLANES, chunk) * conv_hw[k]
        pre = jnp.dot(xh.astype(BF16), w_ref[0], preferred_element_type=F32)
        for d in range(2):
            t_r = jnp.tanh(pre[:, (2 * d) * c:(2 * d + 1) * c] + hb_a[d])
            t_i = jnp.tanh(pre[:, (2 * d + 1) * c:(2 * d + 2) * c] + hb_x[d])
            log2_a = c2[d] * t_r + c2[d]
            a = jnp.exp2(log2_a)
            one_m_a2 = (-1.0 - a * a) * jnp.tanh(LN_2 * log2_a)
            root = one_m_a2 * lax.rsqrt(jnp.maximum(one_m_a2, TINY))
            inp = root * (t_i * xh + xh)
            _store_slabs(a_out[d], r0, chunk, a)
            _store_slabs(x_out[d], r0, chunk, inp)
            h, p = zeros, ones
            for k in (range(groups) if d == 0 else range(groups - 1, -1, -1)):
                h = rows(a, k) * h + rows(inp, k)
                p = rows(a, k) * p
            summ[2 * d, summary_rows(ci), :] = h
            summ[2 * d + 1, summary_rows(ci), :] = p
        return carry

    lax.fori_loop(0, n_chunks, gates, 0)

    def summary(idx, ci):
        return summ[idx, ci * SUBLANES:(ci + 1) * SUBLANES, :]

    h_f, p_f, h_b, p_b = zeros, ones, zeros, ones
    for ci in range(n_chunks):
        cb = n_chunks - 1 - ci
        h_f = summary(0, ci) + summary(1, ci) * h_f
        p_f = summary(1, ci) * p_f
        h_b = summary(2, cb) + summary(3, cb) * h_b
        p_b = summary(3, cb) * p_b
    sub = lax.broadcasted_iota(jnp.int32, (SUBLANES, c), 0)
    e_f = zeros
    e_b = zeros
    for _ in range(NSEG - 1):
        e_f = jnp.where(sub == 0, 0.0, pltpu.roll(h_f + p_f * e_f, 1, 0))
        e_b = jnp.where(sub == SUBLANES - 1, 0.0, pltpu.roll(h_b + p_b * e_b, SUBLANES - 1, 0))
    for ci in range(n_chunks):
        cb = n_chunks - 1 - ci
        ent[0, ci * SUBLANES:(ci + 1) * SUBLANES, :] = e_f
        ent[1, cb * SUBLANES:(cb + 1) * SUBLANES, :] = e_b
        e_f = summary(0, ci) + summary(1, ci) * e_f
        e_b = summary(2, cb) + summary(3, cb) * e_b

    def replay(it, carry):
        for sub_i in range(REPLAY_CHUNKS):
            ci = it * REPLAY_CHUNKS + sub_i
            r0 = pl.multiple_of(ci * chunk, chunk)
            af, xf = _load_slabs(a_f, r0, chunk), _load_slabs(x_f, r0, chunk)
            ab, xb = _load_slabs(a_b, r0, chunk), _load_slabs(x_b, r0, chunk)
            h = ent[0, summary_rows(ci), :]
            fwd = []
            for k in range(groups):
                h = rows(af, k) * h + rows(xf, k)
                fwd.append(h)
            h = ent[1, summary_rows(ci), :]
            out = [None] * groups
            for k in range(groups - 1, -1, -1):
                h = rows(ab, k) * h + rows(xb, k)
                out[k] = h + fwd[k]
            o_ref[0, pl.ds(r0, chunk), :] = jnp.concatenate(out, axis=0).astype(o_ref.dtype)
        return carry

    lax.fori_loop(0, n_chunks // REPLAY_CHUNKS, replay, 0)


def _lru_branch(z_seg, w_gates, params, batch, seq, lru_col0, chunk=512):
    n_heads = w_gates.shape[0]
    c = LRU_BLOCK
    nq = c // LANES
    z3 = z_seg
    col0 = lru_col0 // c
    n_chunks = seq // chunk
    seg_buf = pltpu.VMEM((nq, seq, LANES), F32)
    return pl.pallas_call(
        functools.partial(_lru_kernel, seq=seq, chunk=chunk),
        grid=(batch, n_heads),
        in_specs=[pl.BlockSpec((nq, seq, LANES), lambda b, h: (col0 + h, b, 0)),
                  pl.BlockSpec((1, c, 4 * c), lambda b, h: (h, 0, 0)),
                  pl.BlockSpec((_P_ROWS, c), lambda b, h: (0, h))],
        out_specs=pl.BlockSpec((1, seq, c), lambda b, h: (b, 0, h)),
        out_shape=jax.ShapeDtypeStruct((batch, seq, n_heads * c), F32),
        scratch_shapes=[pltpu.VMEM((nq, seq + 2 * HALO * SUBLANES, LANES), F32)]
        + [seg_buf] * 4
        + [pltpu.VMEM((4, n_chunks * SUBLANES, c), F32), pltpu.VMEM((2, n_chunks * SUBLANES, c), F32)],
        compiler_params=pltpu.CompilerParams(
            dimension_semantics=("arbitrary", "arbitrary"),
            vmem_limit_bytes=VMEM_LIMIT),
        name="lru_branch",
    )(z3, w_gates, params)


def _merge_kernel(yp_ref, yl_ref, gate_ref, g0_ref, g1_ref, x_ref, wp_ref, wl_ref, wo_ref, bo_ref, g_ref, b_ref,
                  of_ref, ob_ref, *, alpha):
    _, n_seg, tj, d = x_ref.shape
    tm = n_seg * tj

    def to_time(v):
        return jnp.swapaxes(v.reshape(tj, n_seg, d), 0, 1).reshape(tm, d)

    y_lru = to_time(yl_ref[...]) * gate_ref[0].reshape(tm, d)
    up_p = jnp.dot(to_time(yp_ref[...].astype(F32)).astype(BF16), wp_ref[...], preferred_element_type=F32)
    up_l = jnp.dot(y_lru.astype(BF16), wl_ref[...], preferred_element_type=F32)
    m = g0_ref[0].reshape(tm, d) * up_p + g1_ref[0].reshape(tm, d) * up_l
    mix = jnp.dot(m.astype(BF16), wo_ref[...], preferred_element_type=F32) + bo_ref[...]
    y = _layer_norm(alpha * x_ref[0].reshape(tm, d) + mix, g_ref[...], b_ref[...])
    of_ref[0] = y.reshape(n_seg, tj, d)
    ob_ref[0] = y.astype(BF16).reshape(n_seg, tj, d)


def _merge(y_pool, y_lru, gate_act, g_act, x_rows, w_pool_up, w_lru_up, w_out, b_out, ln_g, ln_b, alpha, batch, seq,
           tm=256):
    t, d = x_rows.shape
    seg_len = seq // NSEG
    tj = tm // NSEG
    n_jb = seg_len // tj
    row = lambda i: (i, 0)
    const = lambda i: (0, 0)
    wspec = pl.BlockSpec((d, d), const, pipeline_mode=pl.Buffered(1))
    vspec = pl.BlockSpec((1, d), const)

    def tspec(col):
        return pl.BlockSpec((1, NSEG, tj, d), lambda i: (i // n_jb, 0, i % n_jb, col))

    g4 = g_act.reshape(batch, NSEG, seg_len, g_act.shape[1])
    x1_f, x1_b = pl.pallas_call(
        functools.partial(_merge_kernel, alpha=alpha),
        grid=(t // tm,),
        in_specs=[pl.BlockSpec((tm, d), row), pl.BlockSpec((tm, d), row),
                  tspec(0), tspec(0), tspec(1), tspec(0), wspec, wspec, wspec, vspec, vspec, vspec],
        out_specs=[tspec(0), tspec(0)],
        out_shape=[jax.ShapeDtypeStruct((batch, NSEG, seg_len, d), F32),
                   jax.ShapeDtypeStruct((batch, NSEG, seg_len, d), BF16)],
        compiler_params=pltpu.CompilerParams(
            dimension_semantics=("arbitrary",), vmem_limit_bytes=WIDE_VMEM_LIMIT),
        name="merge_out_ln",
    )(y_pool, y_lru, gate_act.reshape(batch, NSEG, seg_len, d), g4, g4, x_rows.reshape(batch, NSEG, seg_len, d),
      w_pool_up, w_lru_up, w_out,
      b_out.reshape(1, d), ln_g.reshape(1, d), ln_b.reshape(1, d))
    return x1_f.reshape(t, d), x1_b.reshape(t, d)


def _mlp_kernel(xb_ref, xf_ref, w1_ref, b1_ref, w2_ref, b2_ref, g_ref, b_ref, o_ref, *, alpha):
    f = pl.program_id(1)

    def contribution():
        h = jnp.dot(xb_ref[...], w1_ref[0].astype(BF16), preferred_element_type=F32) + b1_ref[...]
        h = jnp.square(jnp.maximum(h, 0.0))
        return jnp.dot(h.astype(BF16), w2_ref[0].astype(BF16), preferred_element_type=F32)

    @pl.when(f == 0)
    def _():
        o_ref[...] = (alpha * xf_ref[...] + b2_ref[...]) + contribution()

    @pl.when(f > 0)
    def _():
        o_ref[...] += contribution()

    @pl.when(f == pl.num_programs(1) - 1)
    def _():
        o_ref[...] = _layer_norm(o_ref[...], g_ref[...], b_ref[...])


def _mlp(x_b, x_f, w1, layer, b1, w2, b2, ln_g, ln_b, alpha, tm=1024, tf=512):
    t, d = x_f.shape
    dff = w1.shape[2]
    return pl.pallas_call(
        functools.partial(_mlp_kernel, alpha=alpha),
        grid=(t // tm, dff // tf),
        in_specs=[pl.BlockSpec((tm, d), lambda i, f: (i, 0)),
                  pl.BlockSpec((tm, d), lambda i, f: (i, 0)),
                  pl.BlockSpec((1, d, tf), lambda i, f: (layer, 0, f)),
                  pl.BlockSpec((1, tf), lambda i, f: (0, f)),
                  pl.BlockSpec((1, tf, d), lambda i, f: (layer, f, 0)),
                  pl.BlockSpec((1, d), lambda i, f: (0, 0)),
                  pl.BlockSpec((1, d), lambda i, f: (0, 0)),
                  pl.BlockSpec((1, d), lambda i, f: (0, 0))],
        out_specs=pl.BlockSpec((tm, d), lambda i, f: (i, 0)),
        out_shape=jax.ShapeDtypeStruct((t, d), F32),
        compiler_params=pltpu.CompilerParams(
            dimension_semantics=("arbitrary", "arbitrary"), vmem_limit_bytes=WIDE_VMEM_LIMIT),
        name="mlp_ln",
    )(x_b, x_f, w1, b1.reshape(1, dff), w2, b2.reshape(1, d), ln_g.reshape(1, d), ln_b.reshape(1, d))


def _layer(x_rows, batch, seq, alpha, layer, w_in, pool_w, pool_scale, conv_w, conv_b, lru_wa, lru_ba, lru_wx,
           lru_bx, lru_lambda, w_pool_up, w_lru_up, w_out, b_out, ln1_g, ln1_b, w_ff1, b_ff1, w_ff2, b_ff2,
           ln2_g, ln2_b):
    pool_width = pool_w.shape[0] * pool_w.shape[1]
    lru_width = conv_w.shape[1]
    o1, o2, o3 = pool_width, pool_width + lru_width, pool_width + 2 * lru_width

    z_a, = _mm_act(x_rows, w_in, layer, 0, o2, "none", seg=(batch, seq))
    g_act, = _mm_act(x_rows, w_in, layer, o3, w_in.shape[2] - o3, "sigmoid")
    gate_act, w_pool_up_b, w_lru_up_b, w_out_b = _mm_act(
        x_rows, w_in, layer, o2, o3 - o2, "gelu", casts=(w_pool_up, w_lru_up, w_out))

    y_pool = _pool_branch(z_a, pool_w.astype(BF16), pool_scale, batch, seq)

    w_gates = jnp.concatenate([lru_wa[0], lru_wx[0], lru_wa[1], lru_wx[1]], axis=-1).astype(BF16)
    params = jnp.concatenate(
        [conv_w, conv_b[None], lru_ba[0][None], lru_bx[0][None], lru_lambda[0][None],
         lru_ba[1][None], lru_bx[1][None], lru_lambda[1][None]], axis=0).astype(F32)
    params = jnp.pad(params, ((0, _P_ROWS - params.shape[0]), (0, 0)))
    y_lru = _lru_branch(z_a, w_gates, params, batch, seq, o1)

    t = batch * seq
    x1_f, x1_b = _merge(y_pool.reshape(t, -1), y_lru.reshape(t, -1), gate_act, g_act, x_rows,
                        w_pool_up_b, w_lru_up_b, w_out_b,
                        b_out, ln1_g, ln1_b, alpha, batch, seq)
    return _mlp(x1_b, x1_f, w_ff1, layer, b_ff1, w_ff2, b_ff2, ln2_g, ln2_b, alpha)


def kernel(x, w_in, pool_w, pool_scale, conv_w, conv_b, lru_wa, lru_ba, lru_wx, lru_bx, lru_lambda, w_pool_up, w_lru_up, w_out, b_out, ln1_g, ln1_b, w_ff1, b_ff1, w_ff2, b_ff2, ln2_g, ln2_b):
    batch, seq, d = x.shape
    depth = w_in.shape[0]
    alpha = (2.0 * depth) ** 0.25
    rows = x.reshape(batch * seq, d)
    for l in range(depth):
        rows = _layer(rows, batch, seq, alpha, l, w_in, pool_w[l], pool_scale[l], conv_w[l], conv_b[l],
                      lru_wa[l], lru_ba[l], lru_wx[l], lru_bx[l], lru_lambda[l], w_pool_up[l], w_lru_up[l],
                      w_out[l], b_out[l], ln1_g[l], ln1_b[l], w_ff1, b_ff1[l], w_ff2, b_ff2[l],
                      ln2_g[l], ln2_b[l])
    return rows.reshape(batch, seq, d)
```

```python
import functools

import jax
import jax.numpy as jnp
from jax import lax
from jax.experimental import pallas as pl
from jax.experimental.pallas import tpu as pltpu

SUBLANES = 8
LANES = 128
NSEG = SUBLANES

POOL_WINDOWS = (2, 4, 8, 16)
LRU_BLOCK = 256
CONV_WIDTH = 4
CONV_LEFT = CONV_WIDTH // 2
LRU_C = 8.0
LN_EPS = 1e-5
HALO = 8
REPLAY_CHUNKS = 2
COPY_ROWS = 512
LOG2_E = 1.4426950408889634
LN_2 = 0.6931471805599453
TINY = 1.1754944e-38

VMEM_LIMIT = 60 * 1024 * 1024
WIDE_VMEM_LIMIT = 62 * 1024 * 1024

F32 = jnp.float32
BF16 = jnp.bfloat16


def _sigmoid(v):
    return 0.5 * jnp.tanh(0.5 * v) + 0.5


def _gelu_tanh(v):
    c = 0.7978845608028654
    return 0.5 * v * (1.0 + jnp.tanh(c * (v + 0.044715 * (v * v * v))))


def _layer_norm(y, g, b):
    mu = jnp.mean(y, axis=-1, keepdims=True)
    yc = y - mu
    var = jnp.mean(yc * yc, axis=-1, keepdims=True)
    return yc * lax.rsqrt(var + LN_EPS) * g + b


def _mm_act_kernel(*refs, act, n_casts, seg):
    x_ref, w_ref = refs[:2]
    cast_in = refs[2:2 + n_casts]
    o_ref = refs[2 + n_casts]
    cast_out = refs[3 + n_casts:3 + 2 * n_casts]
    w_bf = refs[-1]

    @pl.when(pl.program_id(1) == 0)
    def _():
        w_bf[...] = w_ref[0].astype(BF16)

    if seg:
        tj = x_ref.shape[2]
        x_tile = jnp.swapaxes(x_ref[0], 0, 1).reshape(NSEG * tj, x_ref.shape[3])
    else:
        x_tile = x_ref[...]
    acc = jnp.dot(x_tile.astype(BF16), w_bf[...], preferred_element_type=F32)
    if act == "gelu":
        acc = _gelu_tanh(acc)
    elif act == "sigmoid":
        acc = _sigmoid(acc)
    if seg:
        for q in range(o_ref.shape[0]):
            o_ref[q] = acc[:, q * LANES:(q + 1) * LANES]
    else:
        o_ref[...] = acc.astype(o_ref.dtype)
    for src, dst in zip(cast_in, cast_out):
        dst[...] = src[...].astype(BF16)


def _mm_act(x, w, layer, col0, n, act, casts=(), seg=None, tm=512, tn=2048):
    m, k = x.shape
    j0 = col0 // tn
    n_i = m // tm
    steps = (n // tn) * n_i
    cast_specs = [pl.BlockSpec((a.shape[0] // steps, a.shape[1]), lambda j, i: (j * n_i + i, 0)) for a in casts]
    if seg:
        batch, seq = seg
        tj = tm // NSEG
        n_jb = seq // NSEG // tj
        x = x.reshape(batch, NSEG, seq // NSEG, k)
        x_spec = pl.BlockSpec((1, NSEG, tj, k), lambda j, i: (i // n_jb, 0, i % n_jb, 0))
        o_spec = pl.BlockSpec((tn // LANES, tm, LANES), lambda j, i: (j, i, 0))
        o_shape = jax.ShapeDtypeStruct((n // LANES, m, LANES), F32)
    else:
        x_spec = pl.BlockSpec((tm, k), lambda j, i: (i, 0))
        o_spec = pl.BlockSpec((tm, tn), lambda j, i: (i, j))
        o_shape = jax.ShapeDtypeStruct((m, n), F32)
    return pl.pallas_call(
        functools.partial(_mm_act_kernel, act=act, n_casts=len(casts), seg=bool(seg)),
        grid=(n // tn, n_i),
        in_specs=[x_spec, pl.BlockSpec((1, k, tn), lambda j, i: (layer, 0, j0 + j))] + cast_specs,
        out_specs=[o_spec] + cast_specs,
        out_shape=[o_shape] + [jax.ShapeDtypeStruct(a.shape, BF16) for a in casts],
        scratch_shapes=[pltpu.VMEM((k, tn), BF16)],
        compiler_params=pltpu.CompilerParams(
            dimension_semantics=("arbitrary", "arbitrary"),
            vmem_limit_bytes=VMEM_LIMIT),
        name=f"mm_in_{act}",
    )(x, w, *casts)


def _load_slabs(buf, r, n):
    return jnp.concatenate([buf[q, pl.ds(r, n), :] for q in range(buf.shape[0])], axis=1)


def _store_slabs(buf, r, n, val):
    for q in range(buf.shape[0]):
        buf[q, pl.ds(r, n), :] = val[:, q * LANES:(q + 1) * LANES]


def _fill_ext(ext, u_ref, seq):
    h_rows = HALO * SUBLANES
    nq = ext.shape[0]

    def copy(bi, carry):
        r0 = pl.multiple_of(bi * COPY_ROWS, COPY_ROWS)
        for q in range(nq):
            ext[q, pl.ds(h_rows + r0, COPY_ROWS), :] = u_ref[q, pl.ds(r0, COPY_ROWS), :]
        return carry

    lax.fori_loop(0, seq // COPY_ROWS, copy, 0)

    sub = lax.broadcasted_iota(jnp.int32, (SUBLANES, LANES), 0)
    for q in range(nq):
        for m in range(HALO):
            nxt = pltpu.roll(u_ref[q, pl.ds(m * SUBLANES, SUBLANES), :], SUBLANES - 1, 0)
            ext[q, pl.ds(h_rows + seq + m * SUBLANES, SUBLANES), :] = jnp.where(sub == SUBLANES - 1, 0.0, nxt)
            prv = pltpu.roll(u_ref[q, pl.ds(seq - (m + 1) * SUBLANES, SUBLANES), :], 1, 0)
            ext[q, pl.ds(h_rows - (m + 1) * SUBLANES, SUBLANES), :] = jnp.where(sub == 0, 0.0, prv)


def _pool_kernel(u_ref, w_ref, s_ref, o_ref, ext, *, seq, chunk):
    seg_len = seq // NSEG
    h_rows = HALO * SUBLANES
    nq = ext.shape[0]
    _fill_ext(ext, u_ref, seq)
    grp = pl.program_id(1)
    n_chunks = seq // chunk

    def body(win):
        half = win // 2

        def do_chunk(ci, carry=0):
            clipped = isinstance(ci, int)
            r0 = ci * chunk if clipped else pl.multiple_of(ci * chunk, chunk)
            base = r0 + h_rows
            if clipped:
                row = r0 + lax.broadcasted_iota(jnp.int32, (chunk, LANES), 0)
                t = (row & (SUBLANES - 1)) * seg_len + (row >> 3)
                cnt = jnp.minimum(t + half, seq) - jnp.maximum(t - half, 0)
                inv = 1.0 / cnt.astype(F32)
            else:
                inv = 1.0 / win
            d = []
            for q in range(nq):
                span_rows = (win - 1) * SUBLANES
                rows_in = ext[q, pl.ds(base - half * SUBLANES, chunk + span_rows), :]
                tot = rows_in
                step = SUBLANES
                while step <= half * SUBLANES:
                    keep = tot.shape[0] - step
                    tot = tot[:keep] + tot[step:step + keep]
                    step *= 2
                d.append(tot * inv - rows_in[half * SUBLANES:half * SUBLANES + chunk])
            d = jnp.concatenate(d, axis=1)
            y = jnp.dot(d.astype(BF16), w_ref[0], preferred_element_type=F32) * s_ref[0]
            o_ref[0, pl.ds(r0, chunk), :] = y.astype(o_ref.dtype)
            return carry

        assert chunk // SUBLANES >= half and n_chunks >= 2
        do_chunk(0)
        lax.fori_loop(1, n_chunks - 1, do_chunk, 0)
        do_chunk(n_chunks - 1)

    for gi, win in enumerate(POOL_WINDOWS):
        pl.when(grp == gi)(functools.partial(body, win))


def _pool_branch(z_seg, pool_w, pool_scale, batch, seq, chunk=512):
    n_groups, gw = pool_w.shape[0], pool_w.shape[1]
    nq = gw // LANES
    return pl.pallas_call(
        functools.partial(_pool_kernel, seq=seq, chunk=chunk),
        grid=(batch, n_groups),
        in_specs=[pl.BlockSpec((nq, seq, LANES), lambda b, g: (g, b, 0)),
                  pl.BlockSpec((1, gw, gw), lambda b, g: (g, 0, 0)),
                  pl.BlockSpec((1, 1, gw), lambda b, g: (g, 0, 0))],
        out_specs=pl.BlockSpec((1, seq, gw), lambda b, g: (b, 0, g)),
        out_shape=jax.ShapeDtypeStruct((batch, seq, n_groups * gw), BF16),
        scratch_shapes=[pltpu.VMEM((nq, seq + 2 * HALO * SUBLANES, LANES), F32)],
        compiler_params=pltpu.CompilerParams(
            dimension_semantics=("arbitrary", "arbitrary"),
            vmem_limit_bytes=VMEM_LIMIT),
        name="pool_branch",
    )(z_seg, pool_w, pool_scale.reshape(n_groups, 1, gw))


_P_CONV_W = 0
_P_CONV_B = CONV_WIDTH
_P_DIR = CONV_WIDTH + 1
_P_ROWS = 16


def _lru_kernel(u_ref, w_ref, p_ref, o_ref, ext, a_f, x_f, a_b, x_b, summ, ent, *, seq, chunk):
    h_rows = HALO * SUBLANES
    c = LRU_BLOCK
    n_chunks = seq // chunk
    groups = chunk // SUBLANES
    _fill_ext(ext, u_ref, seq)

    conv_hw = [0.5 * p_ref[pl.ds(_P_CONV_W + k, 1), :] for k in range(CONV_WIDTH)]
    conv_hb = 0.5 * p_ref[pl.ds(_P_CONV_B, 1), :]
    hb_a, hb_x, c2 = [], [], []
    for d in range(2):
        hb_a.append(0.5 * p_ref[pl.ds(_P_DIR + 3 * d, 1), :])
        hb_x.append(0.5 * p_ref[pl.ds(_P_DIR + 3 * d + 1, 1), :])
        lam = p_ref[pl.ds(_P_DIR + 3 * d + 2, 1), :]
        c2.append((-0.5 * LRU_C * LOG2_E) * jax.nn.softplus(-lam))
    a_out = (a_f, a_b)
    x_out = (x_f, x_b)

    def rows(v, k):
        return v[k * SUBLANES:(k + 1) * SUBLANES]

    def summary_rows(ci):
        return pl.ds(pl.multiple_of(ci * SUBLANES, SUBLANES), SUBLANES)

    zeros = jnp.zeros((SUBLANES, c), F32)
    ones = jnp.ones((SUBLANES, c), F32)

    def gates(ci, carry):
        r0 = pl.multiple_of(ci * chunk, chunk)
        base = r0 + h_rows
        xh = conv_hb + _load_slabs(ext, base - CONV_LEFT * SUBLANES, chunk) * conv_hw[0]
        for k in range(1, CONV_WIDTH):
            xh = xh + _load_slabs(ext, base + (k - CONV_LEFT) * SUBLANES, chunk) * conv_hw[k]
        pre = jnp.dot(xh.astype(BF16), w_ref[0], preferred_element_type=F32)
        for d in range(2):
            t_r = jnp.tanh(pre[:, (2 * d) * c:(2 * d + 1) * c] + hb_a[d])
            t_i = jnp.tanh(pre[:, (2 * d + 1) * c:(2 * d + 2) * c] + hb_x[d])
            log2_a = c2[d] * t_r + c2[d]
            a = jnp.exp2(log2_a)
            one_m_a2 = (-1.0 - a * a) * jnp.tanh(LN_2 * log2_a)
            root = one_m_a2 * lax.rsqrt(jnp.maximum(one_m_a2, TINY))
            inp = root * (t_i * xh + xh)
            _store_slabs(a_out[d], r0, chunk, a)
            _store_slabs(x_out[d], r0, chunk, inp)
            h, p = zeros, ones
            for k in (range(groups) if d == 0 else range(groups - 1, -1, -1)):
                h = rows(a, k) * h + rows(inp, k)
                p = rows(a, k) * p
            summ[2 * d, summary_rows(ci), :] = h
            summ[2 * d + 1, summary_rows(ci), :] = p
        return carry

    lax.fori_loop(0, n_chunks, gates, 0)

    def summary(idx, ci):
        return summ[idx, ci * SUBLANES:(ci + 1) * SUBLANES, :]

    h_f, p_f, h_b, p_b = zeros, ones, zeros, ones
    for ci in range(n_chunks):
        cb = n_chunks - 1 - ci
        h_f = summary(0, ci) + summary(1, ci) * h_f
        p_f = summary(1, ci) * p_f
        h_b = summary(2, cb) + summary(3, cb) * h_b
        p_b = summary(3, cb) * p_b
    sub = lax.broadcasted_iota(jnp.int32, (SUBLANES, c), 0)
    e_f = zeros
    e_b = zeros
    for _ in range(NSEG - 1):
        e_f = jnp.where(sub == 0, 0.0, pltpu.roll(h_f + p_f * e_f, 1, 0))
        e_b = jnp.where(sub == SUBLANES - 1, 0.0, pltpu.roll(h_b + p_b * e_b, SUBLANES - 1, 0))
    for ci in range(n_chunks):
        cb = n_chunks - 1 - ci
        ent[0, ci * SUBLANES:(ci + 1) * SUBLANES, :] = e_f
        ent[1, cb * SUBLANES:(cb + 1) * SUBLANES, :] = e_b
        e_f = summary(0, ci) + summary(1, ci) * e_f
        e_b = summary(2, cb) + summary(3, cb) * e_b

    def replay(it, carry):
        for sub_i in range(REPLAY_CHUNKS):
            ci = it * REPLAY_CHUNKS + sub_i
            r0 = pl.multiple_of(ci * chunk, chunk)
            af, xf = _load_slabs(a_f, r0, chunk), _load_slabs(x_f, r0, chunk)
            ab, xb = _load_slabs(a_b, r0, chunk), _load_slabs(x_b, r0, chunk)
            h = ent[0, summary_rows(ci), :]
            fwd = []
            for k in range(groups):
                h = rows(af, k) * h + rows(xf, k)
                fwd.append(h)
            h = ent[1, summary_rows(ci), :]
            out = [None] * groups
            for k in range(groups - 1, -1, -1):
                h = rows(ab, k) * h + rows(xb, k)
                out[k] = h + fwd[k]
            o_ref[0, pl.ds(r0, chunk), :] = jnp.concatenate(out, axis=0).astype(o_ref.dtype)
        return carry

    lax.fori_loop(0, n_chunks // REPLAY_CHUNKS, replay, 0)


def _lru_branch(z_seg, w_gates, params, batch, seq, lru_col0, chunk=512):
    n_heads = w_gates.shape[0]
    c = LRU_BLOCK
    nq = c // LANES
    z3 = z_seg
    col0 = lru_col0 // c
    n_chunks = seq // chunk
    seg_buf = pltpu.VMEM((nq, seq, LANES), F32)
    return pl.pallas_call(
        functools.partial(_lru_kernel, seq=seq, chunk=chunk),
        grid=(batch, n_heads),
        in_specs=[pl.BlockSpec((nq, seq, LANES), lambda b, h: (col0 + h, b, 0)),
                  pl.BlockSpec((1, c, 4 * c), lambda b, h: (h, 0, 0)),
                  pl.BlockSpec((_P_ROWS, c), lambda b, h: (0, h))],
        out_specs=pl.BlockSpec((1, seq, c), lambda b, h: (b, 0, h)),
        out_shape=jax.ShapeDtypeStruct((batch, seq, n_heads * c), F32),
        scratch_shapes=[pltpu.VMEM((nq, seq + 2 * HALO * SUBLANES, LANES), F32)]
        + [seg_buf] * 4
        + [pltpu.VMEM((4, n_chunks * SUBLANES, c), F32), pltpu.VMEM((2, n_chunks * SUBLANES, c), F32)],
        compiler_params=pltpu.CompilerParams(
            dimension_semantics=("arbitrary", "arbitrary"),
            vmem_limit_bytes=VMEM_LIMIT),
        name="lru_branch",
    )(z3, w_gates, params)


def _merge_kernel(yp_ref, yl_ref, gate_ref, g0_ref, g1_ref, x_ref, wp_hbm, wl_hbm, wo_hbm, bo_ref, g_ref, b_ref,
                  of_ref, ob_ref, wp_ref, wl_ref, wo_ref, sem, *, alpha):
    _, n_seg, tj, d = x_ref.shape
    tm = n_seg * tj
    first = pl.program_id(0) == 0
    copies = [pltpu.make_async_copy(src, dst, sem.at[k])
              for k, (src, dst) in enumerate(((wp_hbm, wp_ref), (wl_hbm, wl_ref), (wo_hbm, wo_ref)))]

    def to_time(v):
        return jnp.swapaxes(v.reshape(tj, n_seg, d), 0, 1).reshape(tm, d)

    def body(load_weights):
        if load_weights:
            for cp in copies:
                cp.start()
        y_lru = (to_time(yl_ref[...]) * gate_ref[0].reshape(tm, d)).astype(BF16)
        y_pool = to_time(yp_ref[...].astype(F32)).astype(BF16)
        if load_weights:
            copies[0].wait()
        up_p = jnp.dot(y_pool, wp_ref[...], preferred_element_type=F32)
        if load_weights:
            copies[1].wait()
        up_l = jnp.dot(y_lru, wl_ref[...], preferred_element_type=F32)
        m = g0_ref[0].reshape(tm, d) * up_p + g1_ref[0].reshape(tm, d) * up_l
        if load_weights:
            copies[2].wait()
        mix = jnp.dot(m.astype(BF16), wo_ref[...], preferred_element_type=F32) + bo_ref[...]
        y = _layer_norm(alpha * x_ref[0].reshape(tm, d) + mix, g_ref[...], b_ref[...])
        of_ref[0] = y.reshape(n_seg, tj, d)
        ob_ref[0] = y.astype(BF16).reshape(n_seg, tj, d)

    pl.when(first)(functools.partial(body, True))
    pl.when(jnp.logical_not(first))(functools.partial(body, False))


def _merge(y_pool, y_lru, gate_act, g_act, x_rows, w_pool_up, w_lru_up, w_out, b_out, ln_g, ln_b, alpha, batch, seq,
           tm=256):
    t, d = x_rows.shape
    seg_len = seq // NSEG
    tj = tm // NSEG
    n_jb = seg_len // tj
    row = lambda i: (i, 0)
    const = lambda i: (0, 0)
    wspec = pl.BlockSpec(memory_space=pl.ANY)
    vspec = pl.BlockSpec((1, d), const)

    def tspec(col):
        return pl.BlockSpec((1, NSEG, tj, d), lambda i: (i // n_jb, 0, i % n_jb, col))

    g4 = g_act.reshape(batch, NSEG, seg_len, g_act.shape[1])
    x1_f, x1_b = pl.pallas_call(
        functools.partial(_merge_kernel, alpha=alpha),
        grid=(t // tm,),
        in_specs=[pl.BlockSpec((tm, d), row), pl.BlockSpec((tm, d), row),
                  tspec(0), tspec(0), tspec(1), tspec(0), wspec, wspec, wspec, vspec, vspec, vspec],
        out_specs=[tspec(0), tspec(0)],
        out_shape=[jax.ShapeDtypeStruct((batch, NSEG, seg_len, d), F32),
                   jax.ShapeDtypeStruct((batch, NSEG, seg_len, d), BF16)],
        scratch_shapes=[pltpu.VMEM((d, d), BF16)] * 3 + [pltpu.SemaphoreType.DMA((3,))],
        compiler_params=pltpu.CompilerParams(
            dimension_semantics=("arbitrary",), vmem_limit_bytes=WIDE_VMEM_LIMIT),
        name="merge_out_ln",
    )(y_pool, y_lru, gate_act.reshape(batch, NSEG, seg_len, d), g4, g4, x_rows.reshape(batch, NSEG, seg_len, d),
      w_pool_up, w_lru_up, w_out,
      b_out.reshape(1, d), ln_g.reshape(1, d), ln_b.reshape(1, d))
    return x1_f.reshape(t, d), x1_b.reshape(t, d)


def _mlp_kernel(xb_ref, xf_ref, w1_ref, b1_ref, w2_ref, b2_ref, g_ref, b_ref, o_ref, *, alpha):
    f = pl.program_id(1)

    def contribution():
        h = jnp.dot(xb_ref[...], w1_ref[0].astype(BF16), preferred_element_type=F32) + b1_ref[...]
        h = jnp.square(jnp.maximum(h, 0.0))
        return jnp.dot(h.astype(BF16), w2_ref[0].astype(BF16), preferred_element_type=F32)

    @pl.when(f == 0)
    def _():
        o_ref[...] = (alpha * xf_ref[...] + b2_ref[...]) + contribution()

    @pl.when(f > 0)
    def _():
        o_ref[...] += contribution()

    @pl.when(f == pl.num_programs(1) - 1)
    def _():
        o_ref[...] = _layer_norm(o_ref[...], g_ref[...], b_ref[...])


def _mlp(x_b, x_f, w1, layer, b1, w2, b2, ln_g, ln_b, alpha, tm=1024, tf=512):
    t, d = x_f.shape
    dff = w1.shape[2]
    return pl.pallas_call(
        functools.partial(_mlp_kernel, alpha=alpha),
        grid=(t // tm, dff // tf),
        in_specs=[pl.BlockSpec((tm, d), lambda i, f: (i, 0)),
                  pl.BlockSpec((tm, d), lambda i, f: (i, 0)),
                  pl.BlockSpec((1, d, tf), lambda i, f: (layer, 0, f)),
                  pl.BlockSpec((1, tf), lambda i, f: (0, f)),
                  pl.BlockSpec((1, tf, d), lambda i, f: (layer, f, 0)),
                  pl.BlockSpec((1, d), lambda i, f: (0, 0)),
                  pl.BlockSpec((1, d), lambda i, f: (0, 0)),
                  pl.BlockSpec((1, d), lambda i, f: (0, 0))],
        out_specs=pl.BlockSpec((tm, d), lambda i, f: (i, 0)),
        out_shape=jax.ShapeDtypeStruct((t, d), F32),
        compiler_params=pltpu.CompilerParams(
            dimension_semantics=("arbitrary", "arbitrary"), vmem_limit_bytes=WIDE_VMEM_LIMIT),
        name="mlp_ln",
    )(x_b, x_f, w1, b1.reshape(1, dff), w2, b2.reshape(1, d), ln_g.reshape(1, d), ln_b.reshape(1, d))


def _layer(x_rows, batch, seq, alpha, layer, w_in, pool_w, pool_scale, conv_w, conv_b, lru_wa, lru_ba, lru_wx,
           lru_bx, lru_lambda, w_pool_up, w_lru_up, w_out, b_out, ln1_g, ln1_b, w_ff1, b_ff1, w_ff2, b_ff2,
           ln2_g, ln2_b):
    pool_width = pool_w.shape[0] * pool_w.shape[1]
    lru_width = conv_w.shape[1]
    o1, o2, o3 = pool_width, pool_width + lru_width, pool_width + 2 * lru_width

    z_a, = _mm_act(x_rows, w_in, layer, 0, o2, "none", seg=(batch, seq))
    g_act, = _mm_act(x_rows, w_in, layer, o3, w_in.shape[2] - o3, "sigmoid")
    gate_act, w_pool_up_b, w_lru_up_b, w_out_b = _mm_act(
        x_rows, w_in, layer, o2, o3 - o2, "gelu", casts=(w_pool_up, w_lru_up, w_out))

    y_pool = _pool_branch(z_a, pool_w.astype(BF16), pool_scale, batch, seq)

    w_gates = jnp.concatenate([lru_wa[0], lru_wx[0], lru_wa[1], lru_wx[1]], axis=-1).astype(BF16)
    params = jnp.concatenate(
        [conv_w, conv_b[None], lru_ba[0][None], lru_bx[0][None], lru_lambda[0][None],
         lru_ba[1][None], lru_bx[1][None], lru_lambda[1][None]], axis=0).astype(F32)
    params = jnp.pad(params, ((0, _P_ROWS - params.shape[0]), (0, 0)))
    y_lru = _lru_branch(z_a, w_gates, params, batch, seq, o1)

    t = batch * seq
    x1_f, x1_b = _merge(y_pool.reshape(t, -1), y_lru.reshape(t, -1), gate_act, g_act, x_rows,
                        w_pool_up_b, w_lru_up_b, w_out_b,
                        b_out, ln1_g, ln1_b, alpha, batch, seq)
    return _mlp(x1_b, x1_f, w_ff1, layer, b_ff1, w_ff2, b_ff2, ln2_g, ln2_b, alpha)


def kernel(x, w_in, pool_w, pool_scale, conv_w, conv_b, lru_wa, lru_ba, lru_wx, lru_bx, lru_lambda, w_pool_up, w_lru_up, w_out, b_out, ln1_g, ln1_b, w_ff1, b_ff1, w_ff2, b_ff2, ln2_g, ln2_b):
    batch, seq, d = x.shape
    depth = w_in.shape[0]
    alpha = (2.0 * depth) ** 0.25
    rows = x.reshape(batch * seq, d)
    for l in range(depth):
        rows = _layer(rows, batch, seq, alpha, l, w_in, pool_w[l], pool_scale[l], conv_w[l], conv_b[l],
                      lru_wa[l], lru_ba[l], lru_wx[l], lru_bx[l], lru_lambda[l], w_pool_up[l], w_lru_up[l],
                      w_out[l], b_out[l], ln1_g[l], ln1_b[l], w_ff1, b_ff1[l], w_ff2, b_ff2[l],
                      ln2_g[l], ln2_b[l])
    return rows.reshape(batch, seq, d)
```
